```python
import jax, jax.numpy as jnp
from jax import lax
import numpy as np

D_MODEL = 2048
BATCH = 2
SEQ = 4096
DEPTH = 2
DEC_BATCH = 8
DEC_SEQ = 32
PAST_LEN = 2048

CHUNK = 64
D_MIX = D_MODEL
HEAD_DIM = 64
SB_WIDTH = 3 * D_MIX // 8
SB_HEADS = SB_WIDTH // HEAD_DIM
SB_BLOCK = 128
POOL_WIDTH = D_MIX // 4
POOL_WINDOWS = (2, 4, 8, 16)
POOL_GROUPS = len(POOL_WINDOWS)
POOL_GC = POOL_WIDTH // POOL_GROUPS
POOL_HIST = max(POOL_WINDOWS) - 1
RW_WIDTH = D_MIX - SB_WIDTH - POOL_WIDTH
RW_HEADS = RW_WIDTH // HEAD_DIM
RW_DECAY_LORA = 64
RW_AAA_LORA = 64
RW_GATE_LORA = 128
RW_COLS = 3 * RW_WIDTH + RW_DECAY_LORA + RW_AAA_LORA + RW_GATE_LORA
IN_COLS = 3 * SB_WIDTH + POOL_WIDTH + RW_COLS
IN_SPLITS = (SB_WIDTH, 2 * SB_WIDTH, 3 * SB_WIDTH, 3 * SB_WIDTH + POOL_WIDTH)
RW_SPLITS = (RW_WIDTH, 2 * RW_WIDTH, 3 * RW_WIDTH, 3 * RW_WIDTH + RW_DECAY_LORA,
             3 * RW_WIDTH + RW_DECAY_LORA + RW_AAA_LORA)
N_EXPERTS = 32
TOP_K = 4
D_EXPERT = D_MODEL
SWIGLU_LIMIT = 7.0
SWIGLU_ALPHA = 1.702
MOE_BLOCK = 128
DN_ALPHA = (2.0 * DEPTH) ** 0.25
DN_BETA = (8.0 * DEPTH) ** -0.25
LN_EPS = 1e-5
GN_EPS = 64e-5

kernel_name = 'hybrid_stickbreak_pool_rwkv7_moe_stream'


def layer_norm(x, g, b):
    xf = x.astype(jnp.float32)
    mu = jnp.mean(xf, -1, keepdims=True)
    var = jnp.mean(jnp.square(xf - mu), -1, keepdims=True)
    return ((xf - mu) * lax.rsqrt(var + LN_EPS) * g + b).astype(x.dtype)


def _sb_block(q, q_pos, k, v, k_pos):
    z = jnp.einsum('bqhd,bkhd->bhqk', q.astype(jnp.float32), k.astype(jnp.float32)) * (HEAD_DIM ** -0.5)
    mask = k_pos[None, :] < q_pos[:, None]
    log_stay = jnp.where(mask, jax.nn.log_sigmoid(-z), 0.0)
    after = lax.cumsum(log_stay, axis=3, reverse=True) - log_stay
    att = jnp.where(mask, jnp.exp(jax.nn.log_sigmoid(z) + after), 0.0)
    return jnp.einsum('bhqk,bkhd->bqhd', att, v.astype(jnp.float32)).astype(q.dtype)


def sb_attention(q, k, v, q_pos, k_pos):
    b, t, h, dh = q.shape
    if t <= SB_BLOCK:
        return _sb_block(q, q_pos, k, v, k_pos)
    nb = t // SB_BLOCK
    qb = jnp.moveaxis(q.reshape(b, nb, SB_BLOCK, h, dh), 1, 0)
    pb = q_pos.reshape(nb, SB_BLOCK)
    ob = lax.map(lambda a: _sb_block(a[0], a[1], k, v, k_pos), (qb, pb))
    return jnp.moveaxis(ob, 0, 1).reshape(b, t, h, dh)


def pool_mixer(u, hist, pos0, w_pool, scale):
    b, t, c = u.shape
    u_ext = jnp.concatenate([hist.astype(jnp.float32), u.astype(jnp.float32)], axis=1)
    cs = jnp.concatenate([jnp.zeros((b, 1, c), jnp.float32), jnp.cumsum(u_ext, axis=1)], axis=1)
    pos = pos0 + jnp.arange(t, dtype=jnp.int32)
    end = cs[:, POOL_HIST + 1:]
    means = []
    for gi, w in enumerate(POOL_WINDOWS):
        sl = slice(gi * POOL_GC, (gi + 1) * POOL_GC)
        start = cs[:, POOL_HIST + 1 - w: POOL_HIST + 1 - w + t, sl]
        cnt = jnp.minimum(pos + 1, w).astype(jnp.float32)[None, :, None]
        means.append((end[..., sl] - start) / cnt)
    pooled = jnp.concatenate(means, axis=-1) - u.astype(jnp.float32)
    y = jnp.einsum('btgc,gce->btge', pooled.reshape(b, t, POOL_GROUPS, POOL_GC), w_pool).reshape(b, t, c) * scale
    return y.astype(u.dtype), u_ext[:, -POOL_HIST:].astype(hist.dtype)


def rwkv7_mixer(c, shift_hist, wkv0, mu, w0, w2, a0, a2, g2, kk_scale, ka, rk, gn_g, gn_b):
    b, t, _ = c.shape
    cf = c.astype(jnp.float32)
    prev = jnp.concatenate([shift_hist.astype(jnp.float32), cf[:, :-1]], axis=1)
    xs = cf + (prev - cf) * mu
    r, k, v, wd, ad, gd = jnp.split(xs, RW_SPLITS, axis=-1)
    log_w = -jnp.exp(-jax.nn.softplus(-(w0 + jnp.tanh(wd) @ w2)) - 0.5)
    a = jax.nn.sigmoid(a0 + ad @ a2)
    g = jax.nn.sigmoid(gd) @ g2
    heads = lambda z: z.reshape(b, t, RW_HEADS, HEAD_DIM)
    kk = heads(k * kk_scale)
    kk = kk * lax.rsqrt(jnp.maximum(jnp.sum(kk * kk, -1, keepdims=True), 1e-24))
    k = k * (1.0 + (a - 1.0) * ka)
    r_h, k_h, v_h, a_h, decay = heads(r), heads(k), heads(v), heads(a), heads(jnp.exp(log_w))

    def step(s, inp):
        r_t, d_t, k_t, v_t, kk_t, a_t = inp
        s_kk = jnp.einsum('bhij,bhj->bhi', s, kk_t)
        s = (s * d_t[:, :, None, :] - s_kk[..., None] * (kk_t * a_t)[:, :, None, :]
             + v_t[..., None] * k_t[:, :, None, :])
        return s, jnp.einsum('bhij,bhj->bhi', s, r_t)

    seqs = tuple(jnp.moveaxis(z, 1, 0) for z in (r_h, decay, k_h, v_h, kk, a_h))
    s_last, o = lax.scan(step, wkv0.astype(jnp.float32), seqs)
    o = jnp.moveaxis(o, 0, 1)
    mu_o = jnp.mean(o, -1, keepdims=True)
    var_o = jnp.mean(jnp.square(o - mu_o), -1, keepdims=True)
    o = ((o - mu_o) * lax.rsqrt(var_o + GN_EPS)).reshape(b, t, RW_WIDTH) * gn_g + gn_b
    bonus = (jnp.sum(r_h * k_h * rk, -1, keepdims=True) * v_h).reshape(b, t, RW_WIDTH)
    out = (o + bonus) * g
    return out.astype(c.dtype), c[:, -1:].astype(shift_hist.dtype), s_last.astype(wkv0.dtype)


def moe_ffn(x, w_router, b_router, w1, b1, w2, b2):
    b, t, d = x.shape
    xt = x.reshape(b * t, d)
    n = b * t
    nk = n * TOP_K
    logits = xt.astype(jnp.float32) @ w_router.astype(jnp.float32) + b_router.astype(jnp.float32)
    top_v, top_i = lax.top_k(logits, TOP_K)
    gates = jax.nn.softmax(top_v, axis=-1)
    flat_e = top_i.reshape(nk)
    order = jnp.argsort(flat_e)
    sorted_e = flat_e[order]
    sorted_tok = order // TOP_K
    counts = jnp.bincount(flat_e, length=N_EXPERTS)
    padded = (counts + MOE_BLOCK - 1) // MOE_BLOCK * MOE_BLOCK
    pad_end = jnp.cumsum(padded)
    pad_start = pad_end - padded
    start = jnp.cumsum(counts) - counts
    dest = pad_start[sorted_e] + jnp.arange(nk, dtype=jnp.int32) - start[sorted_e]
    n_blocks = (nk + N_EXPERTS * (MOE_BLOCK - 1)) // MOE_BLOCK
    rows = n_blocks * MOE_BLOCK
    buf_tok = jnp.zeros((rows,), jnp.int32).at[dest].set(sorted_tok.astype(jnp.int32))
    block_e = jnp.minimum(jnp.searchsorted(pad_end, jnp.arange(n_blocks) * MOE_BLOCK, side='right'),
                          N_EXPERTS - 1)
    xb = xt[buf_tok].reshape(n_blocks, MOE_BLOCK, d)

    def expert_block(args):
        xblk, e = args
        h = xblk @ w1[e] + b1[e]
        h_glu, h_lin = h[:, :D_EXPERT], h[:, D_EXPERT:]
        h_glu = jnp.minimum(h_glu, SWIGLU_LIMIT)
        h_lin = jnp.clip(h_lin, -SWIGLU_LIMIT, SWIGLU_LIMIT)
        act = h_glu * jax.nn.sigmoid(SWIGLU_ALPHA * h_glu) * (h_lin + 1.0)
        return act @ w2[e] + b2[e]

    yb = lax.map(expert_block, (xb, block_e)).reshape(rows, d)
    y_sorted = yb[dest].astype(jnp.float32) * gates.reshape(nk)[order][:, None]
    out = jax.ops.segment_sum(y_sorted, sorted_tok, num_segments=n)
    return out.reshape(b, t, d).astype(x.dtype)


def hybrid_layer(x, k_hist, v_hist, pool_hist, shift_hist, wkv0, lp):
    b, t, _ = x.shape
    past = k_hist.shape[1]
    h = jnp.einsum('btd,dc->btc', x, lp['w_in'])
    q, k, v, u, c = jnp.split(h, IN_SPLITS, axis=-1)
    q = q.reshape(b, t, SB_HEADS, HEAD_DIM)
    k = k.reshape(b, t, SB_HEADS, HEAD_DIM).astype(k_hist.dtype)
    v = v.reshape(b, t, SB_HEADS, HEAD_DIM).astype(v_hist.dtype)
    k_all = jnp.concatenate([k_hist, k], axis=1)
    v_all = jnp.concatenate([v_hist, v], axis=1)
    q_pos = past + jnp.arange(t, dtype=jnp.int32)
    k_pos = jnp.arange(past + t, dtype=jnp.int32)
    o_a = sb_attention(q, k_all, v_all, q_pos, k_pos).reshape(b, t, SB_WIDTH).astype(x.dtype)
    o_b, pool_new = pool_mixer(u, pool_hist, past, lp['pool_w'], lp['pool_scale'])
    o_c, shift_new, wkv_new = rwkv7_mixer(c, shift_hist, wkv0, lp['rw_mu'], lp['rw_w0'], lp['rw_w2'],
                                          lp['rw_a0'], lp['rw_a2'], lp['rw_g2'], lp['rw_kk'], lp['rw_ka'],
                                          lp['rw_rk'], lp['rw_gn_g'], lp['rw_gn_b'])
    mixed = jnp.concatenate([o_a, o_b.astype(x.dtype), o_c.astype(x.dtype)], axis=-1)
    mix = jnp.einsum('btc,cd->btd', mixed, lp['w_out'])
    x = layer_norm(DN_ALPHA * x + mix, lp['ln_mix_g'], lp['ln_mix_b'])
    ffn = moe_ffn(x, lp['w_router'], lp['b_router'], lp['w_e1'], lp['b_e1'], lp['w_e2'], lp['b_e2'])
    x = layer_norm(DN_ALPHA * x + ffn, lp['ln_ffn_g'], lp['ln_ffn_b'])
    return x, (k, v, pool_new, shift_new, wkv_new)


def setup_inputs(seed: int = 0) -> dict:
    key = jax.random.key(seed)
    ks = iter(jax.random.split(key, 40))
    nrm = lambda shape, scale: jax.random.normal(next(ks), shape, jnp.float32) * scale
    L, d = DEPTH, D_MODEL
    return {
        'x_prompt': nrm((BATCH, SEQ, d), 1.0),
        'x_sample': nrm((DEC_BATCH, DEC_SEQ, d), 1.0),
        'cache_k': nrm((L, DEC_BATCH, PAST_LEN, SB_HEADS, HEAD_DIM), 1.0),
        'cache_v': nrm((L, DEC_BATCH, PAST_LEN, SB_HEADS, HEAD_DIM), 1.0),
        'state_pool': nrm((L, DEC_BATCH, POOL_HIST, POOL_WIDTH), 1.0),
        'state_shift': nrm((L, DEC_BATCH, 1, RW_COLS), 1.0),
        'state_wkv': nrm((L, DEC_BATCH, RW_HEADS, HEAD_DIM, HEAD_DIM), 0.3),
        'w_in': nrm((L, d, IN_COLS), d ** -0.5),
        'w_out': nrm((L, D_MIX, d), DN_BETA * D_MIX ** -0.5),
        'ln_mix_g': 1.0 + nrm((L, d), 0.05),
        'ln_mix_b': nrm((L, d), 0.02),
        'ln_ffn_g': 1.0 + nrm((L, d), 0.05),
        'ln_ffn_b': nrm((L, d), 0.02),
        'pool_w': nrm((L, POOL_GROUPS, POOL_GC, POOL_GC), POOL_GC ** -0.5),
        'pool_scale': 1.0 + nrm((L, POOL_WIDTH), 0.1),
        'rw_mu': jax.random.uniform(next(ks), (L, RW_COLS), jnp.float32),
        'rw_w0': nrm((L, RW_WIDTH), 1.0) - 1.5,
        'rw_w2': nrm((L, RW_DECAY_LORA, RW_WIDTH), 0.5 * RW_DECAY_LORA ** -0.5),
        'rw_a0': nrm((L, RW_WIDTH), 0.1),
        'rw_a2': nrm((L, RW_AAA_LORA, RW_WIDTH), 0.5 * RW_AAA_LORA ** -0.5),
        'rw_g2': nrm((L, RW_GATE_LORA, RW_WIDTH), RW_GATE_LORA ** -0.5),
        'rw_kk': 1.0 + nrm((L, RW_WIDTH), 0.1),
        'rw_ka': 1.0 + nrm((L, RW_WIDTH), 0.1),
        'rw_rk': nrm((L, RW_HEADS, HEAD_DIM), 0.1),
        'rw_gn_g': 1.0 + nrm((L, RW_WIDTH), 0.05),
        'rw_gn_b': nrm((L, RW_WIDTH), 0.02),
        'w_router': nrm((L, d, N_EXPERTS), d ** -0.5),
        'b_router': nrm((L, N_EXPERTS), 0.01),
        'w_e1': nrm((L, N_EXPERTS, d, 2 * D_EXPERT), d ** -0.5),
        'b_e1': nrm((L, N_EXPERTS, 2 * D_EXPERT), 0.02),
        'w_e2': nrm((L, N_EXPERTS, D_EXPERT, d), DN_BETA * D_EXPERT ** -0.5),
        'b_e2': nrm((L, N_EXPERTS, d), 0.02 * DN_BETA),
    }


def reference(x_prompt, x_sample, cache_k, cache_v, state_pool, state_shift, state_wkv,
              w_in, w_out, ln_mix_g, ln_mix_b, ln_ffn_g, ln_ffn_b, pool_w, pool_scale,
              rw_mu, rw_w0, rw_w2, rw_a0, rw_a2, rw_g2, rw_kk, rw_ka, rw_rk, rw_gn_g, rw_gn_b,
              w_router, b_router, w_e1, b_e1, w_e2, b_e2):
    bp = x_prompt.shape[0]
    dt = x_prompt.dtype
    y_prompt, y_sample = x_prompt, x_sample
    st_p, st_s = [], []
    for l in range(DEPTH):
        lp = {'w_in': w_in[l], 'w_out': w_out[l], 'ln_mix_g': ln_mix_g[l], 'ln_mix_b': ln_mix_b[l],
              'ln_ffn_g': ln_ffn_g[l], 'ln_ffn_b': ln_ffn_b[l], 'pool_w': pool_w[l],
              'pool_scale': pool_scale[l], 'rw_mu': rw_mu[l], 'rw_w0': rw_w0[l], 'rw_w2': rw_w2[l],
              'rw_a0': rw_a0[l], 'rw_a2': rw_a2[l], 'rw_g2': rw_g2[l], 'rw_kk': rw_kk[l],
              'rw_ka': rw_ka[l], 'rw_rk': rw_rk[l], 'rw_gn_g': rw_gn_g[l], 'rw_gn_b': rw_gn_b[l],
              'w_router': w_router[l], 'b_router': b_router[l], 'w_e1': w_e1[l], 'b_e1': b_e1[l],
              'w_e2': w_e2[l], 'b_e2': b_e2[l]}
        empty_kv = jnp.zeros((bp, 0, SB_HEADS, HEAD_DIM), dt)
        y_prompt, sp = hybrid_layer(y_prompt, empty_kv, empty_kv,
                                    jnp.zeros((bp, POOL_HIST, POOL_WIDTH), dt),
                                    jnp.zeros((bp, 1, RW_COLS), dt),
                                    jnp.zeros((bp, RW_HEADS, HEAD_DIM, HEAD_DIM), dt), lp)
        y_sample, ss = hybrid_layer(y_sample, cache_k[l], cache_v[l], state_pool[l], state_shift[l],
                                    state_wkv[l], lp)
        st_p.append(sp)
        st_s.append(ss)
    k_p, v_p, pool_p, shift_p, wkv_p = (jnp.stack([s[i] for s in st_p]) for i in range(5))
    k_s, v_s, pool_s, shift_s, wkv_s = (jnp.stack([s[i] for s in st_s]) for i in range(5))
    return (y_prompt, y_sample, k_p, v_p, pool_p, shift_p, wkv_p, k_s, v_s, pool_s, shift_s, wkv_s)
```

```python
import functools

import jax
import jax.numpy as jnp
from jax import lax
from jax.experimental import pallas as pl
from jax.experimental.pallas import tpu as pltpu

F32 = jnp.float32
BF16 = jnp.bfloat16
I32 = jnp.int32

HEAD_DIM = 64
SB_HEADS = 12
SB_WIDTH = SB_HEADS * HEAD_DIM
POOL_WINDOWS = (2, 4, 8, 16)
POOL_GC = 128
POOL_WIDTH = POOL_GC * len(POOL_WINDOWS)
POOL_HIST = max(POOL_WINDOWS) - 1
RW_HEADS = 12
RW_WIDTH = RW_HEADS * HEAD_DIM
RW_LORA = 128
RW_GATE = 128
RW_COLS = 3 * RW_WIDTH + RW_LORA + RW_GATE
N_EXPERTS = 32
TOP_K = 4
SWIGLU_LIMIT = 7.0
SWIGLU_ALPHA = 1.702
DEPTH = 2
DN_ALPHA = (2.0 * DEPTH) ** 0.25
LN_EPS = 1e-5
GN_EPS = 64e-5

LANES = 128
HEAD_PAIRS = SB_WIDTH // LANES
VMEM_LIMIT = 56 * 1024 * 1024
HI = lax.Precision.HIGHEST

MOE_BM = 256


def _cparams(n_axes):
    return pltpu.CompilerParams(dimension_semantics=("arbitrary",) * n_axes,
                                vmem_limit_bytes=VMEM_LIMIT)


def _dot(a, b, precision=None):
    return jnp.dot(a, b, preferred_element_type=F32, precision=precision)


def _dot_nt(a, b, precision=None):
    return lax.dot_general(a, b, (((1,), (1,)), ((), ())), preferred_element_type=F32,
                           precision=precision)


def _dot_tn(a, b, precision=None):
    return lax.dot_general(a, b, (((0,), (0,)), ((), ())), preferred_element_type=F32,
                           precision=precision)


def _layer_norm(x, g, b):
    mu = jnp.mean(x, axis=-1, keepdims=True)
    xc = x - mu
    var = jnp.mean(xc * xc, axis=-1, keepdims=True)
    return xc * lax.rsqrt(var + LN_EPS) * g + b


def _inproj_kernel(x_ref, w_ref, o_ref, wbf_ref):
    @pl.when(pl.program_id(1) == 0)
    def _():
        wbf_ref[...] = w_ref[...].astype(BF16)

    o_ref[...] = _dot(x_ref[...].astype(BF16), wbf_ref[...])


def _inproj(x, w, layer, tm, tn):
    m, k = x.shape
    n = w.shape[2]
    return pl.pallas_call(
        _inproj_kernel,
        grid=(n // tn, m // tm),
        in_specs=[pl.BlockSpec((tm, k), lambda j, i: (i, 0)),
                  pl.BlockSpec((None, k, tn), lambda j, i: (layer, 0, j))],
        out_specs=pl.BlockSpec((tm, tn), lambda j, i: (i, j)),
        out_shape=jax.ShapeDtypeStruct((m, n), F32),
        scratch_shapes=[pltpu.VMEM((k, tn), BF16)],
        compiler_params=_cparams(2),
        name="inproj",
    )(x, w)


def _split3(a):
    hi = a.astype(BF16)
    r1 = a - hi.astype(F32)
    mid = r1.astype(BF16)
    lo = (r1 - mid.astype(F32)).astype(BF16)
    return hi, mid, lo


def _suffix_matrix(n):
    return (lax.broadcasted_iota(I32, (n, n), 0) > lax.broadcasted_iota(I32, (n, n), 1)).astype(BF16)


def _sb_block(qh, k_blk, v_blk, suffix, carry, mask):
    z = _dot_nt(qh, k_blk.astype(BF16))
    sp = jnp.maximum(z, 0.0) + jnp.log1p(jnp.exp(-jnp.abs(z)))
    log_stay = -sp if mask is None else jnp.where(mask, -sp, 0.0)
    hi, mid, lo = _split3(log_stay)
    after = (_dot(hi, suffix) + _dot(mid, suffix) + _dot(lo, suffix)) + carry
    att = jnp.exp(z - sp + after)
    if mask is not None:
        att = jnp.where(mask, att, 0.0)
    pv = _dot(att.astype(BF16), v_blk.astype(BF16))
    return pv, carry + jnp.sum(log_stay, axis=1, keepdims=True)


def _sb_kernel(*refs, tq, th, past):
    if past:
        q_ref, kn_ref, vn_ref, kh_ref, vh_ref, o_ref = refs
    else:
        q_ref, kn_ref, vn_ref, o_ref = refs
    qi = pl.program_id(2)
    q = q_ref[...] * (HEAD_DIM ** -0.5)
    lane = lax.broadcasted_iota(I32, (tq, LANES), 1)
    suffix_new = _suffix_matrix(tq)
    diag_mask = lax.broadcasted_iota(I32, (tq, tq), 1) < lax.broadcasted_iota(I32, (tq, tq), 0)
    out = jnp.zeros((tq, LANES), F32)
    for h in range(2):
        head_lanes = (lane < HEAD_DIM) if h == 0 else (lane >= HEAD_DIM)
        qh = jnp.where(head_lanes, q, 0.0).astype(BF16)
        rows = pl.ds(pl.multiple_of(qi * tq, tq), tq)
        acc, carry = _sb_block(qh, kn_ref[rows, :], vn_ref[rows, :], suffix_new,
                               jnp.zeros((tq, 1), F32), diag_mask)

        def new_body(step, c, qh=qh):
            rows = pl.ds(pl.multiple_of((qi - 1 - step) * tq, tq), tq)
            pv, carry = _sb_block(qh, kn_ref[rows, :], vn_ref[rows, :], suffix_new, c[1], None)
            return c[0] + pv, carry

        acc, carry = lax.fori_loop(0, qi, new_body, (acc, carry))
        if past:
            suffix_hist = _suffix_matrix(th)

            def hist_body(step, c, qh=qh, suffix_hist=suffix_hist):
                rows = pl.ds(pl.multiple_of((past // th - 1 - step) * th, th), th)
                pv, carry = _sb_block(qh, kh_ref[rows, :], vh_ref[rows, :], suffix_hist, c[1], None)
                return c[0] + pv, carry

            acc, carry = lax.fori_loop(0, past // th, hist_body, (acc, carry))
        out = jnp.where(head_lanes, acc, out)
    o_ref[...] = out


def _sb_attention(h_all, row0, batch, t, tq, hist, prev_out):
    m = h_all.shape[0]
    nq = t // tq
    rb0 = row0 // tq
    sb0 = row0 // t
    in_specs = [pl.BlockSpec((tq, LANES), lambda b, p, i: (rb0 + b * nq + i, p)),
                pl.BlockSpec((t, LANES), lambda b, p, i: (sb0 + b, HEAD_PAIRS + p)),
                pl.BlockSpec((t, LANES), lambda b, p, i: (sb0 + b, 2 * HEAD_PAIRS + p))]
    args = [h_all, h_all, h_all]
    past, th = 0, 0
    if hist is not None:
        past = hist[0].shape[1]
        th = min(past, 512)
        in_specs += [pl.BlockSpec((None, past, LANES), lambda b, p, i: (b, 0, p))] * 2
        args += list(hist)
    aliases = {}
    if prev_out is not None:
        in_specs.append(pl.BlockSpec(memory_space=pl.ANY))
        args.append(prev_out)
        aliases = {len(args) - 1: 0}
    kern = functools.partial(_sb_kernel, tq=tq, th=th, past=past)
    if prev_out is not None:
        kern = _drop_last_input(kern, n_in=len(args))
    return pl.pallas_call(
        kern,
        grid=(batch, HEAD_PAIRS, nq),
        in_specs=in_specs,
        out_specs=pl.BlockSpec((tq, LANES), lambda b, p, i: (rb0 + b * nq + i, p)),
        out_shape=jax.ShapeDtypeStruct((m, SB_WIDTH), F32),
        input_output_aliases=aliases,
        compiler_params=_cparams(3),
        name="sb_attention",
    )(*args)


def _drop_last_input(kern, n_in):
    def wrapped(*refs):
        return kern(*refs[:n_in - 1], *refs[n_in:])
    return wrapped


def _pool_kernel(u_ref, hist_ref, w_ref, scale_ref, o_ref, ext_ref, *, tt, pos0):
    ti = pl.program_id(1)
    halo = 16

    @pl.when(ti == 0)
    def _():
        ext_ref[0:halo, :] = hist_ref[...]

    @pl.when(ti > 0)
    def _():
        ext_ref[0:halo, :] = ext_ref[tt:tt + halo, :]

    u = u_ref[...]
    ext_ref[halo:halo + tt, :] = u
    pos = pos0 + ti * tt + lax.broadcasted_iota(I32, (tt, 1), 0)
    for g, w in enumerate(POOL_WINDOWS):
        cols = slice(g * POOL_GC, (g + 1) * POOL_GC)
        s = u[:, cols]
        for k in range(1, w):
            s = s + ext_ref[halo - k:halo - k + tt, cols]
        cnt = jnp.minimum(pos + 1, w).astype(F32)
        pooled = s / cnt - u[:, cols]
        y = _dot(pooled.astype(BF16), w_ref[g].astype(BF16))
        o_ref[:, cols] = y * scale_ref[:, cols]


def _pool_mixer(u_all, row0, batch, t, tt, hist16, w_pool, scale, pos0, prev_out):
    m = u_all.shape[0]
    nt = t // tt
    rb0 = row0 // tt
    in_specs = [pl.BlockSpec((tt, POOL_WIDTH), lambda b, i: (rb0 + b * nt + i, 0)),
                pl.BlockSpec((None, 16, POOL_WIDTH), lambda b, i: (b, 0, 0)),
                pl.BlockSpec((len(POOL_WINDOWS), POOL_GC, POOL_GC), lambda b, i: (0, 0, 0)),
                pl.BlockSpec((1, POOL_WIDTH), lambda b, i: (0, 0))]
    args = [u_all, hist16, w_pool, scale]
    aliases = {}
    kern = functools.partial(_pool_kernel, tt=tt, pos0=pos0)
    if prev_out is not None:
        in_specs.append(pl.BlockSpec(memory_space=pl.ANY))
        args.append(prev_out)
        aliases = {len(args) - 1: 0}
        kern = _drop_last_input(kern, n_in=len(args))
    return pl.pallas_call(
        kern,
        grid=(batch, nt),
        in_specs=in_specs,
        out_specs=pl.BlockSpec((tt, POOL_WIDTH), lambda b, i: (rb0 + b * nt + i, 0)),
        out_shape=jax.ShapeDtypeStruct((m, POOL_WIDTH), F32),
        scratch_shapes=[pltpu.VMEM((tt + 16, POOL_WIDTH), F32)],
        input_output_aliases=aliases,
        compiler_params=_cparams(2),
        name="pool_mixer",
    )(*args)


def _rw_pre_kernel(c_ref, hist_ref, mu_ref, w0_ref, w2_ref, a0_ref, a2_ref, g2_ref,
                   rkv_ref, lw_ref, a_ref, g_ref, last_ref, *, tt):
    ti = pl.program_id(1)

    @pl.when(ti == 0)
    def _():
        last_ref[...] = hist_ref[...]

    c = c_ref[...]
    row = lax.broadcasted_iota(I32, (tt, 1), 0)
    prev = jnp.where(row == 0, last_ref[...], pltpu.roll(c, 1, 0))
    last_ref[...] = c[tt - 1:tt, :]
    xs = c + (prev - c) * mu_ref[...]
    rkv_ref[...] = xs[:, :3 * RW_WIDTH]
    lora = xs[:, 3 * RW_WIDTH:3 * RW_WIDTH + RW_LORA]
    wd = _dot(jnp.tanh(lora).astype(BF16), w2_ref[...].astype(BF16))
    sp = jax.nn.softplus(-(w0_ref[...] + wd))
    lw_ref[...] = -jnp.exp(-sp - 0.5)
    a_ref[...] = jax.nn.sigmoid(a0_ref[...] + _dot(lora.astype(BF16), a2_ref[...].astype(BF16)))
    gd = xs[:, 3 * RW_WIDTH + RW_LORA:]
    g_ref[...] = _dot(jax.nn.sigmoid(gd).astype(BF16), g2_ref[...].astype(BF16))


def _rw_pre(c_all, row0, batch, t, tt, shift_hist, mu, w0, w2p, a0, a2p, g2, prev_outs):
    m = c_all.shape[0]
    nt = t // tt
    rb0 = row0 // tt
    row_map = lambda b, i: (rb0 + b * nt + i, 0)
    const = lambda b, i: (0, 0)
    in_specs = [pl.BlockSpec((tt, RW_COLS), row_map),
                pl.BlockSpec((None, 1, RW_COLS), lambda b, i: (b, 0, 0)),
                pl.BlockSpec((1, RW_COLS), const),
                pl.BlockSpec((1, RW_WIDTH), const),
                pl.BlockSpec((RW_LORA, RW_WIDTH), const),
                pl.BlockSpec((1, RW_WIDTH), const),
                pl.BlockSpec((RW_LORA, RW_WIDTH), const),
                pl.BlockSpec((RW_GATE, RW_WIDTH), const)]
    args = [c_all, shift_hist, mu, w0, w2p, a0, a2p, g2]
    widths = (3 * RW_WIDTH, RW_WIDTH, RW_WIDTH, RW_WIDTH)
    aliases = {}
    kern = functools.partial(_rw_pre_kernel, tt=tt)
    if prev_outs is not None:
        n_real = len(args)
        for j, po in enumerate(prev_outs):
            in_specs.append(pl.BlockSpec(memory_space=pl.ANY))
            args.append(po)
            aliases[n_real + j] = j
        kern = _drop_inputs(kern, n_real, len(prev_outs))
    return pl.pallas_call(
        kern,
        grid=(batch, nt),
        in_specs=in_specs,
        out_specs=[pl.BlockSpec((tt, wd), row_map) for wd in widths],
        out_shape=[jax.ShapeDtypeStruct((m, wd), F32) for wd in widths],
        scratch_shapes=[pltpu.VMEM((1, RW_COLS), F32)],
        input_output_aliases=aliases,
        compiler_params=_cparams(2),
        name="rwkv_pre",
    )(*args)


def _drop_inputs(kern, n_real, n_drop):
    def wrapped(*refs):
        return kern(*refs[:n_real], *refs[n_real + n_drop:])
    return wrapped


def _rw_chunk(r, k0, v, lw, a, g, s_prev, prm, masks, c):
    kks, ka, rk, gng, gnb = prm
    l_incl, strict, incl, eye = masks
    gcum = _dot(l_incl, lw, HI)
    e_in = jnp.exp(gcum)
    e_ex = jnp.exp(gcum - lw)
    e_neg = jnp.exp(-gcum)
    kk = k0 * kks
    kk = kk * lax.rsqrt(jnp.maximum(jnp.sum(kk * kk, axis=-1, keepdims=True), 1e-24))
    kmod = k0 * (1.0 + (a - 1.0) * ka)
    al = -kk * e_ex
    be = kk * a * e_neg
    kc = kmod * e_neg
    rc = r * e_in
    ar = jnp.concatenate([al, rc], axis=0)
    bk = jnp.concatenate([be, kc], axis=0)
    m4 = _dot_nt(ar, bk, HI)
    a_ab = m4[:c, :c] * strict
    a_ak = m4[:c, c:] * strict
    p_rb = m4[c:, :c] * incl
    p_rk = m4[c:, c:] * incl
    akv = _dot(jnp.concatenate([a_ak, p_rk], axis=0), v, HI)
    p = a_ab
    tinv = eye + p
    span = 2
    while span < c:
        p = _dot(p, p, HI)
        tinv = tinv + _dot(p, tinv, HI)
        span *= 2
    w = _dot(tinv, al, HI)
    u_free = _dot(tinv, akv[:c], HI)
    ws = _dot_nt(jnp.concatenate([w, rc], axis=0), s_prev, HI)
    u = ws[:c] + u_free
    o = ws[c:] + _dot(p_rb, u, HI) + akv[c:]
    s_new = (s_prev + _dot_tn(jnp.concatenate([u, v], axis=0), bk, HI)) * e_in[c - 1:c, :]
    mu_o = jnp.mean(o, axis=-1, keepdims=True)
    oc = o - mu_o
    var_o = jnp.mean(oc * oc, axis=-1, keepdims=True)
    on = oc * lax.rsqrt(var_o + GN_EPS) * gng + gnb
    bonus = jnp.sum(r * kmod * rk, axis=-1, keepdims=True) * v
    return (on + bonus) * g, s_new


def _rw_scan_kernel(r_ref, k_ref, v_ref, lw_ref, a_ref, g_ref, kks_ref, ka_ref, rk_ref, gng_ref,
                    gnb_ref, s0_ref, o_ref, sout_ref, s_ref, *, tb, c):
    ti = pl.program_id(2)

    @pl.when(ti == 0)
    def _():
        s_ref[...] = s0_ref[...]

    ri = lax.broadcasted_iota(I32, (c, c), 0)
    ci = lax.broadcasted_iota(I32, (c, c), 1)
    masks = ((ri >= ci).astype(F32), (ri > ci).astype(F32), (ri >= ci).astype(F32),
             (ri == ci).astype(F32))
    prms = []
    for h in range(2):
        sl = slice(h * HEAD_DIM, (h + 1) * HEAD_DIM)
        prms.append(tuple(p[:, sl] for p in (kks_ref[...], ka_ref[...], rk_ref[...], gng_ref[...],
                                             gnb_ref[...])))

    def body(ci_, states):
        rows = pl.ds(pl.multiple_of(ci_ * c, c), c)
        blocks = [ref[rows, :] for ref in (r_ref, k_ref, v_ref, lw_ref, a_ref, g_ref)]
        outs, new_states = [], []
        for h in range(2):
            sl = slice(h * HEAD_DIM, (h + 1) * HEAD_DIM)
            r, k0, v, lw, a, g = (x[:, sl] for x in blocks)
            out, s_new = _rw_chunk(r, k0, v, lw, a, g, states[h], prms[h], masks, c)
            outs.append(out)
            new_states.append(s_new)
        o_ref[rows, :] = jnp.concatenate(outs, axis=-1)
        return tuple(new_states)

    states = lax.fori_loop(0, tb // c, body, (s_ref[0], s_ref[1]))
    s_ref[0] = states[0]
    s_ref[1] = states[1]
    sout_ref[...] = s_ref[...]


def _rw_scan(rkv, lw, a, g, row0, batch, t, tb, c, params, wkv0, prev_out):
    m = rkv.shape[0]
    nt = t // tb
    rb0 = row0 // tb

    def col(off):
        return lambda b, p, i: (rb0 + b * nt + i, off + p)

    pspec = pl.BlockSpec((1, LANES), lambda b, p, i: (0, p))
    sspec = pl.BlockSpec((None, 2, HEAD_DIM, HEAD_DIM), lambda b, p, i: (b, p, 0, 0))
    in_specs = [pl.BlockSpec((tb, LANES), col(0)), pl.BlockSpec((tb, LANES), col(HEAD_PAIRS)),
                pl.BlockSpec((tb, LANES), col(2 * HEAD_PAIRS)),
                pl.BlockSpec((tb, LANES), col(0)), pl.BlockSpec((tb, LANES), col(0)),
                pl.BlockSpec((tb, LANES), col(0))] + [pspec] * 5 + [sspec]
    args = [rkv, rkv, rkv, lw, a, g] + list(params) + [wkv0]
    aliases = {}
    kern = functools.partial(_rw_scan_kernel, tb=tb, c=c)
    if prev_out is not None:
        n_real = len(args)
        in_specs.append(pl.BlockSpec(memory_space=pl.ANY))
        args.append(prev_out)
        aliases = {n_real: 0}
        kern = _drop_inputs(kern, n_real, 1)
    return pl.pallas_call(
        kern,
        grid=(batch, HEAD_PAIRS, nt),
        in_specs=in_specs,
        out_specs=[pl.BlockSpec((tb, LANES), col(0)), sspec],
        out_shape=[jax.ShapeDtypeStruct((m, RW_WIDTH), F32),
                   jax.ShapeDtypeStruct((batch, RW_HEADS, HEAD_DIM, HEAD_DIM), F32)],
        scratch_shapes=[pltpu.VMEM((2, HEAD_DIM, HEAD_DIM), F32)],
        input_output_aliases=aliases,
        compiler_params=_cparams(3),
        name="rwkv_scan",
    )(*args)


def _outproj_kernel(oa_ref, ob_ref, oc_ref, x_ref, w_ref, g_ref, b_ref, y_ref):
    mix = _dot(oa_ref[...].astype(BF16), w_ref[0:SB_WIDTH, :])
    mix += _dot(ob_ref[...].astype(BF16), w_ref[SB_WIDTH:SB_WIDTH + POOL_WIDTH, :])
    mix += _dot(oc_ref[...].astype(BF16), w_ref[SB_WIDTH + POOL_WIDTH:, :])
    y_ref[...] = _layer_norm(DN_ALPHA * x_ref[...] + mix, g_ref[...], b_ref[...])


def _outproj_ln(oa, ob, oc, x, w_bf, g, b, tm):
    m, d = x.shape
    row = lambda i: (i, 0)
    const = lambda i: (0, 0)
    return pl.pallas_call(
        _outproj_kernel,
        grid=(m // tm,),
        in_specs=[pl.BlockSpec((tm, SB_WIDTH), row), pl.BlockSpec((tm, POOL_WIDTH), row),
                  pl.BlockSpec((tm, RW_WIDTH), row), pl.BlockSpec((tm, d), row),
                  pl.BlockSpec(w_bf.shape, const), pl.BlockSpec((1, d), const),
                  pl.BlockSpec((1, d), const)],
        out_specs=pl.BlockSpec((tm, d), row),
        out_shape=jax.ShapeDtypeStruct((m, d), F32),
        compiler_params=_cparams(1),
        name="outproj_ln",
    )(oa, ob, oc, x, w_bf, g, b)


def _router_kernel(x_ref, w_ref, b_ref, idx_ref, gate_ref, *, tm):
    logits = _dot(x_ref[...], w_ref[...], HI) + b_ref[...]
    lane = lax.broadcasted_iota(I32, (tm, N_EXPERTS), 1).astype(F32)
    work = logits
    vals, idxs = [], []
    for _ in range(TOP_K):
        top = jnp.max(work, axis=1, keepdims=True)
        idx = jnp.min(jnp.where(work == top, lane, float(N_EXPERTS)), axis=1, keepdims=True)
        vals.append(top)
        idxs.append(idx)
        work = jnp.where(lane == idx, -jnp.inf, work)
    exps = [jnp.exp(v - vals[0]) for v in vals]
    total = exps[0] + exps[1] + exps[2] + exps[3]
    out_lane = lax.broadcasted_iota(I32, (tm, LANES), 1)
    idx_out = jnp.zeros((tm, LANES), F32)
    gate_out = jnp.zeros((tm, LANES), F32)
    for k in range(TOP_K):
        idx_out = jnp.where(out_lane == k, idxs[k], idx_out)
        gate_out = jnp.where(out_lane == k, exps[k] / total, gate_out)
    idx_ref[...] = idx_out.astype(I32)
    gate_ref[...] = gate_out


def _router(x, w, b, tm):
    m, d = x.shape
    return pl.pallas_call(
        functools.partial(_router_kernel, tm=tm),
        grid=(m // tm,),
        in_specs=[pl.BlockSpec((tm, d), lambda i: (i, 0)),
                  pl.BlockSpec((d, N_EXPERTS), lambda i: (0, 0)),
                  pl.BlockSpec((1, N_EXPERTS), lambda i: (0, 0))],
        out_specs=[pl.BlockSpec((tm, LANES), lambda i: (i, 0))] * 2,
        out_shape=[jax.ShapeDtypeStruct((m, LANES), I32), jax.ShapeDtypeStruct((m, LANES), F32)],
        compiler_params=_cparams(1),
        name="router",
    )(x, w, b)


def _row_copy(src_hbm, dst_ref, sem, src_row, dst_row):
    return pltpu.make_async_copy(src_hbm.at[pl.ds(src_row, 1), :], dst_ref.at[pl.ds(dst_row, 1), :], sem)


def _gather_kernel(tok_ref, x_hbm, o_ref, sem, *, bm):
    def start(r, carry):
        _row_copy(x_hbm, o_ref, sem, tok_ref[0, 0, r], r).start()
        return carry

    lax.fori_loop(0, bm, start, 0)

    def wait(r, carry):
        _row_copy(x_hbm, o_ref, sem, 0, r).wait()
        return carry

    lax.fori_loop(0, bm, wait, 0)


def _gather_rows(x, buf_tok, bm):
    d = x.shape[1]
    nb = buf_tok.shape[0] // bm
    return pl.pallas_call(
        functools.partial(_gather_kernel, bm=bm),
        grid=(nb,),
        in_specs=[pl.BlockSpec((1, 1, bm), lambda i: (i, 0, 0), memory_space=pltpu.SMEM),
                  pl.BlockSpec(memory_space=pl.ANY)],
        out_specs=pl.BlockSpec((bm, d), lambda i: (i, 0)),
        out_shape=jax.ShapeDtypeStruct((nb * bm, d), F32),
        scratch_shapes=[pltpu.SemaphoreType.DMA(())],
        compiler_params=_cparams(1),
        name="moe_gather",
    )(buf_tok.reshape(nb, 1, bm), x)


def _expert_up_kernel(be_ref, nu_ref, x_ref, wg_ref, wl_ref, bg_ref, bl_ref, act_ref):
    @pl.when(pl.program_id(1) < nu_ref[0])
    def _():
        x = x_ref[...].astype(BF16)
        h_glu = _dot(x, wg_ref[...].astype(BF16)) + bg_ref[...]
        h_lin = _dot(x, wl_ref[...].astype(BF16)) + bl_ref[...]
        h_glu = jnp.minimum(h_glu, SWIGLU_LIMIT)
        h_lin = jnp.clip(h_lin, -SWIGLU_LIMIT, SWIGLU_LIMIT)
        act = h_glu * jax.nn.sigmoid(SWIGLU_ALPHA * h_glu) * (h_lin + 1.0)
        act_ref[...] = act.astype(BF16)


def _expert_up(xb, w1, b1, layer, block_e, n_used, bm, tn):
    rows, d = xb.shape
    de = w1.shape[3] // 2
    nj = de // tn
    nb = rows // bm

    def blk(i, nu):
        return jnp.minimum(i, nu[0] - 1)

    grid_spec = pltpu.PrefetchScalarGridSpec(
        num_scalar_prefetch=2,
        grid=(nj, nb),
        in_specs=[pl.BlockSpec((bm, d), lambda j, i, be, nu: (blk(i, nu), 0)),
                  pl.BlockSpec((None, None, d, tn), lambda j, i, be, nu: (layer, be[blk(i, nu)], 0, j)),
                  pl.BlockSpec((None, None, d, tn),
                               lambda j, i, be, nu: (layer, be[blk(i, nu)], 0, nj + j)),
                  pl.BlockSpec((None, None, 1, tn), lambda j, i, be, nu: (layer, be[blk(i, nu)], 0, j)),
                  pl.BlockSpec((None, None, 1, tn),
                               lambda j, i, be, nu: (layer, be[blk(i, nu)], 0, nj + j))],
        out_specs=pl.BlockSpec((bm, tn), lambda j, i, be, nu: (blk(i, nu), j)),
    )
    return pl.pallas_call(
        _expert_up_kernel,
        grid_spec=grid_spec,
        out_shape=jax.ShapeDtypeStruct((rows, de), BF16),
        compiler_params=_cparams(2),
        name="moe_up",
    )(block_e, n_used, xb, w1, w1, b1, b1)


def _expert_down_kernel(be_ref, nu_ref, a_ref, w_ref, b_ref, y_ref):
    @pl.when(pl.program_id(1) < nu_ref[0])
    def _():
        y_ref[...] = _dot(a_ref[...], w_ref[...].astype(BF16)) + b_ref[...]


def _expert_down(act, w2, b2, layer, block_e, n_used, bm, tn):
    rows, de = act.shape
    d = w2.shape[3]
    nj = d // tn
    nb = rows // bm

    def blk(i, nu):
        return jnp.minimum(i, nu[0] - 1)

    grid_spec = pltpu.PrefetchScalarGridSpec(
        num_scalar_prefetch=2,
        grid=(nj, nb),
        in_specs=[pl.BlockSpec((bm, de), lambda j, i, be, nu: (blk(i, nu), 0)),
                  pl.BlockSpec((None, None, de, tn), lambda j, i, be, nu: (layer, be[blk(i, nu)], 0, j)),
                  pl.BlockSpec((None, None, 1, tn), lambda j, i, be, nu: (layer, be[blk(i, nu)], 0, j))],
        out_specs=pl.BlockSpec((bm, tn), lambda j, i, be, nu: (blk(i, nu), j)),
    )
    return pl.pallas_call(
        _expert_down_kernel,
        grid_spec=grid_spec,
        out_shape=jax.ShapeDtypeStruct((rows, d), F32),
        compiler_params=_cparams(2),
        name="moe_down",
    )(block_e, n_used, act, w2, b2)


def _combine_kernel(dest_ref, y_hbm, gate_ref, x_ref, g_ref, b_ref, o_ref, rows_ref, sem, *, tm):
    def start(r, carry):
        for k in range(TOP_K):
            _row_copy(y_hbm, rows_ref.at[k], sem, dest_ref[0, 0, r * TOP_K + k], r).start()
        return carry

    lax.fori_loop(0, tm, start, 0)

    def wait(r, carry):
        for k in range(TOP_K):
            _row_copy(y_hbm, rows_ref.at[k], sem, 0, r).wait()
        return carry

    lax.fori_loop(0, tm, wait, 0)
    gates = gate_ref[...]
    ffn = gates[:, 0:1] * rows_ref[0]
    for k in range(1, TOP_K):
        ffn += gates[:, k:k + 1] * rows_ref[k]
    o_ref[...] = _layer_norm(DN_ALPHA * x_ref[...] + ffn, g_ref[...], b_ref[...])


def _combine_ln(y, dest, gates, x, g, b, tm):
    m, d = x.shape
    nb = m // tm
    return pl.pallas_call(
        functools.partial(_combine_kernel, tm=tm),
        grid=(nb,),
        in_specs=[pl.BlockSpec((1, 1, tm * TOP_K), lambda i: (i, 0, 0), memory_space=pltpu.SMEM),
                  pl.BlockSpec(memory_space=pl.ANY),
                  pl.BlockSpec((tm, LANES), lambda i: (i, 0)),
                  pl.BlockSpec((tm, d), lambda i: (i, 0)),
                  pl.BlockSpec((1, d), lambda i: (0, 0)),
                  pl.BlockSpec((1, d), lambda i: (0, 0))],
        out_specs=pl.BlockSpec((tm, d), lambda i: (i, 0)),
        out_shape=jax.ShapeDtypeStruct((m, d), F32),
        scratch_shapes=[pltpu.VMEM((TOP_K, tm, d), F32), pltpu.SemaphoreType.DMA(())],
        compiler_params=_cparams(1),
        name="moe_combine_ln",
    )(dest.reshape(nb, 1, tm * TOP_K), y, gates, x, g, b)


def _routing_tables(top_i, bm):
    n = top_i.shape[0]
    nk = n * TOP_K
    flat_e = top_i.reshape(nk)
    onehot = (flat_e[:, None] == jnp.arange(N_EXPERTS, dtype=I32)[None, :]).astype(I32)
    rank = jnp.sum((jnp.cumsum(onehot, axis=0) - onehot) * onehot, axis=1)
    counts = jnp.sum(onehot, axis=0)
    padded = (counts + bm - 1) // bm * bm
    pad_end = jnp.cumsum(padded)
    pad_start = pad_end - padded
    dest = (pad_start[flat_e] + rank).astype(I32)
    nb = (nk + N_EXPERTS * (bm - 1)) // bm
    buf_tok = jnp.zeros((nb * bm,), I32).at[dest].set(jnp.arange(nk, dtype=I32) // TOP_K)
    block_e = jnp.minimum(jnp.searchsorted(pad_end, jnp.arange(nb, dtype=I32) * bm, side='right'),
                          N_EXPERTS - 1).astype(I32)
    n_used = (pad_end[-1:] // bm).astype(I32)
    return dest, buf_tok, block_e, n_used


def _moe_ln(x, lp, big, layer):
    idx, gates = _router(x, lp['w_router'], lp['b_router'].reshape(1, N_EXPERTS), tm=256)
    dest, buf_tok, block_e, n_used = _routing_tables(idx[:, :TOP_K], MOE_BM)
    xb = _gather_rows(x, buf_tok, MOE_BM)
    nl, e, d, de2 = big['w_e1'].shape
    act = _expert_up(xb, big['w_e1'], big['b_e1'].reshape(nl, e, 1, de2), layer, block_e, n_used,
                     MOE_BM, tn=512)
    y = _expert_down(act, big['w_e2'], big['b_e2'].reshape(nl, e, 1, d), layer, block_e, n_used,
                     MOE_BM, tn=512)
    return _combine_ln(y, dest, gates, x, lp['ln_ffn_g'].reshape(1, d), lp['ln_ffn_b'].reshape(1, d),
                       tm=128)


def _layer(x, lp, big, layer, st, n_p, bp, tp, bs, ts):
    d = x.shape[1]
    h = _inproj(x, big['w_in'], layer, tm=768, tn=768)
    qkv_w = 3 * SB_WIDTH
    u = h[:, qkv_w:qkv_w + POOL_WIDTH]
    c = h[:, qkv_w + POOL_WIDTH:]

    o_a = _sb_attention(h, 0, bp, tp, 128, None, None)
    hist = (st['cache_k'].reshape(bs, -1, SB_WIDTH), st['cache_v'].reshape(bs, -1, SB_WIDTH))
    o_a = _sb_attention(h, n_p, bs, ts, ts, hist, o_a)
    past = hist[0].shape[1]

    w_pool = lp['pool_w']
    scale = lp['pool_scale'].reshape(1, POOL_WIDTH)
    o_b = _pool_mixer(u, 0, bp, tp, 512, jnp.zeros((bp, 16, POOL_WIDTH), F32), w_pool, scale, 0, None)
    hist16 = jnp.pad(st['state_pool'], ((0, 0), (1, 0), (0, 0)))
    o_b = _pool_mixer(u, n_p, bs, ts, ts, hist16, w_pool, scale, past, o_b)

    zero_lora = jnp.zeros((RW_LORA // 2, RW_WIDTH), F32)
    w2p = jnp.concatenate([lp['rw_w2'], zero_lora], axis=0)
    a2p = jnp.concatenate([zero_lora, lp['rw_a2']], axis=0)
    pre_args = (lp['rw_mu'].reshape(1, RW_COLS), lp['rw_w0'].reshape(1, RW_WIDTH), w2p,
                lp['rw_a0'].reshape(1, RW_WIDTH), a2p, lp['rw_g2'])
    pre = _rw_pre(c, 0, bp, tp, 256, jnp.zeros((bp, 1, RW_COLS), F32), *pre_args, None)
    rkv, lw, a, g = _rw_pre(c, n_p, bs, ts, ts, st['state_shift'], *pre_args, pre)
    prm = tuple(lp[k].reshape(1, RW_WIDTH) for k in ('rw_kk', 'rw_ka', 'rw_rk', 'rw_gn_g', 'rw_gn_b'))
    o_c, wkv_p = _rw_scan(rkv, lw, a, g, 0, bp, tp, 512, 64, prm,
                          jnp.zeros((bp, RW_HEADS, HEAD_DIM, HEAD_DIM), F32), None)
    o_c, wkv_s = _rw_scan(rkv, lw, a, g, n_p, bs, ts, ts, ts, prm, st['state_wkv'], o_c)

    x1 = _outproj_ln(o_a, o_b, o_c, x, lp['w_out'].astype(BF16), lp['ln_mix_g'].reshape(1, d),
                     lp['ln_mix_b'].reshape(1, d), tm=256)
    x2 = _moe_ln(x1, lp, big, layer)

    def states(rows, b, t, wkv):
        hb = rows.reshape(b, t, -1)
        k_new = hb[:, :, SB_WIDTH:2 * SB_WIDTH].reshape(b, t, SB_HEADS, HEAD_DIM)
        v_new = hb[:, :, 2 * SB_WIDTH:qkv_w].reshape(b, t, SB_HEADS, HEAD_DIM)
        pool_new = hb[:, t - POOL_HIST:, qkv_w:qkv_w + POOL_WIDTH]
        shift_new = hb[:, t - 1:, qkv_w + POOL_WIDTH:]
        return k_new, v_new, pool_new, shift_new, wkv

    return x2, states(h[:n_p], bp, tp, wkv_p), states(h[n_p:], bs, ts, wkv_s)


def kernel(x_prompt, x_sample, cache_k, cache_v, state_pool, state_shift, state_wkv, w_in, w_out, ln_mix_g, ln_mix_b, ln_ffn_g, ln_ffn_b, pool_w, pool_scale, rw_mu, rw_w0, rw_w2, rw_a0, rw_a2, rw_g2, rw_kk, rw_ka, rw_rk, rw_gn_g, rw_gn_b, w_router, b_router, w_e1, b_e1, w_e2, b_e2):
    bp, tp, d = x_prompt.shape
    bs, ts, _ = x_sample.shape
    n_p = bp * tp
    x = jnp.concatenate([x_prompt.reshape(n_p, d), x_sample.reshape(bs * ts, d)], axis=0)
    big = dict(w_in=w_in, w_e1=w_e1, b_e1=b_e1, w_e2=w_e2, b_e2=b_e2)
    weights = dict(w_out=w_out, ln_mix_g=ln_mix_g, ln_mix_b=ln_mix_b, ln_ffn_g=ln_ffn_g,
                   ln_ffn_b=ln_ffn_b, pool_w=pool_w, pool_scale=pool_scale, rw_mu=rw_mu, rw_w0=rw_w0,
                   rw_w2=rw_w2, rw_a0=rw_a0, rw_a2=rw_a2, rw_g2=rw_g2, rw_kk=rw_kk, rw_ka=rw_ka,
                   rw_rk=rw_rk, rw_gn_g=rw_gn_g, rw_gn_b=rw_gn_b, w_router=w_router,
                   b_router=b_router)
    st_p, st_s = [], []
    for l in range(w_in.shape[0]):
        lp = {k: v[l] for k, v in weights.items()}
        st = dict(cache_k=cache_k[l], cache_v=cache_v[l], state_pool=state_pool[l],
                  state_shift=state_shift[l], state_wkv=state_wkv[l])
        x, sp, ss = _layer(x, lp, big, l, st, n_p, bp, tp, bs, ts)
        st_p.append(sp)
        st_s.append(ss)
    y_prompt = x[:n_p].reshape(bp, tp, d)
    y_sample = x[n_p:].reshape(bs, ts, d)
    k_p, v_p, pool_p, shift_p, wkv_p = (jnp.stack([s[i] for s in st_p]) for i in range(5))
    k_s, v_s, pool_s, shift_s, wkv_s = (jnp.stack([s[i] for s in st_s]) for i in range(5))
    return (y_prompt, y_sample, k_p, v_p, pool_p, shift_p, wkv_p, k_s, v_s, pool_s, shift_s, wkv_s)
```

```python
import functools

import jax
import jax.numpy as jnp
from jax import lax
from jax.experimental import pallas as pl
from jax.experimental.pallas import tpu as pltpu

F32 = jnp.float32
BF16 = jnp.bfloat16
I32 = jnp.int32

HEAD_DIM = 64
SB_HEADS = 12
SB_WIDTH = SB_HEADS * HEAD_DIM
POOL_WINDOWS = (2, 4, 8, 16)
POOL_GC = 128
POOL_WIDTH = POOL_GC * len(POOL_WINDOWS)
POOL_HIST = max(POOL_WINDOWS) - 1
RW_HEADS = 12
RW_WIDTH = RW_HEADS * HEAD_DIM
RW_LORA = 128
RW_GATE = 128
RW_COLS = 3 * RW_WIDTH + RW_LORA + RW_GATE
N_EXPERTS = 32
TOP_K = 4
SWIGLU_LIMIT = 7.0
SWIGLU_ALPHA = 1.702
DEPTH = 2
DN_ALPHA = (2.0 * DEPTH) ** 0.25
LN_EPS = 1e-5
GN_EPS = 64e-5

LANES = 128
HEAD_PAIRS = SB_WIDTH // LANES
VMEM_LIMIT = 56 * 1024 * 1024
HI = lax.Precision.HIGHEST
ATT_LOG_FLOOR = -110.0

MOE_BM = 256


def _cparams(n_axes):
    return pltpu.CompilerParams(dimension_semantics=("arbitrary",) * n_axes,
                                vmem_limit_bytes=VMEM_LIMIT)


def _dot(a, b, precision=None):
    return jnp.dot(a, b, preferred_element_type=F32, precision=precision)


def _dot_nt(a, b, precision=None):
    return lax.dot_general(a, b, (((1,), (1,)), ((), ())), preferred_element_type=F32,
                           precision=precision)


def _dot_tn(a, b, precision=None):
    return lax.dot_general(a, b, (((0,), (0,)), ((), ())), preferred_element_type=F32,
                           precision=precision)


def _layer_norm(x, g, b):
    mu = jnp.mean(x, axis=-1, keepdims=True)
    xc = x - mu
    var = jnp.mean(xc * xc, axis=-1, keepdims=True)
    return xc * lax.rsqrt(var + LN_EPS) * g + b


def _inproj_kernel(x_ref, w_ref, o_ref, wbf_ref):
    @pl.when(pl.program_id(1) == 0)
    def _():
        wbf_ref[...] = w_ref[...].astype(BF16)

    o_ref[...] = _dot(x_ref[...].astype(BF16), wbf_ref[...])


def _inproj(x, w, layer, tm, tn):
    m, k = x.shape
    n = w.shape[2]
    return pl.pallas_call(
        _inproj_kernel,
        grid=(n // tn, m // tm),
        in_specs=[pl.BlockSpec((tm, k), lambda j, i: (i, 0)),
                  pl.BlockSpec((None, k, tn), lambda j, i: (layer, 0, j))],
        out_specs=pl.BlockSpec((tm, tn), lambda j, i: (i, j)),
        out_shape=jax.ShapeDtypeStruct((m, n), F32),
        scratch_shapes=[pltpu.VMEM((k, tn), BF16)],
        compiler_params=_cparams(2),
        name="inproj",
    )(x, w)


def _split2(a):
    hi = a.astype(BF16)
    lo = (a - hi.astype(F32)).astype(BF16)
    return hi, lo


def _suffix_matrix(n):
    return (lax.broadcasted_iota(I32, (n, n), 0) > lax.broadcasted_iota(I32, (n, n), 1)).astype(BF16)


def _sb_block(qh, k_bf, v_bf, suffix, carry, mask):
    z = _dot_nt(qh, k_bf)
    sp = jnp.maximum(z, 0.0) + jnp.log1p(jnp.exp(-jnp.abs(z)))
    log_stay = -sp if mask is None else jnp.where(mask, -sp, 0.0)
    hi, lo = _split2(log_stay)
    after = _dot(hi, suffix) + _dot(lo, suffix) + carry
    att = jnp.exp(z - sp + after)
    if mask is not None:
        att = jnp.where(mask, att, 0.0)
    pv = _dot(att.astype(BF16), v_bf)
    return pv, after[:, 0:1] + log_stay[:, 0:1]


def _sb_live(carries):
    return (jnp.max(jnp.maximum(carries[0], carries[1])) > ATT_LOG_FLOOR).astype(I32)


def _sb_sweep(qhs, k_ref, v_ref, tk, n_blocks, suffix, live, accs, carries):
    def cond(s):
        return jnp.logical_and(s[0] < n_blocks, s[1] > 0)

    def body(s):
        step, _, accs, carries = s
        rows = pl.ds(pl.multiple_of((n_blocks - 1 - step) * tk, tk), tk)
        k_bf = k_ref[rows, :].astype(BF16)
        v_bf = v_ref[rows, :].astype(BF16)
        new = [_sb_block(qhs[h], k_bf, v_bf, suffix, carries[h], None) for h in range(2)]
        accs = tuple(accs[h] + new[h][0] for h in range(2))
        carries = tuple(new[h][1] for h in range(2))
        return step + 1, _sb_live(carries), accs, carries

    _, live, accs, carries = lax.while_loop(cond, body, (jnp.int32(0), live, accs, carries))
    return live, accs, carries


def _sb_kernel(*refs, tq, th, past):
    if past:
        q_ref, kn_ref, vn_ref, kh_ref, vh_ref, o_ref, kbf_ref, vbf_ref = refs
    else:
        q_ref, kn_ref, vn_ref, o_ref, kbf_ref, vbf_ref = refs
    qi = pl.program_id(2)

    @pl.when(qi == 0)
    def _():
        kbf_ref[...] = kn_ref[...].astype(BF16)
        vbf_ref[...] = vn_ref[...].astype(BF16)

    q = q_ref[...] * (HEAD_DIM ** -0.5)
    lane = lax.broadcasted_iota(I32, (tq, LANES), 1)
    qhs = [jnp.where(lane < HEAD_DIM, q, 0.0).astype(BF16),
           jnp.where(lane >= HEAD_DIM, q, 0.0).astype(BF16)]
    suffix_new = _suffix_matrix(tq)
    diag_mask = lax.broadcasted_iota(I32, (tq, tq), 1) < lax.broadcasted_iota(I32, (tq, tq), 0)
    rows = pl.ds(pl.multiple_of(qi * tq, tq), tq)
    k_bf = kbf_ref[rows, :]
    v_bf = vbf_ref[rows, :]
    first = [_sb_block(qhs[h], k_bf, v_bf, suffix_new, jnp.zeros((tq, 1), F32), diag_mask)
             for h in range(2)]
    accs = tuple(f[0] for f in first)
    carries = tuple(f[1] for f in first)
    live, accs, carries = _sb_sweep(qhs, kbf_ref, vbf_ref, tq, qi, suffix_new, _sb_live(carries),
                                    accs, carries)
    if past:
        _, accs, carries = _sb_sweep(qhs, kh_ref, vh_ref, th, past // th, _suffix_matrix(th), live,
                                     accs, carries)
    o_ref[...] = jnp.where(lane < HEAD_DIM, accs[0], accs[1])


def _sb_attention(h_all, row0, batch, t, tq, hist, prev_out):
    m = h_all.shape[0]
    nq = t // tq
    rb0 = row0 // tq
    sb0 = row0 // t
    in_specs = [pl.BlockSpec((tq, LANES), lambda b, p, i: (rb0 + b * nq + i, p)),
                pl.BlockSpec((t, LANES), lambda b, p, i: (sb0 + b, HEAD_PAIRS + p)),
                pl.BlockSpec((t, LANES), lambda b, p, i: (sb0 + b, 2 * HEAD_PAIRS + p))]
    args = [h_all, h_all, h_all]
    past, th = 0, 0
    if hist is not None:
        past = hist[0].shape[1]
        th = min(past, 512)
        in_specs += [pl.BlockSpec((None, past, LANES), lambda b, p, i: (b, 0, p))] * 2
        args += list(hist)
    aliases = {}
    if prev_out is not None:
        in_specs.append(pl.BlockSpec(memory_space=pl.ANY))
        args.append(prev_out)
        aliases = {len(args) - 1: 0}
    kern = functools.partial(_sb_kernel, tq=tq, th=th, past=past)
    if prev_out is not None:
        kern = _drop_last_input(kern, n_in=len(args))
    return pl.pallas_call(
        kern,
        grid=(batch, HEAD_PAIRS, nq),
        in_specs=in_specs,
        out_specs=pl.BlockSpec((tq, LANES), lambda b, p, i: (rb0 + b * nq + i, p)),
        out_shape=jax.ShapeDtypeStruct((m, SB_WIDTH), F32),
        scratch_shapes=[pltpu.VMEM((t, LANES), BF16)] * 2,
        input_output_aliases=aliases,
        compiler_params=_cparams(3),
        name="sb_attention",
    )(*args)


def _drop_last_input(kern, n_in):
    def wrapped(*refs):
        return kern(*refs[:n_in - 1], *refs[n_in:])
    return wrapped


def _pool_kernel(u_ref, hist_ref, w_ref, scale_ref, o_ref, ext_ref, *, tt, pos0):
    ti = pl.program_id(1)
    halo = 16

    @pl.when(ti == 0)
    def _():
        ext_ref[0:halo, :] = hist_ref[...]

    @pl.when(ti > 0)
    def _():
        ext_ref[0:halo, :] = ext_ref[tt:tt + halo, :]

    u = u_ref[...]
    ext_ref[halo:halo + tt, :] = u
    pos = pos0 + ti * tt + lax.broadcasted_iota(I32, (tt, 1), 0)
    for g, w in enumerate(POOL_WINDOWS):
        cols = slice(g * POOL_GC, (g + 1) * POOL_GC)
        s = u[:, cols]
        for k in range(1, w):
            s = s + ext_ref[halo - k:halo - k + tt, cols]
        cnt = jnp.minimum(pos + 1, w).astype(F32)
        pooled = s / cnt - u[:, cols]
        y = _dot(pooled.astype(BF16), w_ref[g].astype(BF16))
        o_ref[:, cols] = y * scale_ref[:, cols]


def _pool_mixer(u_all, row0, batch, t, tt, hist16, w_pool, scale, pos0, prev_out):
    m = u_all.shape[0]
    nt = t // tt
    rb0 = row0 // tt
    in_specs = [pl.BlockSpec((tt, POOL_WIDTH), lambda b, i: (rb0 + b * nt + i, 0)),
                pl.BlockSpec((None, 16, POOL_WIDTH), lambda b, i: (b, 0, 0)),
                pl.BlockSpec((len(POOL_WINDOWS), POOL_GC, POOL_GC), lambda b, i: (0, 0, 0)),
                pl.BlockSpec((1, POOL_WIDTH), lambda b, i: (0, 0))]
    args = [u_all, hist16, w_pool, scale]
    aliases = {}
    kern = functools.partial(_pool_kernel, tt=tt, pos0=pos0)
    if prev_out is not None:
        in_specs.append(pl.BlockSpec(memory_space=pl.ANY))
        args.append(prev_out)
        aliases = {len(args) - 1: 0}
        kern = _drop_last_input(kern, n_in=len(args))
    return pl.pallas_call(
        kern,
        grid=(batch, nt),
        in_specs=in_specs,
        out_specs=pl.BlockSpec((tt, POOL_WIDTH), lambda b, i: (rb0 + b * nt + i, 0)),
        out_shape=jax.ShapeDtypeStruct((m, POOL_WIDTH), F32),
        scratch_shapes=[pltpu.VMEM((tt + 16, POOL_WIDTH), F32)],
        input_output_aliases=aliases,
        compiler_params=_cparams(2),
        name="pool_mixer",
    )(*args)


def _rw_pre_kernel(c_ref, hist_ref, mu_ref, w0_ref, w2_ref, a0_ref, a2_ref, g2_ref,
                   rkv_ref, lw_ref, a_ref, g_ref, last_ref, *, tt):
    ti = pl.program_id(1)

    @pl.when(ti == 0)
    def _():
        last_ref[...] = hist_ref[...]

    c = c_ref[...]
    row = lax.broadcasted_iota(I32, (tt, 1), 0)
    prev = jnp.where(row == 0, last_ref[...], pltpu.roll(c, 1, 0))
    last_ref[...] = c[tt - 1:tt, :]
    xs = c + (prev - c) * mu_ref[...]
    rkv_ref[...] = xs[:, :3 * RW_WIDTH]
    lora = xs[:, 3 * RW_WIDTH:3 * RW_WIDTH + RW_LORA]
    wd = _dot(jnp.tanh(lora).astype(BF16), w2_ref[...].astype(BF16))
    sp = jax.nn.softplus(-(w0_ref[...] + wd))
    lw_ref[...] = -jnp.exp(-sp - 0.5)
    a_ref[...] = jax.nn.sigmoid(a0_ref[...] + _dot(lora.astype(BF16), a2_ref[...].astype(BF16)))
    gd = xs[:, 3 * RW_WIDTH + RW_LORA:]
    g_ref[...] = _dot(jax.nn.sigmoid(gd).astype(BF16), g2_ref[...].astype(BF16))


def _rw_pre(c_all, row0, batch, t, tt, shift_hist, mu, w0, w2p, a0, a2p, g2, prev_outs):
    m = c_all.shape[0]
    nt = t // tt
    rb0 = row0 // tt
    row_map = lambda b, i: (rb0 + b * nt + i, 0)
    const = lambda b, i: (0, 0)
    in_specs = [pl.BlockSpec((tt, RW_COLS), row_map),
                pl.BlockSpec((None, 1, RW_COLS), lambda b, i: (b, 0, 0)),
                pl.BlockSpec((1, RW_COLS), const),
                pl.BlockSpec((1, RW_WIDTH), const),
                pl.BlockSpec((RW_LORA, RW_WIDTH), const),
                pl.BlockSpec((1, RW_WIDTH), const),
                pl.BlockSpec((RW_LORA, RW_WIDTH), const),
                pl.BlockSpec((RW_GATE, RW_WIDTH), const)]
    args = [c_all, shift_hist, mu, w0, w2p, a0, a2p, g2]
    widths = (3 * RW_WIDTH, RW_WIDTH, RW_WIDTH, RW_WIDTH)
    aliases = {}
    kern = functools.partial(_rw_pre_kernel, tt=tt)
    if prev_outs is not None:
        n_real = len(args)
        for j, po in enumerate(prev_outs):
            in_specs.append(pl.BlockSpec(memory_space=pl.ANY))
            args.append(po)
            aliases[n_real + j] = j
        kern = _drop_inputs(kern, n_real, len(prev_outs))
    return pl.pallas_call(
        kern,
        grid=(batch, nt),
        in_specs=in_specs,
        out_specs=[pl.BlockSpec((tt, wd), row_map) for wd in widths],
        out_shape=[jax.ShapeDtypeStruct((m, wd), F32) for wd in widths],
        scratch_shapes=[pltpu.VMEM((1, RW_COLS), F32)],
        input_output_aliases=aliases,
        compiler_params=_cparams(2),
        name="rwkv_pre",
    )(*args)


def _drop_inputs(kern, n_real, n_drop):
    def wrapped(*refs):
        return kern(*refs[:n_real], *refs[n_real + n_drop:])
    return wrapped


def _rw_chunk(r, k0, v, lw, a, g, s_prev, prm, masks, c):
    kks, ka, rk, gng, gnb = prm
    l_incl, strict, incl, eye = masks
    lw_hi, lw_lo = _split2(lw)
    gcum = _dot(l_incl, lw_hi) + _dot(l_incl, lw_lo)
    e_in = jnp.exp(gcum)
    e_ex = jnp.exp(gcum - lw)
    e_neg = jnp.exp(-gcum)
    kk = k0 * kks
    kk = kk * lax.rsqrt(jnp.maximum(jnp.sum(kk * kk, axis=-1, keepdims=True), 1e-24))
    kmod = k0 * (1.0 + (a - 1.0) * ka)
    al = (-kk * e_ex).astype(BF16)
    be = kk * a * e_neg
    kc = kmod * e_neg
    rc = (r * e_in).astype(BF16)
    v_bf = v.astype(BF16)
    bk = jnp.concatenate([be, kc], axis=0).astype(BF16)
    m4 = _dot_nt(jnp.concatenate([al, rc], axis=0), bk)
    a_ab = m4[:c, :c] * strict
    a_ak = m4[:c, c:] * strict
    p_rb = m4[c:, :c] * incl
    p_rk = m4[c:, c:] * incl
    akv = _dot(jnp.concatenate([a_ak, p_rk], axis=0).astype(BF16), v_bf)
    p = a_ab
    tinv = eye + p
    span = 2
    while span < c:
        p_bf = p.astype(BF16)
        p = _dot(p_bf, p_bf)
        tinv = tinv + _dot(p.astype(BF16), tinv.astype(BF16))
        span *= 2
    tinv_bf = tinv.astype(BF16)
    w = _dot(tinv_bf, al)
    u_free = _dot(tinv_bf, akv[:c].astype(BF16))
    ws = _dot_nt(jnp.concatenate([w.astype(BF16), rc], axis=0), s_prev.astype(BF16))
    u = ws[:c] + u_free
    u_bf = u.astype(BF16)
    o = ws[c:] + _dot(p_rb.astype(BF16), u_bf) + akv[c:]
    s_new = (s_prev + _dot_tn(jnp.concatenate([u_bf, v_bf], axis=0), bk)) * e_in[c - 1:c, :]
    mu_o = jnp.mean(o, axis=-1, keepdims=True)
    oc = o - mu_o
    var_o = jnp.mean(oc * oc, axis=-1, keepdims=True)
    on = oc * lax.rsqrt(var_o + GN_EPS) * gng + gnb
    bonus = jnp.sum(r * kmod * rk, axis=-1, keepdims=True) * v
    return (on + bonus) * g, s_new


def _rw_scan_kernel(r_ref, k_ref, v_ref, lw_ref, a_ref, g_ref, kks_ref, ka_ref, rk_ref, gng_ref,
                    gnb_ref, s0_ref, o_ref, sout_ref, s_ref, *, tb, c):
    ti = pl.program_id(2)

    @pl.when(ti == 0)
    def _():
        s_ref[...] = s0_ref[...]

    ri = lax.broadcasted_iota(I32, (c, c), 0)
    ci = lax.broadcasted_iota(I32, (c, c), 1)
    masks = ((ri >= ci).astype(BF16), (ri > ci).astype(F32), (ri >= ci).astype(F32),
             (ri == ci).astype(F32))
    prms = []
    for h in range(2):
        sl = slice(h * HEAD_DIM, (h + 1) * HEAD_DIM)
        prms.append(tuple(p[:, sl] for p in (kks_ref[...], ka_ref[...], rk_ref[...], gng_ref[...],
                                             gnb_ref[...])))

    def body(ci_, states):
        rows = pl.ds(pl.multiple_of(ci_ * c, c), c)
        blocks = [ref[rows, :] for ref in (r_ref, k_ref, v_ref, lw_ref, a_ref, g_ref)]
        outs, new_states = [], []
        for h in range(2):
            sl = slice(h * HEAD_DIM, (h + 1) * HEAD_DIM)
            r, k0, v, lw, a, g = (x[:, sl] for x in blocks)
            out, s_new = _rw_chunk(r, k0, v, lw, a, g, states[h], prms[h], masks, c)
            outs.append(out)
            new_states.append(s_new)
        o_ref[rows, :] = jnp.concatenate(outs, axis=-1)
        return tuple(new_states)

    states = lax.fori_loop(0, tb // c, body, (s_ref[0], s_ref[1]))
    s_ref[0] = states[0]
    s_ref[1] = states[1]
    sout_ref[...] = s_ref[...]


def _rw_scan(rkv, lw, a, g, row0, batch, t, tb, c, params, wkv0, prev_out):
    m = rkv.shape[0]
    nt = t // tb
    rb0 = row0 // tb

    def col(off):
        return lambda b, p, i: (rb0 + b * nt + i, off + p)

    pspec = pl.BlockSpec((1, LANES), lambda b, p, i: (0, p))
    sspec = pl.BlockSpec((None, 2, HEAD_DIM, HEAD_DIM), lambda b, p, i: (b, p, 0, 0))
    in_specs = [pl.BlockSpec((tb, LANES), col(0)), pl.BlockSpec((tb, LANES), col(HEAD_PAIRS)),
                pl.BlockSpec((tb, LANES), col(2 * HEAD_PAIRS)),
                pl.BlockSpec((tb, LANES), col(0)), pl.BlockSpec((tb, LANES), col(0)),
                pl.BlockSpec((tb, LANES), col(0))] + [pspec] * 5 + [sspec]
    args = [rkv, rkv, rkv, lw, a, g] + list(params) + [wkv0]
    aliases = {}
    kern = functools.partial(_rw_scan_kernel, tb=tb, c=c)
    if prev_out is not None:
        n_real = len(args)
        in_specs.append(pl.BlockSpec(memory_space=pl.ANY))
        args.append(prev_out)
        aliases = {n_real: 0}
        kern = _drop_inputs(kern, n_real, 1)
    return pl.pallas_call(
        kern,
        grid=(batch, HEAD_PAIRS, nt),
        in_specs=in_specs,
        out_specs=[pl.BlockSpec((tb, LANES), col(0)), sspec],
        out_shape=[jax.ShapeDtypeStruct((m, RW_WIDTH), F32),
                   jax.ShapeDtypeStruct((batch, RW_HEADS, HEAD_DIM, HEAD_DIM), F32)],
        scratch_shapes=[pltpu.VMEM((2, HEAD_DIM, HEAD_DIM), F32)],
        input_output_aliases=aliases,
        compiler_params=_cparams(3),
        name="rwkv_scan",
    )(*args)


def _outproj_kernel(oa_ref, ob_ref, oc_ref, x_ref, w_ref, g_ref, b_ref, y_ref):
    mix = _dot(oa_ref[...].astype(BF16), w_ref[0:SB_WIDTH, :])
    mix += _dot(ob_ref[...].astype(BF16), w_ref[SB_WIDTH:SB_WIDTH + POOL_WIDTH, :])
    mix += _dot(oc_ref[...].astype(BF16), w_ref[SB_WIDTH + POOL_WIDTH:, :])
    y_ref[...] = _layer_norm(DN_ALPHA * x_ref[...] + mix, g_ref[...], b_ref[...])


def _outproj_ln(oa, ob, oc, x, w_bf, g, b, tm):
    m, d = x.shape
    row = lambda i: (i, 0)
    const = lambda i: (0, 0)
    return pl.pallas_call(
        _outproj_kernel,
        grid=(m // tm,),
        in_specs=[pl.BlockSpec((tm, SB_WIDTH), row), pl.BlockSpec((tm, POOL_WIDTH), row),
                  pl.BlockSpec((tm, RW_WIDTH), row), pl.BlockSpec((tm, d), row),
                  pl.BlockSpec(w_bf.shape, const), pl.BlockSpec((1, d), const),
                  pl.BlockSpec((1, d), const)],
        out_specs=pl.BlockSpec((tm, d), row),
        out_shape=jax.ShapeDtypeStruct((m, d), F32),
        compiler_params=_cparams(1),
        name="outproj_ln",
    )(oa, ob, oc, x, w_bf, g, b)


def _router_kernel(x_ref, w_ref, b_ref, idx_ref, gate_ref, *, tm):
    logits = _dot(x_ref[...], w_ref[...], HI) + b_ref[...]
    lane = lax.broadcasted_iota(I32, (tm, N_EXPERTS), 1).astype(F32)
    work = logits
    vals, idxs = [], []
    for _ in range(TOP_K):
        top = jnp.max(work, axis=1, keepdims=True)
        idx = jnp.min(jnp.where(work == top, lane, float(N_EXPERTS)), axis=1, keepdims=True)
        vals.append(top)
        idxs.append(idx)
        work = jnp.where(lane == idx, -jnp.inf, work)
    exps = [jnp.exp(v - vals[0]) for v in vals]
    total = exps[0] + exps[1] + exps[2] + exps[3]
    out_lane = lax.broadcasted_iota(I32, (tm, LANES), 1)
    idx_out = jnp.zeros((tm, LANES), F32)
    gate_out = jnp.zeros((tm, LANES), F32)
    for k in range(TOP_K):
        idx_out = jnp.where(out_lane == k, idxs[k], idx_out)
        gate_out = jnp.where(out_lane == k, exps[k] / total, gate_out)
    idx_ref[...] = idx_out.astype(I32)
    gate_ref[...] = gate_out


def _router(x, w, b, tm):
    m, d = x.shape
    return pl.pallas_call(
        functools.partial(_router_kernel, tm=tm),
        grid=(m // tm,),
        in_specs=[pl.BlockSpec((tm, d), lambda i: (i, 0)),
                  pl.BlockSpec((d, N_EXPERTS), lambda i: (0, 0)),
                  pl.BlockSpec((1, N_EXPERTS), lambda i: (0, 0))],
        out_specs=[pl.BlockSpec((tm, LANES), lambda i: (i, 0))] * 2,
        out_shape=[jax.ShapeDtypeStruct((m, LANES), I32), jax.ShapeDtypeStruct((m, LANES), F32)],
        compiler_params=_cparams(1),
        name="router",
    )(x, w, b)


def _row_copy(src_hbm, dst_ref, sem, src_row, dst_row):
    return pltpu.make_async_copy(src_hbm.at[pl.ds(src_row, 1), :], dst_ref.at[pl.ds(dst_row, 1), :], sem)


def _gather_kernel(tok_ref, x_hbm, o_ref, sem, *, bm):
    def start(r, carry):
        _row_copy(x_hbm, o_ref, sem, tok_ref[0, 0, r], r).start()
        return carry

    lax.fori_loop(0, bm, start, 0)

    def wait(r, carry):
        _row_copy(x_hbm, o_ref, sem, 0, r).wait()
        return carry

    lax.fori_loop(0, bm, wait, 0)


def _gather_rows(x, buf_tok, bm):
    d = x.shape[1]
    nb = buf_tok.shape[0] // bm
    return pl.pallas_call(
        functools.partial(_gather_kernel, bm=bm),
        grid=(nb,),
        in_specs=[pl.BlockSpec((1, 1, bm), lambda i: (i, 0, 0), memory_space=pltpu.SMEM),
                  pl.BlockSpec(memory_space=pl.ANY)],
        out_specs=pl.BlockSpec((bm, d), lambda i: (i, 0)),
        out_shape=jax.ShapeDtypeStruct((nb * bm, d), F32),
        scratch_shapes=[pltpu.SemaphoreType.DMA(())],
        compiler_params=_cparams(1),
        name="moe_gather",
    )(buf_tok.reshape(nb, 1, bm), x)


def _expert_up_kernel(be_ref, nu_ref, x_ref, wg_ref, wl_ref, bg_ref, bl_ref, act_ref):
    @pl.when(pl.program_id(1) < nu_ref[0])
    def _():
        x = x_ref[...].astype(BF16)
        h_glu = _dot(x, wg_ref[...].astype(BF16)) + bg_ref[...]
        h_lin = _dot(x, wl_ref[...].astype(BF16)) + bl_ref[...]
        h_glu = jnp.minimum(h_glu, SWIGLU_LIMIT)
        h_lin = jnp.clip(h_lin, -SWIGLU_LIMIT, SWIGLU_LIMIT)
        act = h_glu * jax.nn.sigmoid(SWIGLU_ALPHA * h_glu) * (h_lin + 1.0)
        act_ref[...] = act.astype(BF16)


def _expert_up(xb, w1, b1, layer, block_e, n_used, bm, tn):
    rows, d = xb.shape
    de = w1.shape[3] // 2
    nj = de // tn
    nb = rows // bm

    def blk(i, nu):
        return jnp.minimum(i, nu[0] - 1)

    grid_spec = pltpu.PrefetchScalarGridSpec(
        num_scalar_prefetch=2,
        grid=(nj, nb),
        in_specs=[pl.BlockSpec((bm, d), lambda j, i, be, nu: (blk(i, nu), 0)),
                  pl.BlockSpec((None, None, d, tn), lambda j, i, be, nu: (layer, be[blk(i, nu)], 0, j)),
                  pl.BlockSpec((None, None, d, tn),
                               lambda j, i, be, nu: (layer, be[blk(i, nu)], 0, nj + j)),
                  pl.BlockSpec((None, None, 1, tn), lambda j, i, be, nu: (layer, be[blk(i, nu)], 0, j)),
                  pl.BlockSpec((None, None, 1, tn),
                               lambda j, i, be, nu: (layer, be[blk(i, nu)], 0, nj + j))],
        out_specs=pl.BlockSpec((bm, tn), lambda j, i, be, nu: (blk(i, nu), j)),
    )
    return pl.pallas_call(
        _expert_up_kernel,
        grid_spec=grid_spec,
        out_shape=jax.ShapeDtypeStruct((rows, de), BF16),
        compiler_params=_cparams(2),
        name="moe_up",
    )(block_e, n_used, xb, w1, w1, b1, b1)


def _expert_down_kernel(be_ref, nu_ref, a_ref, w_ref, b_ref, y_ref):
    @pl.when(pl.program_id(1) < nu_ref[0])
    def _():
        y_ref[...] = _dot(a_ref[...], w_ref[...].astype(BF16)) + b_ref[...]


def _expert_down(act, w2, b2, layer, block_e, n_used, bm, tn):
    rows, de = act.shape
    d = w2.shape[3]
    nj = d // tn
    nb = rows // bm

    def blk(i, nu):
        return jnp.minimum(i, nu[0] - 1)

    grid_spec = pltpu.PrefetchScalarGridSpec(
        num_scalar_prefetch=2,
        grid=(nj, nb),
        in_specs=[pl.BlockSpec((bm, de), lambda j, i, be, nu: (blk(i, nu), 0)),
                  pl.BlockSpec((None, None, de, tn), lambda j, i, be, nu: (layer, be[blk(i, nu)], 0, j)),
                  pl.BlockSpec((None, None, 1, tn), lambda j, i, be, nu: (layer, be[blk(i, nu)], 0, j))],
        out_specs=pl.BlockSpec((bm, tn), lambda j, i, be, nu: (blk(i, nu), j)),
    )
    return pl.pallas_call(
        _expert_down_kernel,
        grid_spec=grid_spec,
        out_shape=jax.ShapeDtypeStruct((rows, d), F32),
        compiler_params=_cparams(2),
        name="moe_down",
    )(block_e, n_used, act, w2, b2)


def _combine_kernel(dest_ref, y_hbm, gate_ref, x_ref, g_ref, b_ref, o_ref, rows_ref, sem, *, tm):
    def start(r, carry):
        for k in range(TOP_K):
            _row_copy(y_hbm, rows_ref.at[k], sem, dest_ref[0, 0, r * TOP_K + k], r).start()
        return carry

    lax.fori_loop(0, tm, start, 0)

    def wait(r, carry):
        for k in range(TOP_K):
            _row_copy(y_hbm, rows_ref.at[k], sem, 0, r).wait()
        return carry

    lax.fori_loop(0, tm, wait, 0)
    gates = gate_ref[...]
    ffn = gates[:, 0:1] * rows_ref[0]
    for k in range(1, TOP_K):
        ffn += gates[:, k:k + 1] * rows_ref[k]
    o_ref[...] = _layer_norm(DN_ALPHA * x_ref[...] + ffn, g_ref[...], b_ref[...])


def _combine_ln(y, dest, gates, x, g, b, tm):
    m, d = x.shape
    nb = m // tm
    return pl.pallas_call(
        functools.partial(_combine_kernel, tm=tm),
        grid=(nb,),
        in_specs=[pl.BlockSpec((1, 1, tm * TOP_K), lambda i: (i, 0, 0), memory_space=pltpu.SMEM),
                  pl.BlockSpec(memory_space=pl.ANY),
                  pl.BlockSpec((tm, LANES), lambda i: (i, 0)),
                  pl.BlockSpec((tm, d), lambda i: (i, 0)),
                  pl.BlockSpec((1, d), lambda i: (0, 0)),
                  pl.BlockSpec((1, d), lambda i: (0, 0))],
        out_specs=pl.BlockSpec((tm, d), lambda i: (i, 0)),
        out_shape=jax.ShapeDtypeStruct((m, d), F32),
        scratch_shapes=[pltpu.VMEM((TOP_K, tm, d), F32), pltpu.SemaphoreType.DMA(())],
        compiler_params=_cparams(1),
        name="moe_combine_ln",
    )(dest.reshape(nb, 1, tm * TOP_K), y, gates, x, g, b)


def _routing_tables(top_i, bm):
    n = top_i.shape[0]
    nk = n * TOP_K
    flat_e = top_i.reshape(nk)
    onehot = (flat_e[:, None] == jnp.arange(N_EXPERTS, dtype=I32)[None, :]).astype(I32)
    rank = jnp.sum((jnp.cumsum(onehot, axis=0) - onehot) * onehot, axis=1)
    counts = jnp.sum(onehot, axis=0)
    padded = (counts + bm - 1) // bm * bm
    pad_end = jnp.cumsum(padded)
    pad_start = pad_end - padded
    dest = (pad_start[flat_e] + rank).astype(I32)
    nb = (nk + N_EXPERTS * (bm - 1)) // bm
    buf_tok = jnp.zeros((nb * bm,), I32).at[dest].set(jnp.arange(nk, dtype=I32) // TOP_K)
    block_e = jnp.minimum(jnp.searchsorted(pad_end, jnp.arange(nb, dtype=I32) * bm, side='right'),
                          N_EXPERTS - 1).astype(I32)
    n_used = (pad_end[-1:] // bm).astype(I32)
    return dest, buf_tok, block_e, n_used


def _moe_ln(x, lp, big, layer):
    idx, gates = _router(x, lp['w_router'], lp['b_router'].reshape(1, N_EXPERTS), tm=256)
    dest, buf_tok, block_e, n_used = _routing_tables(idx[:, :TOP_K], MOE_BM)
    xb = _gather_rows(x, buf_tok, MOE_BM)
    nl, e, d, de2 = big['w_e1'].shape
    act = _expert_up(xb, big['w_e1'], big['b_e1'].reshape(nl, e, 1, de2), layer, block_e, n_used,
                     MOE_BM, tn=512)
    y = _expert_down(act, big['w_e2'], big['b_e2'].reshape(nl, e, 1, d), layer, block_e, n_used,
                     MOE_BM, tn=512)
    return _combine_ln(y, dest, gates, x, lp['ln_ffn_g'].reshape(1, d), lp['ln_ffn_b'].reshape(1, d),
                       tm=128)


def _layer(x, lp, big, layer, st, n_p, bp, tp, bs, ts):
    d = x.shape[1]
    h = _inproj(x, big['w_in'], layer, tm=768, tn=768)
    qkv_w = 3 * SB_WIDTH
    u = h[:, qkv_w:qkv_w + POOL_WIDTH]
    c = h[:, qkv_w + POOL_WIDTH:]

    o_a = _sb_attention(h, 0, bp, tp, 256, None, None)
    hist = (st['cache_k'].reshape(bs, -1, SB_WIDTH), st['cache_v'].reshape(bs, -1, SB_WIDTH))
    o_a = _sb_attention(h, n_p, bs, ts, ts, hist, o_a)
    past = hist[0].shape[1]

    w_pool = lp['pool_w']
    scale = lp['pool_scale'].reshape(1, POOL_WIDTH)
    o_b = _pool_mixer(u, 0, bp, tp, 512, jnp.zeros((bp, 16, POOL_WIDTH), F32), w_pool, scale, 0, None)
    hist16 = jnp.pad(st['state_pool'], ((0, 0), (1, 0), (0, 0)))
    o_b = _pool_mixer(u, n_p, bs, ts, ts, hist16, w_pool, scale, past, o_b)

    zero_lora = jnp.zeros((RW_LORA // 2, RW_WIDTH), F32)
    w2p = jnp.concatenate([lp['rw_w2'], zero_lora], axis=0)
    a2p = jnp.concatenate([zero_lora, lp['rw_a2']], axis=0)
    pre_args = (lp['rw_mu'].reshape(1, RW_COLS), lp['rw_w0'].reshape(1, RW_WIDTH), w2p,
                lp['rw_a0'].reshape(1, RW_WIDTH), a2p, lp['rw_g2'])
    pre = _rw_pre(c, 0, bp, tp, 256, jnp.zeros((bp, 1, RW_COLS), F32), *pre_args, None)
    rkv, lw, a, g = _rw_pre(c, n_p, bs, ts, ts, st['state_shift'], *pre_args, pre)
    prm = tuple(lp[k].reshape(1, RW_WIDTH) for k in ('rw_kk', 'rw_ka', 'rw_rk', 'rw_gn_g', 'rw_gn_b'))
    o_c, wkv_p = _rw_scan(rkv, lw, a, g, 0, bp, tp, 512, 64, prm,
                          jnp.zeros((bp, RW_HEADS, HEAD_DIM, HEAD_DIM), F32), None)
    o_c, wkv_s = _rw_scan(rkv, lw, a, g, n_p, bs, ts, ts, ts, prm, st['state_wkv'], o_c)

    x1 = _outproj_ln(o_a, o_b, o_c, x, lp['w_out'].astype(BF16), lp['ln_mix_g'].reshape(1, d),
                     lp['ln_mix_b'].reshape(1, d), tm=256)
    x2 = _moe_ln(x1, lp, big, layer)

    def states(rows, b, t, wkv):
        hb = rows.reshape(b, t, -1)
        k_new = hb[:, :, SB_WIDTH:2 * SB_WIDTH].reshape(b, t, SB_HEADS, HEAD_DIM)
        v_new = hb[:, :, 2 * SB_WIDTH:qkv_w].reshape(b, t, SB_HEADS, HEAD_DIM)
        pool_new = hb[:, t - POOL_HIST:, qkv_w:qkv_w + POOL_WIDTH]
        shift_new = hb[:, t - 1:, qkv_w + POOL_WIDTH:]
        return k_new, v_new, pool_new, shift_new, wkv

    return x2, states(h[:n_p], bp, tp, wkv_p), states(h[n_p:], bs, ts, wkv_s)


def kernel(x_prompt, x_sample, cache_k, cache_v, state_pool, state_shift, state_wkv, w_in, w_out, ln_mix_g, ln_mix_b, ln_ffn_g, ln_ffn_b, pool_w, pool_scale, rw_mu, rw_w0, rw_w2, rw_a0, rw_a2, rw_g2, rw_kk, rw_ka, rw_rk, rw_gn_g, rw_gn_b, w_router, b_router, w_e1, b_e1, w_e2, b_e2):
    bp, tp, d = x_prompt.shape
    bs, ts, _ = x_sample.shape
    n_p = bp * tp
    x = jnp.concatenate([x_prompt.reshape(n_p, d), x_sample.reshape(bs * ts, d)], axis=0)
    big = dict(w_in=w_in, w_e1=w_e1, b_e1=b_e1, w_e2=w_e2, b_e2=b_e2)
    weights = dict(w_out=w_out, ln_mix_g=ln_mix_g, ln_mix_b=ln_mix_b, ln_ffn_g=ln_ffn_g,
                   ln_ffn_b=ln_ffn_b, pool_w=pool_w, pool_scale=pool_scale, rw_mu=rw_mu, rw_w0=rw_w0,
                   rw_w2=rw_w2, rw_a0=rw_a0, rw_a2=rw_a2, rw_g2=rw_g2, rw_kk=rw_kk, rw_ka=rw_ka,
                   rw_rk=rw_rk, rw_gn_g=rw_gn_g, rw_gn_b=rw_gn_b, w_router=w_router,
                   b_router=b_router)
    st_p, st_s = [], []
    for l in range(w_in.shape[0]):
        lp = {k: v[l] for k, v in weights.items()}
        st = dict(cache_k=cache_k[l], cache_v=cache_v[l], state_pool=state_pool[l],
                  state_shift=state_shift[l], state_wkv=state_wkv[l])
        x, sp, ss = _layer(x, lp, big, l, st, n_p, bp, tp, bs, ts)
        st_p.append(sp)
        st_s.append(ss)
    y_prompt = x[:n_p].reshape(bp, tp, d)
    y_sample = x[n_p:].reshape(bs, ts, d)
    k_p, v_p, pool_p, shift_p, wkv_p = (jnp.stack([s[i] for s in st_p]) for i in range(5))
    k_s, v_s, pool_s, shift_s, wkv_s = (jnp.stack([s[i] for s in st_s]) for i in range(5))
    return (y_prompt, y_sample, k_p, v_p, pool_p, shift_p, wkv_p, k_s, v_s, pool_s, shift_s, wkv_s)
```

```python
import functools

import jax
import jax.numpy as jnp
from jax import lax
from jax.experimental import pallas as pl
from jax.experimental.pallas import tpu as pltpu

F32 = jnp.float32
BF16 = jnp.bfloat16
I32 = jnp.int32

HEAD_DIM = 64
SB_HEADS = 12
SB_WIDTH = SB_HEADS * HEAD_DIM
POOL_WINDOWS = (2, 4, 8, 16)
POOL_GC = 128
POOL_WIDTH = POOL_GC * len(POOL_WINDOWS)
POOL_HIST = max(POOL_WINDOWS) - 1
RW_HEADS = 12
RW_WIDTH = RW_HEADS * HEAD_DIM
RW_LORA = 128
RW_GATE = 128
RW_COLS = 3 * RW_WIDTH + RW_LORA + RW_GATE
N_EXPERTS = 32
TOP_K = 4
SWIGLU_LIMIT = 7.0
SWIGLU_ALPHA = 1.702
DEPTH = 2
DN_ALPHA = (2.0 * DEPTH) ** 0.25
LN_EPS = 1e-5
GN_EPS = 64e-5

LANES = 128
HEAD_PAIRS = SB_WIDTH // LANES
VMEM_LIMIT = 56 * 1024 * 1024
ATT_LOG_FLOOR = -110.0

MOE_BM = 256
RW_PREP_GROUP = 8


def _cparams(n_axes):
    return pltpu.CompilerParams(dimension_semantics=("arbitrary",) * n_axes,
                                vmem_limit_bytes=VMEM_LIMIT)


def _dot(a, b, precision=None):
    return jnp.dot(a, b, preferred_element_type=F32, precision=precision)


def _dot_nt(a, b, precision=None):
    return lax.dot_general(a, b, (((1,), (1,)), ((), ())), preferred_element_type=F32,
                           precision=precision)


def _dot_tn(a, b, precision=None):
    return lax.dot_general(a, b, (((0,), (0,)), ((), ())), preferred_element_type=F32,
                           precision=precision)


def _layer_norm(x, g, b):
    mu = jnp.mean(x, axis=-1, keepdims=True)
    xc = x - mu
    var = jnp.mean(xc * xc, axis=-1, keepdims=True)
    return xc * lax.rsqrt(var + LN_EPS) * g + b


def _inproj_kernel(x_ref, w_ref, o_ref, wbf_ref):
    @pl.when(pl.program_id(1) == 0)
    def _():
        wbf_ref[...] = w_ref[...].astype(BF16)

    o_ref[...] = _dot(x_ref[...].astype(BF16), wbf_ref[...])


def _inproj(x, w, layer, tm, tn):
    m, k = x.shape
    n = w.shape[2]
    return pl.pallas_call(
        _inproj_kernel,
        grid=(n // tn, m // tm),
        in_specs=[pl.BlockSpec((tm, k), lambda j, i: (i, 0)),
                  pl.BlockSpec((None, k, tn), lambda j, i: (layer, 0, j))],
        out_specs=pl.BlockSpec((tm, tn), lambda j, i: (i, j)),
        out_shape=jax.ShapeDtypeStruct((m, n), F32),
        scratch_shapes=[pltpu.VMEM((k, tn), BF16)],
        compiler_params=_cparams(2),
        name="inproj",
    )(x, w)


def _split2(a):
    hi = a.astype(BF16)
    lo = (a - hi.astype(F32)).astype(BF16)
    return hi, lo


def _suffix_matrix(n):
    return (lax.broadcasted_iota(I32, (n, n), 0) > lax.broadcasted_iota(I32, (n, n), 1)).astype(BF16)


def _sb_block(qhs, k_bf, v_bf, suffix, carries, mask):
    heads = range(2)
    z = [_dot_nt(qhs[h], k_bf) for h in heads]
    sp = [jnp.maximum(z_, 0.0) + jnp.log1p(jnp.exp(-jnp.abs(z_))) for z_ in z]
    log_stay = [-s_ if mask is None else jnp.where(mask, -s_, 0.0) for s_ in sp]
    parts = [_split2(l_) for l_ in log_stay]
    hi = [_dot(parts[h][0], suffix) for h in heads]
    lo = [_dot(parts[h][1], suffix) for h in heads]
    after = [hi[h] + lo[h] + carries[h] for h in heads]
    att = [jnp.exp(z[h] - sp[h] + after[h]) for h in heads]
    if mask is not None:
        att = [jnp.where(mask, a_, 0.0) for a_ in att]
    pv = [_dot(att[h].astype(BF16), v_bf) for h in heads]
    return pv, [after[h][:, 0:1] + log_stay[h][:, 0:1] for h in heads]


def _sb_live(carries):
    return (jnp.max(jnp.maximum(carries[0], carries[1])) > ATT_LOG_FLOOR).astype(I32)


def _sb_sweep(qhs, k_ref, v_ref, tk, n_blocks, suffix, live, accs, carries):
    def cond(s):
        return jnp.logical_and(s[0] < n_blocks, s[1] > 0)

    def body(s):
        step, _, accs, carries = s
        rows = pl.ds(pl.multiple_of((n_blocks - 1 - step) * tk, tk), tk)
        k_bf = k_ref[rows, :].astype(BF16)
        v_bf = v_ref[rows, :].astype(BF16)
        pv, carries = _sb_block(qhs, k_bf, v_bf, suffix, carries, None)
        accs = tuple(accs[h] + pv[h] for h in range(2))
        return step + 1, _sb_live(carries), accs, tuple(carries)

    _, live, accs, carries = lax.while_loop(cond, body, (jnp.int32(0), live, accs, carries))
    return live, accs, carries


def _sb_kernel(*refs, tq, th, past):
    if past:
        q_ref, kn_ref, vn_ref, kh_ref, vh_ref, o_ref, kbf_ref, vbf_ref = refs
    else:
        q_ref, kn_ref, vn_ref, o_ref, kbf_ref, vbf_ref = refs
    qi = pl.program_id(2)

    @pl.when(qi == 0)
    def _():
        kbf_ref[...] = kn_ref[...].astype(BF16)
        vbf_ref[...] = vn_ref[...].astype(BF16)

    q = q_ref[...] * (HEAD_DIM ** -0.5)
    lane = lax.broadcasted_iota(I32, (tq, LANES), 1)
    qhs = [jnp.where(lane < HEAD_DIM, q, 0.0).astype(BF16),
           jnp.where(lane >= HEAD_DIM, q, 0.0).astype(BF16)]
    suffix_new = _suffix_matrix(tq)
    diag_mask = lax.broadcasted_iota(I32, (tq, tq), 1) < lax.broadcasted_iota(I32, (tq, tq), 0)
    rows = pl.ds(pl.multiple_of(qi * tq, tq), tq)
    k_bf = kbf_ref[rows, :]
    v_bf = vbf_ref[rows, :]
    accs, carries = _sb_block(qhs, k_bf, v_bf, suffix_new, [jnp.zeros((tq, 1), F32)] * 2, diag_mask)
    accs, carries = tuple(accs), tuple(carries)
    live, accs, carries = _sb_sweep(qhs, kbf_ref, vbf_ref, tq, qi, suffix_new, _sb_live(carries),
                                    accs, carries)
    if past:
        _, accs, carries = _sb_sweep(qhs, kh_ref, vh_ref, th, past // th, _suffix_matrix(th), live,
                                     accs, carries)
    o_ref[...] = jnp.where(lane < HEAD_DIM, accs[0], accs[1])


def _sb_attention(h_all, row0, batch, t, tq, hist, prev_out):
    m = h_all.shape[0]
    nq = t // tq
    rb0 = row0 // tq
    sb0 = row0 // t
    in_specs = [pl.BlockSpec((tq, LANES), lambda b, p, i: (rb0 + b * nq + i, p)),
                pl.BlockSpec((t, LANES), lambda b, p, i: (sb0 + b, HEAD_PAIRS + p)),
                pl.BlockSpec((t, LANES), lambda b, p, i: (sb0 + b, 2 * HEAD_PAIRS + p))]
    args = [h_all, h_all, h_all]
    past, th = 0, 0
    if hist is not None:
        past = hist[0].shape[1]
        th = min(past, 512)
        in_specs += [pl.BlockSpec((None, past, LANES), lambda b, p, i: (b, 0, p))] * 2
        args += list(hist)
    aliases = {}
    if prev_out is not None:
        in_specs.append(pl.BlockSpec(memory_space=pl.ANY))
        args.append(prev_out)
        aliases = {len(args) - 1: 0}
    kern = functools.partial(_sb_kernel, tq=tq, th=th, past=past)
    if prev_out is not None:
        kern = _drop_last_input(kern, n_in=len(args))
    return pl.pallas_call(
        kern,
        grid=(batch, HEAD_PAIRS, nq),
        in_specs=in_specs,
        out_specs=pl.BlockSpec((tq, LANES), lambda b, p, i: (rb0 + b * nq + i, p)),
        out_shape=jax.ShapeDtypeStruct((m, SB_WIDTH), F32),
        scratch_shapes=[pltpu.VMEM((t, LANES), BF16)] * 2,
        input_output_aliases=aliases,
        compiler_params=_cparams(3),
        name="sb_attention",
    )(*args)


def _drop_last_input(kern, n_in):
    def wrapped(*refs):
        return kern(*refs[:n_in - 1], *refs[n_in:])
    return wrapped


def _pool_kernel(u_ref, hist_ref, w_ref, scale_ref, o_ref, ext_ref, *, tt, pos0):
    ti = pl.program_id(1)
    halo = 16

    @pl.when(ti == 0)
    def _():
        ext_ref[0:halo, :] = hist_ref[...]

    @pl.when(ti > 0)
    def _():
        ext_ref[0:halo, :] = ext_ref[tt:tt + halo, :]

    u = u_ref[...]
    ext_ref[halo:halo + tt, :] = u
    pos = pos0 + ti * tt + lax.broadcasted_iota(I32, (tt, 1), 0)
    for g, w in enumerate(POOL_WINDOWS):
        cols = slice(g * POOL_GC, (g + 1) * POOL_GC)
        s = u[:, cols]
        for k in range(1, w):
            s = s + ext_ref[halo - k:halo - k + tt, cols]
        cnt = jnp.minimum(pos + 1, w).astype(F32)
        pooled = s / cnt - u[:, cols]
        y = _dot(pooled.astype(BF16), w_ref[g].astype(BF16))
        o_ref[:, cols] = y * scale_ref[:, cols]


def _pool_mixer(u_all, row0, batch, t, tt, hist16, w_pool, scale, pos0, prev_out):
    m = u_all.shape[0]
    nt = t // tt
    rb0 = row0 // tt
    in_specs = [pl.BlockSpec((tt, POOL_WIDTH), lambda b, i: (rb0 + b * nt + i, 0)),
                pl.BlockSpec((None, 16, POOL_WIDTH), lambda b, i: (b, 0, 0)),
                pl.BlockSpec((len(POOL_WINDOWS), POOL_GC, POOL_GC), lambda b, i: (0, 0, 0)),
                pl.BlockSpec((1, POOL_WIDTH), lambda b, i: (0, 0))]
    args = [u_all, hist16, w_pool, scale]
    aliases = {}
    kern = functools.partial(_pool_kernel, tt=tt, pos0=pos0)
    if prev_out is not None:
        in_specs.append(pl.BlockSpec(memory_space=pl.ANY))
        args.append(prev_out)
        aliases = {len(args) - 1: 0}
        kern = _drop_last_input(kern, n_in=len(args))
    return pl.pallas_call(
        kern,
        grid=(batch, nt),
        in_specs=in_specs,
        out_specs=pl.BlockSpec((tt, POOL_WIDTH), lambda b, i: (rb0 + b * nt + i, 0)),
        out_shape=jax.ShapeDtypeStruct((m, POOL_WIDTH), F32),
        scratch_shapes=[pltpu.VMEM((tt + 16, POOL_WIDTH), F32)],
        input_output_aliases=aliases,
        compiler_params=_cparams(2),
        name="pool_mixer",
    )(*args)


def _rw_pre_kernel(c_ref, hist_ref, mu_ref, w0_ref, w2_ref, a0_ref, a2_ref, g2_ref,
                   rkv_ref, lw_ref, a_ref, g_ref, last_ref, *, tt):
    ti = pl.program_id(1)

    @pl.when(ti == 0)
    def _():
        last_ref[...] = hist_ref[...]

    c = c_ref[...]
    row = lax.broadcasted_iota(I32, (tt, 1), 0)
    prev = jnp.where(row == 0, last_ref[...], pltpu.roll(c, 1, 0))
    last_ref[...] = c[tt - 1:tt, :]
    xs = c + (prev - c) * mu_ref[...]
    rkv_ref[...] = xs[:, :3 * RW_WIDTH]
    lora = xs[:, 3 * RW_WIDTH:3 * RW_WIDTH + RW_LORA]
    wd = _dot(jnp.tanh(lora).astype(BF16), w2_ref[...].astype(BF16))
    sp = jax.nn.softplus(-(w0_ref[...] + wd))
    lw_ref[...] = -jnp.exp(-sp - 0.5)
    a_ref[...] = jax.nn.sigmoid(a0_ref[...] + _dot(lora.astype(BF16), a2_ref[...].astype(BF16)))
    gd = xs[:, 3 * RW_WIDTH + RW_LORA:]
    g_ref[...] = _dot(jax.nn.sigmoid(gd).astype(BF16), g2_ref[...].astype(BF16))


def _rw_pre(c_all, row0, batch, t, tt, shift_hist, mu, w0, w2p, a0, a2p, g2, prev_outs):
    m = c_all.shape[0]
    nt = t // tt
    rb0 = row0 // tt
    row_map = lambda b, i: (rb0 + b * nt + i, 0)
    const = lambda b, i: (0, 0)
    in_specs = [pl.BlockSpec((tt, RW_COLS), row_map),
                pl.BlockSpec((None, 1, RW_COLS), lambda b, i: (b, 0, 0)),
                pl.BlockSpec((1, RW_COLS), const),
                pl.BlockSpec((1, RW_WIDTH), const),
                pl.BlockSpec((RW_LORA, RW_WIDTH), const),
                pl.BlockSpec((1, RW_WIDTH), const),
                pl.BlockSpec((RW_LORA, RW_WIDTH), const),
                pl.BlockSpec((RW_GATE, RW_WIDTH), const)]
    args = [c_all, shift_hist, mu, w0, w2p, a0, a2p, g2]
    widths = (3 * RW_WIDTH, RW_WIDTH, RW_WIDTH, RW_WIDTH)
    aliases = {}
    kern = functools.partial(_rw_pre_kernel, tt=tt)
    if prev_outs is not None:
        n_real = len(args)
        for j, po in enumerate(prev_outs):
            in_specs.append(pl.BlockSpec(memory_space=pl.ANY))
            args.append(po)
            aliases[n_real + j] = j
        kern = _drop_inputs(kern, n_real, len(prev_outs))
    return pl.pallas_call(
        kern,
        grid=(batch, nt),
        in_specs=in_specs,
        out_specs=[pl.BlockSpec((tt, wd), row_map) for wd in widths],
        out_shape=[jax.ShapeDtypeStruct((m, wd), F32) for wd in widths],
        scratch_shapes=[pltpu.VMEM((1, RW_COLS), F32)],
        input_output_aliases=aliases,
        compiler_params=_cparams(2),
        name="rwkv_pre",
    )(*args)


def _drop_inputs(kern, n_real, n_drop):
    def wrapped(*refs):
        return kern(*refs[:n_real], *refs[n_real + n_drop:])
    return wrapped


def _head_ones():
    r = lax.broadcasted_iota(I32, (LANES, LANES), 0) // HEAD_DIM
    c = lax.broadcasted_iota(I32, (LANES, LANES), 1) // HEAD_DIM
    return (r == c).astype(BF16)


def _head_sum(x, ones_bd):
    hi, lo = _split2(x)
    return _dot(hi, ones_bd) + _dot(lo, ones_bd)


def _stack_heads(x):
    lane = lax.broadcasted_iota(I32, x.shape, 1)
    zero = jnp.zeros_like(x)
    return jnp.concatenate([jnp.where(lane < HEAD_DIM, x, zero), jnp.where(lane >= HEAD_DIM, x, zero)],
                           axis=0)


def _rw_scan_kernel(r_ref, k_ref, v_ref, lw_ref, a_ref, g_ref, kks_ref, ka_ref, rk_ref, gng_ref,
                    gnb_ref, s0_ref, o_ref, sout_ref, st_ref, o_s, lhs_s, ufree_s, prb_s, akvo_s, bkt_s,
                    v_s, e_s, *, tb, c):
    ti = pl.program_id(2)
    n = 2 * c
    nchunk = tb // c

    @pl.when(ti == 0)
    def _():
        st_ref[...] = s0_ref[...]

    ones_bd = _head_ones()
    r = r_ref[...]
    k0 = k_ref[...]
    v = v_ref[...]
    lw = lw_ref[...]
    a = a_ref[...]
    kk = k0 * kks_ref[...]
    kk = kk * lax.rsqrt(jnp.maximum(_head_sum(kk * kk, ones_bd), 1e-24))
    kmod = k0 * (1.0 + (a - 1.0) * ka_ref[...])
    ri = lax.broadcasted_iota(I32, (tb, tb), 0)
    ci = lax.broadcasted_iota(I32, (tb, tb), 1)
    l_chunk = jnp.logical_and(ri >= ci, ri // c == ci // c).astype(BF16)
    lw_hi, lw_lo = _split2(lw)
    gcum = _dot(l_chunk, lw_hi) + _dot(l_chunk, lw_lo)
    e_in = jnp.exp(gcum)
    e_neg = jnp.exp(-gcum)
    al_all = (-kk * jnp.exp(gcum - lw)).astype(BF16)
    be_all = (kk * a * e_neg).astype(BF16)
    kc_all = (kmod * e_neg).astype(BF16)
    rc_all = (r * e_in).astype(BF16)
    bonus = _head_sum(r * kmod * rk_ref[...], ones_bd) * v

    rn = lax.broadcasted_iota(I32, (n, n), 0)
    cn = lax.broadcasted_iota(I32, (n, n), 1)
    same_head = rn // c == cn // c
    strict = jnp.logical_and(same_head, cn < rn).astype(F32)
    incl = jnp.logical_and(same_head, cn <= rn).astype(F32)
    eye = (rn == cn).astype(F32)

    def prepare(group):
        each = lambda f, *lists: [f(*xs) for xs in zip(*lists)]
        rows = [slice(j * c, (j + 1) * c) for j in group]
        al = [_stack_heads(al_all[r_]) for r_ in rows]
        rc = [_stack_heads(rc_all[r_]) for r_ in rows]
        bk = [jnp.concatenate([_stack_heads(be_all[r_]), _stack_heads(kc_all[r_])], axis=0) for r_ in rows]
        vs = [_stack_heads(v[r_]).astype(BF16) for r_ in rows]
        m4 = each(lambda al_, rc_, bk_: _dot_nt(jnp.concatenate([al_, rc_], axis=0), bk_), al, rc, bk)
        a_ab = [m[:n, :n] * strict for m in m4]
        lower = [jnp.concatenate([m[:n, n:] * strict, m[n:, n:] * incl], axis=0).astype(BF16) for m in m4]
        p_rb = [(m[n:, :n] * incl).astype(BF16) for m in m4]
        akv = each(_dot, lower, vs)
        tinv = [eye + a_ for a_ in a_ab]
        p_bf = [a_.astype(BF16) for a_ in a_ab]
        p_bf = [_dot(p_, p_).astype(BF16) for p_ in p_bf]
        span = 2
        while span < c:
            if 2 * span < c:
                both = each(lambda p_, t_: _dot(p_, jnp.concatenate([t_.astype(BF16), p_], axis=1)),
                            p_bf, tinv)
                tinv = each(lambda t_, b_: t_ + b_[:, :n], tinv, both)
                p_bf = [b_[:, n:].astype(BF16) for b_ in both]
            else:
                tinv = each(lambda t_, p_: t_ + _dot(p_, t_.astype(BF16)), tinv, p_bf)
            span *= 2
        wu = each(lambda t_, al_, akv_: _dot(t_.astype(BF16),
                                             jnp.concatenate([al_, akv_[:n].astype(BF16)], axis=1)),
                  tinv, al, akv)
        for i, j in enumerate(group):
            lhs_s[j] = jnp.concatenate([wu[i][:, :LANES].astype(BF16), rc[i]], axis=0)
            ufree_s[j] = wu[i][:, LANES:]
            prb_s[j] = p_rb[i]
            akvo_s[j] = akv[i][n:]
            bkt_s[j] = bk[i].astype(F32).T.astype(BF16)
            v_s[j] = vs[i]
            e_last = e_in[(j + 1) * c - 1:(j + 1) * c, :]
            e_s[j] = jnp.broadcast_to(e_last, (LANES, LANES)).T

    for j0 in range(0, nchunk, RW_PREP_GROUP):
        prepare(range(j0, min(j0 + RW_PREP_GROUP, nchunk)))

    def advance(j, st):
        rows = pl.ds(pl.multiple_of(j * c, c), c)
        ws = _dot(lhs_s[j], st.astype(BF16))
        u = ws[:n] + ufree_s[j]
        u_bf = u.astype(BF16)
        o_bd = ws[n:] + _dot(prb_s[j], u_bf) + akvo_s[j]
        o_s[rows, :] = o_bd[:c] + o_bd[c:]
        return e_s[j] * (st + _dot(bkt_s[j], jnp.concatenate([u_bf, v_s[j]], axis=0)))

    st = lax.fori_loop(0, nchunk, advance, st_ref[...])
    st_ref[...] = st
    sout_ref[...] = st

    o = o_s[...]
    mu_o = _head_sum(o, ones_bd) * (1.0 / HEAD_DIM)
    oc = o - mu_o
    var_o = _head_sum(oc * oc, ones_bd) * (1.0 / HEAD_DIM)
    on = oc * lax.rsqrt(var_o + GN_EPS) * gng_ref[...] + gnb_ref[...]
    o_ref[...] = (on + bonus) * g_ref[...]


def _rw_scan(rkv, lw, a, g, row0, batch, t, tb, c, params, wkv0, prev_out):
    m = rkv.shape[0]
    nt = t // tb
    rb0 = row0 // tb
    n = 2 * c
    nchunk = tb // c

    def col(off):
        return lambda b, p, i: (rb0 + b * nt + i, off + p)

    s_t = jnp.swapaxes(wkv0, -1, -2).reshape(batch, HEAD_PAIRS, 2, HEAD_DIM, HEAD_DIM)
    st0 = jnp.einsum('bphkv,hg->bphkgv', s_t, jnp.eye(2, dtype=F32)).reshape(batch, HEAD_PAIRS, LANES, LANES)
    pspec = pl.BlockSpec((1, LANES), lambda b, p, i: (0, p))
    sspec = pl.BlockSpec((None, None, LANES, LANES), lambda b, p, i: (b, p, 0, 0))
    in_specs = [pl.BlockSpec((tb, LANES), col(0)), pl.BlockSpec((tb, LANES), col(HEAD_PAIRS)),
                pl.BlockSpec((tb, LANES), col(2 * HEAD_PAIRS)),
                pl.BlockSpec((tb, LANES), col(0)), pl.BlockSpec((tb, LANES), col(0)),
                pl.BlockSpec((tb, LANES), col(0))] + [pspec] * 5 + [sspec]
    args = [rkv, rkv, rkv, lw, a, g] + list(params) + [st0]
    aliases = {}
    kern = functools.partial(_rw_scan_kernel, tb=tb, c=c)
    if prev_out is not None:
        n_real = len(args)
        in_specs.append(pl.BlockSpec(memory_space=pl.ANY))
        args.append(prev_out)
        aliases = {n_real: 0}
        kern = _drop_inputs(kern, n_real, 1)
    scratch = [pltpu.VMEM((LANES, LANES), F32), pltpu.VMEM((tb, LANES), F32)]
    scratch += [pltpu.VMEM((nchunk, 2 * n, LANES), BF16), pltpu.VMEM((nchunk, n, LANES), F32),
                pltpu.VMEM((nchunk, n, n), BF16), pltpu.VMEM((nchunk, n, LANES), F32),
                pltpu.VMEM((nchunk, LANES, 2 * n), BF16), pltpu.VMEM((nchunk, n, LANES), BF16),
                pltpu.VMEM((nchunk, LANES, LANES), F32)]
    o_c, st = pl.pallas_call(
        kern,
        grid=(batch, HEAD_PAIRS, nt),
        in_specs=in_specs,
        out_specs=[pl.BlockSpec((tb, LANES), col(0)), sspec],
        out_shape=[jax.ShapeDtypeStruct((m, RW_WIDTH), F32),
                   jax.ShapeDtypeStruct((batch, HEAD_PAIRS, LANES, LANES), F32)],
        scratch_shapes=scratch,
        input_output_aliases=aliases,
        compiler_params=_cparams(3),
        name="rwkv_scan",
    )(*args)
    st = st.reshape(batch, HEAD_PAIRS, 2, HEAD_DIM, 2, HEAD_DIM)
    wkv = jnp.stack([st[:, :, 0, :, 0, :], st[:, :, 1, :, 1, :]], axis=2)
    return o_c, jnp.swapaxes(wkv, -1, -2).reshape(batch, RW_HEADS, HEAD_DIM, HEAD_DIM)


def _outproj_kernel(oa_ref, ob_ref, oc_ref, x_ref, w_ref, g_ref, b_ref, y_ref):
    mix = _dot(oa_ref[...].astype(BF16), w_ref[0:SB_WIDTH, :])
    mix += _dot(ob_ref[...].astype(BF16), w_ref[SB_WIDTH:SB_WIDTH + POOL_WIDTH, :])
    mix += _dot(oc_ref[...].astype(BF16), w_ref[SB_WIDTH + POOL_WIDTH:, :])
    y_ref[...] = _layer_norm(DN_ALPHA * x_ref[...] + mix, g_ref[...], b_ref[...])


def _outproj_ln(oa, ob, oc, x, w_bf, g, b, tm):
    m, d = x.shape
    row = lambda i: (i, 0)
    const = lambda i: (0, 0)
    return pl.pallas_call(
        _outproj_kernel,
        grid=(m // tm,),
        in_specs=[pl.BlockSpec((tm, SB_WIDTH), row), pl.BlockSpec((tm, POOL_WIDTH), row),
                  pl.BlockSpec((tm, RW_WIDTH), row), pl.BlockSpec((tm, d), row),
                  pl.BlockSpec(w_bf.shape, const), pl.BlockSpec((1, d), const),
                  pl.BlockSpec((1, d), const)],
        out_specs=pl.BlockSpec((tm, d), row),
        out_shape=jax.ShapeDtypeStruct((m, d), F32),
        compiler_params=_cparams(1),
        name="outproj_ln",
    )(oa, ob, oc, x, w_bf, g, b)


def _router_kernel(x_ref, w_ref, b_ref, idx_ref, gate_ref, rank_ref, cnt_ref, run_ref, *, tm):
    @pl.when(pl.program_id(0) == 0)
    def _():
        run_ref[...] = jnp.zeros_like(run_ref)

    logits = _dot(x_ref[...].astype(BF16), w_ref[...].astype(BF16)) + b_ref[...]
    lane = lax.broadcasted_iota(I32, (tm, N_EXPERTS), 1).astype(F32)
    work = logits
    vals, idxs = [], []
    for _ in range(TOP_K):
        top = jnp.max(work, axis=1, keepdims=True)
        idx = jnp.min(jnp.where(work == top, lane, float(N_EXPERTS)), axis=1, keepdims=True)
        vals.append(top)
        idxs.append(idx)
        work = jnp.where(lane == idx, -jnp.inf, work)
    exps = [jnp.exp(v - vals[0]) for v in vals]
    total = exps[0] + exps[1] + exps[2] + exps[3]
    out_lane = lax.broadcasted_iota(I32, (tm, LANES), 1)
    lane_f = out_lane.astype(F32)
    onehots = [(lane_f == idxs[k]).astype(F32) for k in range(TOP_K)]
    chosen = onehots[0] + onehots[1] + onehots[2] + onehots[3]
    earlier = (lax.broadcasted_iota(I32, (tm, tm), 1) < lax.broadcasted_iota(I32, (tm, tm), 0)).astype(BF16)
    base = run_ref[...] + _dot(earlier, chosen.astype(BF16))
    run_ref[...] = run_ref[...] + jnp.sum(chosen, axis=0, keepdims=True)
    cnt_ref[...] = run_ref[...].astype(I32)
    idx_out = jnp.zeros((tm, LANES), F32)
    gate_out = jnp.zeros((tm, LANES), F32)
    rank_out = jnp.zeros((tm, LANES), F32)
    for k in range(TOP_K):
        idx_out = jnp.where(out_lane == k, idxs[k], idx_out)
        gate_out = jnp.where(out_lane == k, exps[k] / total, gate_out)
        rank_out = jnp.where(out_lane == k, jnp.sum(onehots[k] * base, axis=1, keepdims=True), rank_out)
    idx_ref[...] = idx_out.astype(I32)
    gate_ref[...] = gate_out
    rank_ref[...] = rank_out.astype(I32)


def _router(x, w, b, tm):
    m, d = x.shape
    return pl.pallas_call(
        functools.partial(_router_kernel, tm=tm),
        grid=(m // tm,),
        in_specs=[pl.BlockSpec((tm, d), lambda i: (i, 0)),
                  pl.BlockSpec((d, N_EXPERTS), lambda i: (0, 0)),
                  pl.BlockSpec((1, N_EXPERTS), lambda i: (0, 0))],
        out_specs=[pl.BlockSpec((tm, LANES), lambda i: (i, 0))] * 3 + [pl.BlockSpec((1, LANES), lambda i: (0, 0))],
        out_shape=[jax.ShapeDtypeStruct((m, LANES), I32), jax.ShapeDtypeStruct((m, LANES), F32),
                   jax.ShapeDtypeStruct((m, LANES), I32), jax.ShapeDtypeStruct((1, LANES), I32)],
        scratch_shapes=[pltpu.VMEM((1, LANES), F32)],
        compiler_params=_cparams(1),
        name="router",
    )(x, w, b)


def _row_copy(src_hbm, dst_ref, sem, src_row, dst_row):
    return pltpu.make_async_copy(src_hbm.at[pl.ds(src_row, 1), :], dst_ref.at[pl.ds(dst_row, 1), :], sem)


GATHER_UNROLL = 8


def _gather_kernel(tok_ref, nu_ref, x_hbm, o_ref, rows_ref, sem, *, bm):
    @pl.when(pl.program_id(0) < nu_ref[0])
    def _():
        def start(i, carry):
            for u in range(GATHER_UNROLL):
                r = i * GATHER_UNROLL + u
                _row_copy(x_hbm, rows_ref, sem, tok_ref[0, 0, r], r).start()
            return carry

        lax.fori_loop(0, bm // GATHER_UNROLL, start, 0)

        def wait(i, carry):
            for u in range(GATHER_UNROLL):
                _row_copy(x_hbm, rows_ref, sem, 0, i * GATHER_UNROLL + u).wait()
            return carry

        lax.fori_loop(0, bm // GATHER_UNROLL, wait, 0)
        o_ref[...] = rows_ref[...].astype(BF16)


def _gather_rows(x, buf_tok, n_used, bm):
    d = x.shape[1]
    nb = buf_tok.shape[0] // bm
    return pl.pallas_call(
        functools.partial(_gather_kernel, bm=bm),
        grid=(nb,),
        in_specs=[pl.BlockSpec((1, 1, bm), lambda i: (i, 0, 0), memory_space=pltpu.SMEM),
                  pl.BlockSpec(memory_space=pltpu.SMEM),
                  pl.BlockSpec(memory_space=pl.ANY)],
        out_specs=pl.BlockSpec((bm, d), lambda i: (i, 0)),
        out_shape=jax.ShapeDtypeStruct((nb * bm, d), BF16),
        scratch_shapes=[pltpu.VMEM((bm, d), F32), pltpu.SemaphoreType.DMA(())],
        compiler_params=_cparams(1),
        name="moe_gather",
    )(buf_tok.reshape(nb, 1, bm), n_used, x)


def _expert_up_kernel(be_ref, nu_ref, x_ref, wg_ref, wl_ref, bg_ref, bl_ref, act_ref):
    @pl.when(pl.program_id(1) < nu_ref[0])
    def _():
        x = x_ref[...]
        h_glu = _dot(x, wg_ref[...].astype(BF16)) + bg_ref[...]
        h_lin = _dot(x, wl_ref[...].astype(BF16)) + bl_ref[...]
        h_glu = jnp.minimum(h_glu, SWIGLU_LIMIT)
        h_lin = jnp.clip(h_lin, -SWIGLU_LIMIT, SWIGLU_LIMIT)
        act = h_glu * jax.nn.sigmoid(SWIGLU_ALPHA * h_glu) * (h_lin + 1.0)
        act_ref[...] = act.astype(BF16)


def _expert_up(xb, w1, b1, layer, block_e, n_used, bm, tn):
    rows, d = xb.shape
    de = w1.shape[3] // 2
    nj = de // tn
    nb = rows // bm

    def blk(i, nu):
        return jnp.minimum(i, nu[0] - 1)

    grid_spec = pltpu.PrefetchScalarGridSpec(
        num_scalar_prefetch=2,
        grid=(nj, nb),
        in_specs=[pl.BlockSpec((bm, d), lambda j, i, be, nu: (blk(i, nu), 0)),
                  pl.BlockSpec((None, None, d, tn), lambda j, i, be, nu: (layer, be[blk(i, nu)], 0, j)),
                  pl.BlockSpec((None, None, d, tn),
                               lambda j, i, be, nu: (layer, be[blk(i, nu)], 0, nj + j)),
                  pl.BlockSpec((None, None, 1, tn), lambda j, i, be, nu: (layer, be[blk(i, nu)], 0, j)),
                  pl.BlockSpec((None, None, 1, tn),
                               lambda j, i, be, nu: (layer, be[blk(i, nu)], 0, nj + j))],
        out_specs=pl.BlockSpec((bm, tn), lambda j, i, be, nu: (blk(i, nu), j)),
    )
    return pl.pallas_call(
        _expert_up_kernel,
        grid_spec=grid_spec,
        out_shape=jax.ShapeDtypeStruct((rows, de), BF16),
        compiler_params=_cparams(2),
        name="moe_up",
    )(block_e, n_used, xb, w1, w1, b1, b1)


def _expert_down_kernel(be_ref, nu_ref, a_ref, w_ref, b_ref, y_ref):
    @pl.when(pl.program_id(1) < nu_ref[0])
    def _():
        y_ref[...] = _dot(a_ref[...], w_ref[...].astype(BF16)) + b_ref[...]


def _expert_down(act, w2, b2, layer, block_e, n_used, bm, tn):
    rows, de = act.shape
    d = w2.shape[3]
    nj = d // tn
    nb = rows // bm

    def blk(i, nu):
        return jnp.minimum(i, nu[0] - 1)

    grid_spec = pltpu.PrefetchScalarGridSpec(
        num_scalar_prefetch=2,
        grid=(nj, nb),
        in_specs=[pl.BlockSpec((bm, de), lambda j, i, be, nu: (blk(i, nu), 0)),
                  pl.BlockSpec((None, None, de, tn), lambda j, i, be, nu: (layer, be[blk(i, nu)], 0, j)),
                  pl.BlockSpec((None, None, 1, tn), lambda j, i, be, nu: (layer, be[blk(i, nu)], 0, j))],
        out_specs=pl.BlockSpec((bm, tn), lambda j, i, be, nu: (blk(i, nu), j)),
    )
    return pl.pallas_call(
        _expert_down_kernel,
        grid_spec=grid_spec,
        out_shape=jax.ShapeDtypeStruct((rows, d), F32),
        compiler_params=_cparams(2),
        name="moe_down",
    )(block_e, n_used, act, w2, b2)


def _combine_kernel(dest_ref, y_hbm, gate_ref, x_ref, g_ref, b_ref, o_ref, rows_ref, sem, *, tm):
    def start(r, carry):
        for k in range(TOP_K):
            _row_copy(y_hbm, rows_ref.at[k], sem, dest_ref[0, 0, r * TOP_K + k], r).start()
        return carry

    lax.fori_loop(0, tm, start, 0)

    def wait(r, carry):
        for k in range(TOP_K):
            _row_copy(y_hbm, rows_ref.at[k], sem, 0, r).wait()
        return carry

    lax.fori_loop(0, tm, wait, 0)
    gates = gate_ref[...]
    ffn = gates[:, 0:1] * rows_ref[0]
    for k in range(1, TOP_K):
        ffn += gates[:, k:k + 1] * rows_ref[k]
    o_ref[...] = _layer_norm(DN_ALPHA * x_ref[...] + ffn, g_ref[...], b_ref[...])


def _combine_ln(y, dest, gates, x, g, b, tm):
    m, d = x.shape
    nb = m // tm
    return pl.pallas_call(
        functools.partial(_combine_kernel, tm=tm),
        grid=(nb,),
        in_specs=[pl.BlockSpec((1, 1, tm * TOP_K), lambda i: (i, 0, 0), memory_space=pltpu.SMEM),
                  pl.BlockSpec(memory_space=pl.ANY),
                  pl.BlockSpec((tm, LANES), lambda i: (i, 0)),
                  pl.BlockSpec((tm, d), lambda i: (i, 0)),
                  pl.BlockSpec((1, d), lambda i: (0, 0)),
                  pl.BlockSpec((1, d), lambda i: (0, 0))],
        out_specs=pl.BlockSpec((tm, d), lambda i: (i, 0)),
        out_shape=jax.ShapeDtypeStruct((m, d), F32),
        scratch_shapes=[pltpu.VMEM((TOP_K, tm, d), F32), pltpu.SemaphoreType.DMA(())],
        compiler_params=_cparams(1),
        name="moe_combine_ln",
    )(dest.reshape(nb, 1, tm * TOP_K), y, gates, x, g, b)


def _routing_tables(top_i, rank, counts, bm):
    n = top_i.shape[0]
    nk = n * TOP_K
    padded = (counts + bm - 1) // bm * bm
    pad_end = jnp.cumsum(padded)
    pad_start = pad_end - padded
    dest = (pad_start[top_i] + rank).astype(I32).reshape(nk)
    nb = (nk + N_EXPERTS * (bm - 1)) // bm
    buf_tok = jnp.zeros((nb * bm,), I32).at[dest].set(jnp.arange(nk, dtype=I32) // TOP_K)
    block_e = jnp.minimum(jnp.searchsorted(pad_end, jnp.arange(nb, dtype=I32) * bm, side='right'),
                          N_EXPERTS - 1).astype(I32)
    n_used = (pad_end[-1:] // bm).astype(I32)
    return dest, buf_tok, block_e, n_used


def _moe_ln(x, lp, big, layer):
    idx, gates, rank, counts = _router(x, lp['w_router'], lp['b_router'].reshape(1, N_EXPERTS), tm=256)
    dest, buf_tok, block_e, n_used = _routing_tables(idx[:, :TOP_K], rank[:, :TOP_K],
                                                     counts[0, :N_EXPERTS], MOE_BM)
    xb = _gather_rows(x, buf_tok, n_used, MOE_BM)
    nl, e, d, de2 = big['w_e1'].shape
    act = _expert_up(xb, big['w_e1'], big['b_e1'].reshape(nl, e, 1, de2), layer, block_e, n_used,
                     MOE_BM, tn=1024)
    y = _expert_down(act, big['w_e2'], big['b_e2'].reshape(nl, e, 1, d), layer, block_e, n_used,
                     MOE_BM, tn=1024)
    return _combine_ln(y, dest, gates, x, lp['ln_ffn_g'].reshape(1, d), lp['ln_ffn_b'].reshape(1, d),
                       tm=128)


def _layer(x, lp, big, layer, st, n_p, bp, tp, bs, ts):
    d = x.shape[1]
    h = _inproj(x, big['w_in'], layer, tm=768, tn=768)
    qkv_w = 3 * SB_WIDTH
    u = h[:, qkv_w:qkv_w + POOL_WIDTH]
    c = h[:, qkv_w + POOL_WIDTH:]

    o_a = _sb_attention(h, 0, bp, tp, 256, None, None)
    hist = (st['cache_k'].reshape(bs, -1, SB_WIDTH), st['cache_v'].reshape(bs, -1, SB_WIDTH))
    o_a = _sb_attention(h, n_p, bs, ts, ts, hist, o_a)
    past = hist[0].shape[1]

    w_pool = lp['pool_w']
    scale = lp['pool_scale'].reshape(1, POOL_WIDTH)
    o_b = _pool_mixer(u, 0, bp, tp, 512, jnp.zeros((bp, 16, POOL_WIDTH), F32), w_pool, scale, 0, None)
    hist16 = jnp.pad(st['state_pool'], ((0, 0), (1, 0), (0, 0)))
    o_b = _pool_mixer(u, n_p, bs, ts, ts, hist16, w_pool, scale, past, o_b)

    zero_lora = jnp.zeros((RW_LORA // 2, RW_WIDTH), F32)
    w2p = jnp.concatenate([lp['rw_w2'], zero_lora], axis=0)
    a2p = jnp.concatenate([zero_lora, lp['rw_a2']], axis=0)
    pre_args = (lp['rw_mu'].reshape(1, RW_COLS), lp['rw_w0'].reshape(1, RW_WIDTH), w2p,
                lp['rw_a0'].reshape(1, RW_WIDTH), a2p, lp['rw_g2'])
    pre = _rw_pre(c, 0, bp, tp, 256, jnp.zeros((bp, 1, RW_COLS), F32), *pre_args, None)
    rkv, lw, a, g = _rw_pre(c, n_p, bs, ts, ts, st['state_shift'], *pre_args, pre)
    prm = tuple(lp[k].reshape(1, RW_WIDTH) for k in ('rw_kk', 'rw_ka', 'rw_rk', 'rw_gn_g', 'rw_gn_b'))
    o_c, wkv_p = _rw_scan(rkv, lw, a, g, 0, bp, tp, 512, 64, prm,
                          jnp.zeros((bp, RW_HEADS, HEAD_DIM, HEAD_DIM), F32), None)
    o_c, wkv_s = _rw_scan(rkv, lw, a, g, n_p, bs, ts, ts, ts, prm, st['state_wkv'], o_c)

    x1 = _outproj_ln(o_a, o_b, o_c, x, lp['w_out'].astype(BF16), lp['ln_mix_g'].reshape(1, d),
                     lp['ln_mix_b'].reshape(1, d), tm=256)
    x2 = _moe_ln(x1, lp, big, layer)

    def states(rows, b, t, wkv):
        hb = rows.reshape(b, t, -1)
        k_new = hb[:, :, SB_WIDTH:2 * SB_WIDTH].reshape(b, t, SB_HEADS, HEAD_DIM)
        v_new = hb[:, :, 2 * SB_WIDTH:qkv_w].reshape(b, t, SB_HEADS, HEAD_DIM)
        pool_new = hb[:, t - POOL_HIST:, qkv_w:qkv_w + POOL_WIDTH]
        shift_new = hb[:, t - 1:, qkv_w + POOL_WIDTH:]
        return k_new, v_new, pool_new, shift_new, wkv

    return x2, states(h[:n_p], bp, tp, wkv_p), states(h[n_p:], bs, ts, wkv_s)


def kernel(x_prompt, x_sample, cache_k, cache_v, state_pool, state_shift, state_wkv, w_in, w_out, ln_mix_g, ln_mix_b, ln_ffn_g, ln_ffn_b, pool_w, pool_scale, rw_mu, rw_w0, rw_w2, rw_a0, rw_a2, rw_g2, rw_kk, rw_ka, rw_rk, rw_gn_g, rw_gn_b, w_router, b_router, w_e1, b_e1, w_e2, b_e2):
    bp, tp, d = x_prompt.shape
    bs, ts, _ = x_sample.shape
    n_p = bp * tp
    x = jnp.concatenate([x_prompt.reshape(n_p, d), x_sample.reshape(bs * ts, d)], axis=0)
    big = dict(w_in=w_in, w_e1=w_e1, b_e1=b_e1, w_e2=w_e2, b_e2=b_e2)
    weights = dict(w_out=w_out, ln_mix_g=ln_mix_g, ln_mix_b=ln_mix_b, ln_ffn_g=ln_ffn_g,
                   ln_ffn_b=ln_ffn_b, pool_w=pool_w, pool_scale=pool_scale, rw_mu=rw_mu, rw_w0=rw_w0,
                   rw_w2=rw_w2, rw_a0=rw_a0, rw_a2=rw_a2, rw_g2=rw_g2, rw_kk=rw_kk, rw_ka=rw_ka,
                   rw_rk=rw_rk, rw_gn_g=rw_gn_g, rw_gn_b=rw_gn_b, w_router=w_router,
                   b_router=b_router)
    st_p, st_s = [], []
    for l in range(w_in.shape[0]):
        lp = {k: v[l] for k, v in weights.items()}
        st = dict(cache_k=cache_k[l], cache_v=cache_v[l], state_pool=state_pool[l],
                  state_shift=state_shift[l], state_wkv=state_wkv[l])
        x, sp, ss = _layer(x, lp, big, l, st, n_p, bp, tp, bs, ts)
        st_p.append(sp)
        st_s.append(ss)
    y_prompt = x[:n_p].reshape(bp, tp, d)
    y_sample = x[n_p:].reshape(bs, ts, d)
    k_p, v_p, pool_p, shift_p, wkv_p = (jnp.stack([s[i] for s in st_p]) for i in range(5))
    k_s, v_s, pool_s, shift_s, wkv_s = (jnp.stack([s[i] for s in st_s]) for i in range(5))
    return (y_prompt, y_sample, k_p, v_p, pool_p, shift_p, wkv_p, k_s, v_s, pool_s, shift_s, wkv_s)
```

```python
import functools

import jax
import jax.numpy as jnp
from jax import lax
from jax.experimental import pallas as pl
from jax.experimental.pallas import tpu as pltpu

F32 = jnp.float32
BF16 = jnp.bfloat16
I32 = jnp.int32

HEAD_DIM = 64
SB_HEADS = 12
SB_WIDTH = SB_HEADS * HEAD_DIM
POOL_WINDOWS = (2, 4, 8, 16)
POOL_GC = 128
POOL_WIDTH = POOL_GC * len(POOL_WINDOWS)
POOL_HIST = max(POOL_WINDOWS) - 1
RW_HEADS = 12
RW_WIDTH = RW_HEADS * HEAD_DIM
RW_LORA = 128
RW_GATE = 128
RW_COLS = 3 * RW_WIDTH + RW_LORA + RW_GATE
N_EXPERTS = 32
TOP_K = 4
SWIGLU_LIMIT = 7.0
SWIGLU_ALPHA = 1.702
DEPTH = 2
DN_ALPHA = (2.0 * DEPTH) ** 0.25
LN_EPS = 1e-5
GN_EPS = 64e-5

LANES = 128
HEAD_PAIRS = SB_WIDTH // LANES
VMEM_LIMIT = 56 * 1024 * 1024
ATT_LOG_FLOOR = -110.0

MOE_BM = 256
RW_PREP_GROUP = 8


def _cparams(n_axes):
    return pltpu.CompilerParams(dimension_semantics=("arbitrary",) * n_axes,
                                vmem_limit_bytes=VMEM_LIMIT)


def _dot(a, b, precision=None):
    return jnp.dot(a, b, preferred_element_type=F32, precision=precision)


def _dot_nt(a, b, precision=None):
    return lax.dot_general(a, b, (((1,), (1,)), ((), ())), preferred_element_type=F32,
                           precision=precision)


def _dot_tn(a, b, precision=None):
    return lax.dot_general(a, b, (((0,), (0,)), ((), ())), preferred_element_type=F32,
                           precision=precision)


def _layer_norm(x, g, b):
    mu = jnp.mean(x, axis=-1, keepdims=True)
    xc = x - mu
    var = jnp.mean(xc * xc, axis=-1, keepdims=True)
    return xc * lax.rsqrt(var + LN_EPS) * g + b


def _inproj_kernel(x_ref, w_ref, o_ref, wbf_ref):
    @pl.when(pl.program_id(1) == 0)
    def _():
        wbf_ref[...] = w_ref[...].astype(BF16)

    o_ref[...] = _dot(x_ref[...].astype(BF16), wbf_ref[...])


def _inproj(x, w, layer, tm, tn):
    m, k = x.shape
    n = w.shape[2]
    return pl.pallas_call(
        _inproj_kernel,
        grid=(n // tn, m // tm),
        in_specs=[pl.BlockSpec((tm, k), lambda j, i: (i, 0)),
                  pl.BlockSpec((None, k, tn), lambda j, i: (layer, 0, j))],
        out_specs=pl.BlockSpec((tm, tn), lambda j, i: (i, j)),
        out_shape=jax.ShapeDtypeStruct((m, n), F32),
        scratch_shapes=[pltpu.VMEM((k, tn), BF16)],
        compiler_params=_cparams(2),
        name="inproj",
    )(x, w)


def _split2(a):
    hi = a.astype(BF16)
    lo = (a - hi.astype(F32)).astype(BF16)
    return hi, lo


def _suffix_matrix(n):
    return (lax.broadcasted_iota(I32, (n, n), 0) > lax.broadcasted_iota(I32, (n, n), 1)).astype(BF16)


def _sb_block(qhs, k_bf, v_bf, suffix, carries, mask):
    heads = range(2)
    z = [_dot_nt(qhs[h], k_bf) for h in heads]
    sp = [jnp.maximum(z_, 0.0) + jnp.log1p(jnp.exp(-jnp.abs(z_))) for z_ in z]
    log_stay = [-s_ if mask is None else jnp.where(mask, -s_, 0.0) for s_ in sp]
    parts = [_split2(l_) for l_ in log_stay]
    hi = [_dot(parts[h][0], suffix) for h in heads]
    lo = [_dot(parts[h][1], suffix) for h in heads]
    after = [hi[h] + lo[h] + carries[h] for h in heads]
    att = [jnp.exp(z[h] - sp[h] + after[h]) for h in heads]
    if mask is not None:
        att = [jnp.where(mask, a_, 0.0) for a_ in att]
    pv = [_dot(att[h].astype(BF16), v_bf) for h in heads]
    return pv, [after[h][:, 0:1] + log_stay[h][:, 0:1] for h in heads]


def _sb_live(carries):
    return (jnp.max(jnp.maximum(carries[0], carries[1])) > ATT_LOG_FLOOR).astype(I32)


def _sb_sweep(qhs, k_ref, v_ref, tk, n_blocks, suffix, live, accs, carries):
    def cond(s):
        return jnp.logical_and(s[0] < n_blocks, s[1] > 0)

    def body(s):
        step, _, accs, carries = s
        rows = pl.ds(pl.multiple_of((n_blocks - 1 - step) * tk, tk), tk)
        k_bf = k_ref[rows, :].astype(BF16)
        v_bf = v_ref[rows, :].astype(BF16)
        pv, carries = _sb_block(qhs, k_bf, v_bf, suffix, carries, None)
        accs = tuple(accs[h] + pv[h] for h in range(2))
        return step + 1, _sb_live(carries), accs, tuple(carries)

    _, live, accs, carries = lax.while_loop(cond, body, (jnp.int32(0), live, accs, carries))
    return live, accs, carries


def _sb_kernel(*refs, tq, th, past):
    if past:
        q_ref, kn_ref, vn_ref, kh_ref, vh_ref, o_ref, kbf_ref, vbf_ref = refs
    else:
        q_ref, kn_ref, vn_ref, o_ref, kbf_ref, vbf_ref = refs
    qi = pl.program_id(2)

    @pl.when(qi == 0)
    def _():
        kbf_ref[...] = kn_ref[...].astype(BF16)
        vbf_ref[...] = vn_ref[...].astype(BF16)

    q = q_ref[...] * (HEAD_DIM ** -0.5)
    lane = lax.broadcasted_iota(I32, (tq, LANES), 1)
    qhs = [jnp.where(lane < HEAD_DIM, q, 0.0).astype(BF16),
           jnp.where(lane >= HEAD_DIM, q, 0.0).astype(BF16)]
    suffix_new = _suffix_matrix(tq)
    diag_mask = lax.broadcasted_iota(I32, (tq, tq), 1) < lax.broadcasted_iota(I32, (tq, tq), 0)
    rows = pl.ds(pl.multiple_of(qi * tq, tq), tq)
    k_bf = kbf_ref[rows, :]
    v_bf = vbf_ref[rows, :]
    accs, carries = _sb_block(qhs, k_bf, v_bf, suffix_new, [jnp.zeros((tq, 1), F32)] * 2, diag_mask)
    accs, carries = tuple(accs), tuple(carries)
    live, accs, carries = _sb_sweep(qhs, kbf_ref, vbf_ref, tq, qi, suffix_new, _sb_live(carries),
                                    accs, carries)
    if past:
        _, accs, carries = _sb_sweep(qhs, kh_ref, vh_ref, th, past // th, _suffix_matrix(th), live,
                                     accs, carries)
    o_ref[...] = jnp.where(lane < HEAD_DIM, accs[0], accs[1])


def _sb_attention(h_all, row0, batch, t, tq, hist, prev_out):
    m = h_all.shape[0]
    nq = t // tq
    rb0 = row0 // tq
    sb0 = row0 // t
    in_specs = [pl.BlockSpec((tq, LANES), lambda b, p, i: (rb0 + b * nq + i, p)),
                pl.BlockSpec((t, LANES), lambda b, p, i: (sb0 + b, HEAD_PAIRS + p)),
                pl.BlockSpec((t, LANES), lambda b, p, i: (sb0 + b, 2 * HEAD_PAIRS + p))]
    args = [h_all, h_all, h_all]
    past, th = 0, 0
    if hist is not None:
        past = hist[0].shape[1]
        th = min(past, 512)
        in_specs += [pl.BlockSpec((None, past, LANES), lambda b, p, i: (b, 0, p))] * 2
        args += list(hist)
    aliases = {}
    if prev_out is not None:
        in_specs.append(pl.BlockSpec(memory_space=pl.ANY))
        args.append(prev_out)
        aliases = {len(args) - 1: 0}
    kern = functools.partial(_sb_kernel, tq=tq, th=th, past=past)
    if prev_out is not None:
        kern = _drop_last_input(kern, n_in=len(args))
    return pl.pallas_call(
        kern,
        grid=(batch, HEAD_PAIRS, nq),
        in_specs=in_specs,
        out_specs=pl.BlockSpec((tq, LANES), lambda b, p, i: (rb0 + b * nq + i, p)),
        out_shape=jax.ShapeDtypeStruct((m, SB_WIDTH), F32),
        scratch_shapes=[pltpu.VMEM((t, LANES), BF16)] * 2,
        input_output_aliases=aliases,
        compiler_params=_cparams(3),
        name="sb_attention",
    )(*args)


def _drop_last_input(kern, n_in):
    def wrapped(*refs):
        return kern(*refs[:n_in - 1], *refs[n_in:])
    return wrapped


def _pool_kernel(u_ref, hist_ref, w_ref, scale_ref, o_ref, ext_ref, *, tt, pos0):
    ti = pl.program_id(1)
    halo = 16

    @pl.when(ti == 0)
    def _():
        ext_ref[0:halo, :] = hist_ref[...]

    @pl.when(ti > 0)
    def _():
        ext_ref[0:halo, :] = ext_ref[tt:tt + halo, :]

    u = u_ref[...]
    ext_ref[halo:halo + tt, :] = u
    pos = pos0 + ti * tt + lax.broadcasted_iota(I32, (tt, 1), 0)
    for g, w in enumerate(POOL_WINDOWS):
        cols = slice(g * POOL_GC, (g + 1) * POOL_GC)
        s = u[:, cols]
        for k in range(1, w):
            s = s + ext_ref[halo - k:halo - k + tt, cols]
        cnt = jnp.minimum(pos + 1, w).astype(F32)
        pooled = s / cnt - u[:, cols]
        y = _dot(pooled.astype(BF16), w_ref[g].astype(BF16))
        o_ref[:, cols] = y * scale_ref[:, cols]


def _pool_mixer(u_all, row0, batch, t, tt, hist16, w_pool, scale, pos0, prev_out):
    m = u_all.shape[0]
    nt = t // tt
    rb0 = row0 // tt
    in_specs = [pl.BlockSpec((tt, POOL_WIDTH), lambda b, i: (rb0 + b * nt + i, 0)),
                pl.BlockSpec((None, 16, POOL_WIDTH), lambda b, i: (b, 0, 0)),
                pl.BlockSpec((len(POOL_WINDOWS), POOL_GC, POOL_GC), lambda b, i: (0, 0, 0)),
                pl.BlockSpec((1, POOL_WIDTH), lambda b, i: (0, 0))]
    args = [u_all, hist16, w_pool, scale]
    aliases = {}
    kern = functools.partial(_pool_kernel, tt=tt, pos0=pos0)
    if prev_out is not None:
        in_specs.append(pl.BlockSpec(memory_space=pl.ANY))
        args.append(prev_out)
        aliases = {len(args) - 1: 0}
        kern = _drop_last_input(kern, n_in=len(args))
    return pl.pallas_call(
        kern,
        grid=(batch, nt),
        in_specs=in_specs,
        out_specs=pl.BlockSpec((tt, POOL_WIDTH), lambda b, i: (rb0 + b * nt + i, 0)),
        out_shape=jax.ShapeDtypeStruct((m, POOL_WIDTH), F32),
        scratch_shapes=[pltpu.VMEM((tt + 16, POOL_WIDTH), F32)],
        input_output_aliases=aliases,
        compiler_params=_cparams(2),
        name="pool_mixer",
    )(*args)


def _rw_pre_kernel(c_ref, hist_ref, mu_ref, w0_ref, w2_ref, a0_ref, a2_ref, g2_ref,
                   rkv_ref, lw_ref, a_ref, g_ref, last_ref, *, tt):
    ti = pl.program_id(1)

    @pl.when(ti == 0)
    def _():
        last_ref[...] = hist_ref[...]

    c = c_ref[...]
    row = lax.broadcasted_iota(I32, (tt, 1), 0)
    prev = jnp.where(row == 0, last_ref[...], pltpu.roll(c, 1, 0))
    last_ref[...] = c[tt - 1:tt, :]
    xs = c + (prev - c) * mu_ref[...]
    rkv_ref[...] = xs[:, :3 * RW_WIDTH]
    lora = xs[:, 3 * RW_WIDTH:3 * RW_WIDTH + RW_LORA]
    wd = _dot(jnp.tanh(lora).astype(BF16), w2_ref[...].astype(BF16))
    sp = jax.nn.softplus(-(w0_ref[...] + wd))
    lw_ref[...] = -jnp.exp(-sp - 0.5)
    a_ref[...] = jax.nn.sigmoid(a0_ref[...] + _dot(lora.astype(BF16), a2_ref[...].astype(BF16)))
    gd = xs[:, 3 * RW_WIDTH + RW_LORA:]
    g_ref[...] = _dot(jax.nn.sigmoid(gd).astype(BF16), g2_ref[...].astype(BF16))


def _rw_pre(c_all, row0, batch, t, tt, shift_hist, mu, w0, w2p, a0, a2p, g2, prev_outs):
    m = c_all.shape[0]
    nt = t // tt
    rb0 = row0 // tt
    row_map = lambda b, i: (rb0 + b * nt + i, 0)
    const = lambda b, i: (0, 0)
    in_specs = [pl.BlockSpec((tt, RW_COLS), row_map),
                pl.BlockSpec((None, 1, RW_COLS), lambda b, i: (b, 0, 0)),
                pl.BlockSpec((1, RW_COLS), const),
                pl.BlockSpec((1, RW_WIDTH), const),
                pl.BlockSpec((RW_LORA, RW_WIDTH), const),
                pl.BlockSpec((1, RW_WIDTH), const),
                pl.BlockSpec((RW_LORA, RW_WIDTH), const),
                pl.BlockSpec((RW_GATE, RW_WIDTH), const)]
    args = [c_all, shift_hist, mu, w0, w2p, a0, a2p, g2]
    widths = (3 * RW_WIDTH, RW_WIDTH, RW_WIDTH, RW_WIDTH)
    aliases = {}
    kern = functools.partial(_rw_pre_kernel, tt=tt)
    if prev_outs is not None:
        n_real = len(args)
        for j, po in enumerate(prev_outs):
            in_specs.append(pl.BlockSpec(memory_space=pl.ANY))
            args.append(po)
            aliases[n_real + j] = j
        kern = _drop_inputs(kern, n_real, len(prev_outs))
    return pl.pallas_call(
        kern,
        grid=(batch, nt),
        in_specs=in_specs,
        out_specs=[pl.BlockSpec((tt, wd), row_map) for wd in widths],
        out_shape=[jax.ShapeDtypeStruct((m, wd), F32) for wd in widths],
        scratch_shapes=[pltpu.VMEM((1, RW_COLS), F32)],
        input_output_aliases=aliases,
        compiler_params=_cparams(2),
        name="rwkv_pre",
    )(*args)


def _drop_inputs(kern, n_real, n_drop):
    def wrapped(*refs):
        return kern(*refs[:n_real], *refs[n_real + n_drop:])
    return wrapped


def _head_ones():
    r = lax.broadcasted_iota(I32, (LANES, LANES), 0) // HEAD_DIM
    c = lax.broadcasted_iota(I32, (LANES, LANES), 1) // HEAD_DIM
    return (r == c).astype(BF16)


def _head_sum(x, ones_bd):
    hi, lo = _split2(x)
    return _dot(hi, ones_bd) + _dot(lo, ones_bd)


def _stack_heads(x):
    lane = lax.broadcasted_iota(I32, x.shape, 1)
    zero = jnp.zeros_like(x)
    return jnp.concatenate([jnp.where(lane < HEAD_DIM, x, zero), jnp.where(lane >= HEAD_DIM, x, zero)],
                           axis=0)


def _rw_scan_kernel(r_ref, k_ref, v_ref, lw_ref, a_ref, g_ref, kks_ref, ka_ref, rk_ref, gng_ref,
                    gnb_ref, s0_ref, o_ref, sout_ref, st_ref, o_s, lhs_s, ufree_s, prb_s, akvo_s, bkt_s,
                    v_s, e_s, *, tb, c):
    ti = pl.program_id(2)
    n = 2 * c
    nchunk = tb // c

    @pl.when(ti == 0)
    def _():
        st_ref[...] = s0_ref[...]

    ones_bd = _head_ones()
    r = r_ref[...]
    k0 = k_ref[...]
    v = v_ref[...]
    lw = lw_ref[...]
    a = a_ref[...]
    kk = k0 * kks_ref[...]
    kk = kk * lax.rsqrt(jnp.maximum(_head_sum(kk * kk, ones_bd), 1e-24))
    kmod = k0 * (1.0 + (a - 1.0) * ka_ref[...])
    ri = lax.broadcasted_iota(I32, (tb, tb), 0)
    ci = lax.broadcasted_iota(I32, (tb, tb), 1)
    l_chunk = jnp.logical_and(ri >= ci, ri // c == ci // c).astype(BF16)
    lw_hi, lw_lo = _split2(lw)
    gcum = _dot(l_chunk, lw_hi) + _dot(l_chunk, lw_lo)
    e_in = jnp.exp(gcum)
    e_neg = jnp.exp(-gcum)
    al_all = (-kk * jnp.exp(gcum - lw)).astype(BF16)
    be_all = (kk * a * e_neg).astype(BF16)
    kc_all = (kmod * e_neg).astype(BF16)
    rc_all = (r * e_in).astype(BF16)
    bonus = _head_sum(r * kmod * rk_ref[...], ones_bd) * v

    rn = lax.broadcasted_iota(I32, (n, n), 0)
    cn = lax.broadcasted_iota(I32, (n, n), 1)
    same_head = rn // c == cn // c
    strict = jnp.logical_and(same_head, cn < rn).astype(F32)
    incl = jnp.logical_and(same_head, cn <= rn).astype(F32)
    eye = (rn == cn).astype(F32)

    def prepare(group):
        each = lambda f, *lists: [f(*xs) for xs in zip(*lists)]
        rows = [slice(j * c, (j + 1) * c) for j in group]
        al = [_stack_heads(al_all[r_]) for r_ in rows]
        rc = [_stack_heads(rc_all[r_]) for r_ in rows]
        bk = [jnp.concatenate([_stack_heads(be_all[r_]), _stack_heads(kc_all[r_])], axis=0) for r_ in rows]
        vs = [_stack_heads(v[r_]).astype(BF16) for r_ in rows]
        m4 = each(lambda al_, rc_, bk_: _dot_nt(jnp.concatenate([al_, rc_], axis=0), bk_), al, rc, bk)
        a_ab = [m[:n, :n] * strict for m in m4]
        lower = [jnp.concatenate([m[:n, n:] * strict, m[n:, n:] * incl], axis=0).astype(BF16) for m in m4]
        p_rb = [(m[n:, :n] * incl).astype(BF16) for m in m4]
        akv = each(_dot, lower, vs)
        tinv = [eye + a_ for a_ in a_ab]
        p_bf = [a_.astype(BF16) for a_ in a_ab]
        p_bf = [_dot(p_, p_).astype(BF16) for p_ in p_bf]
        span = 2
        while span < c:
            if 2 * span < c:
                both = each(lambda p_, t_: _dot(p_, jnp.concatenate([t_.astype(BF16), p_], axis=1)),
                            p_bf, tinv)
                tinv = each(lambda t_, b_: t_ + b_[:, :n], tinv, both)
                p_bf = [b_[:, n:].astype(BF16) for b_ in both]
            else:
                tinv = each(lambda t_, p_: t_ + _dot(p_, t_.astype(BF16)), tinv, p_bf)
            span *= 2
        wu = each(lambda t_, al_, akv_: _dot(t_.astype(BF16),
                                             jnp.concatenate([al_, akv_[:n].astype(BF16)], axis=1)),
                  tinv, al, akv)
        for i, j in enumerate(group):
            lhs_s[j] = jnp.concatenate([wu[i][:, :LANES].astype(BF16), rc[i]], axis=0)
            ufree_s[j] = wu[i][:, LANES:]
            prb_s[j] = p_rb[i]
            akvo_s[j] = akv[i][n:]
            bkt_s[j] = bk[i].astype(F32).T.astype(BF16)
            v_s[j] = vs[i]
            e_last = e_in[(j + 1) * c - 1:(j + 1) * c, :]
            e_s[j] = jnp.broadcast_to(e_last, (LANES, LANES)).T

    for j0 in range(0, nchunk, RW_PREP_GROUP):
        prepare(range(j0, min(j0 + RW_PREP_GROUP, nchunk)))

    def advance(j, st):
        rows = pl.ds(pl.multiple_of(j * c, c), c)
        ws = _dot(lhs_s[j], st.astype(BF16))
        u = ws[:n] + ufree_s[j]
        u_bf = u.astype(BF16)
        o_bd = ws[n:] + _dot(prb_s[j], u_bf) + akvo_s[j]
        o_s[rows, :] = o_bd[:c] + o_bd[c:]
        return e_s[j] * (st + _dot(bkt_s[j], jnp.concatenate([u_bf, v_s[j]], axis=0)))

    st = lax.fori_loop(0, nchunk, advance, st_ref[...])
    st_ref[...] = st
    sout_ref[...] = st

    o = o_s[...]
    mu_o = _head_sum(o, ones_bd) * (1.0 / HEAD_DIM)
    oc = o - mu_o
    var_o = _head_sum(oc * oc, ones_bd) * (1.0 / HEAD_DIM)
    on = oc * lax.rsqrt(var_o + GN_EPS) * gng_ref[...] + gnb_ref[...]
    o_ref[...] = (on + bonus) * g_ref[...]


def _rw_scan(rkv, lw, a, g, row0, batch, t, tb, c, params, wkv0, prev_out):
    m = rkv.shape[0]
    nt = t // tb
    rb0 = row0 // tb
    n = 2 * c
    nchunk = tb // c

    def col(off):
        return lambda b, p, i: (rb0 + b * nt + i, off + p)

    s_t = jnp.swapaxes(wkv0, -1, -2).reshape(batch, HEAD_PAIRS, 2, HEAD_DIM, HEAD_DIM)
    st0 = jnp.einsum('bphkv,hg->bphkgv', s_t, jnp.eye(2, dtype=F32)).reshape(batch, HEAD_PAIRS, LANES, LANES)
    pspec = pl.BlockSpec((1, LANES), lambda b, p, i: (0, p))
    sspec = pl.BlockSpec((None, None, LANES, LANES), lambda b, p, i: (b, p, 0, 0))
    in_specs = [pl.BlockSpec((tb, LANES), col(0)), pl.BlockSpec((tb, LANES), col(HEAD_PAIRS)),
                pl.BlockSpec((tb, LANES), col(2 * HEAD_PAIRS)),
                pl.BlockSpec((tb, LANES), col(0)), pl.BlockSpec((tb, LANES), col(0)),
                pl.BlockSpec((tb, LANES), col(0))] + [pspec] * 5 + [sspec]
    args = [rkv, rkv, rkv, lw, a, g] + list(params) + [st0]
    aliases = {}
    kern = functools.partial(_rw_scan_kernel, tb=tb, c=c)
    if prev_out is not None:
        n_real = len(args)
        in_specs.append(pl.BlockSpec(memory_space=pl.ANY))
        args.append(prev_out)
        aliases = {n_real: 0}
        kern = _drop_inputs(kern, n_real, 1)
    scratch = [pltpu.VMEM((LANES, LANES), F32), pltpu.VMEM((tb, LANES), F32)]
    scratch += [pltpu.VMEM((nchunk, 2 * n, LANES), BF16), pltpu.VMEM((nchunk, n, LANES), F32),
                pltpu.VMEM((nchunk, n, n), BF16), pltpu.VMEM((nchunk, n, LANES), F32),
                pltpu.VMEM((nchunk, LANES, 2 * n), BF16), pltpu.VMEM((nchunk, n, LANES), BF16),
                pltpu.VMEM((nchunk, LANES, LANES), F32)]
    o_c, st = pl.pallas_call(
        kern,
        grid=(batch, HEAD_PAIRS, nt),
        in_specs=in_specs,
        out_specs=[pl.BlockSpec((tb, LANES), col(0)), sspec],
        out_shape=[jax.ShapeDtypeStruct((m, RW_WIDTH), F32),
                   jax.ShapeDtypeStruct((batch, HEAD_PAIRS, LANES, LANES), F32)],
        scratch_shapes=scratch,
        input_output_aliases=aliases,
        compiler_params=_cparams(3),
        name="rwkv_scan",
    )(*args)
    st = st.reshape(batch, HEAD_PAIRS, 2, HEAD_DIM, 2, HEAD_DIM)
    wkv = jnp.stack([st[:, :, 0, :, 0, :], st[:, :, 1, :, 1, :]], axis=2)
    return o_c, jnp.swapaxes(wkv, -1, -2).reshape(batch, RW_HEADS, HEAD_DIM, HEAD_DIM)


def _outproj_kernel(oa_ref, ob_ref, oc_ref, x_ref, w_ref, g_ref, b_ref, y_ref):
    mix = _dot(oa_ref[...].astype(BF16), w_ref[0:SB_WIDTH, :])
    mix += _dot(ob_ref[...].astype(BF16), w_ref[SB_WIDTH:SB_WIDTH + POOL_WIDTH, :])
    mix += _dot(oc_ref[...].astype(BF16), w_ref[SB_WIDTH + POOL_WIDTH:, :])
    y_ref[...] = _layer_norm(DN_ALPHA * x_ref[...] + mix, g_ref[...], b_ref[...])


def _outproj_ln(oa, ob, oc, x, w_bf, g, b, tm):
    m, d = x.shape
    row = lambda i: (i, 0)
    const = lambda i: (0, 0)
    return pl.pallas_call(
        _outproj_kernel,
        grid=(m // tm,),
        in_specs=[pl.BlockSpec((tm, SB_WIDTH), row), pl.BlockSpec((tm, POOL_WIDTH), row),
                  pl.BlockSpec((tm, RW_WIDTH), row), pl.BlockSpec((tm, d), row),
                  pl.BlockSpec(w_bf.shape, const), pl.BlockSpec((1, d), const),
                  pl.BlockSpec((1, d), const)],
        out_specs=pl.BlockSpec((tm, d), row),
        out_shape=jax.ShapeDtypeStruct((m, d), F32),
        compiler_params=_cparams(1),
        name="outproj_ln",
    )(oa, ob, oc, x, w_bf, g, b)


def _router_kernel(x_ref, w_ref, b_ref, idx_ref, gate_ref, rank_ref, cnt_ref, run_ref, *, tm):
    @pl.when(pl.program_id(0) == 0)
    def _():
        run_ref[...] = jnp.zeros_like(run_ref)

    logits = _dot(x_ref[...].astype(BF16), w_ref[...].astype(BF16)) + b_ref[...]
    lane = lax.broadcasted_iota(I32, (tm, N_EXPERTS), 1).astype(F32)
    work = logits
    vals, idxs = [], []
    for _ in range(TOP_K):
        top = jnp.max(work, axis=1, keepdims=True)
        idx = jnp.min(jnp.where(work == top, lane, float(N_EXPERTS)), axis=1, keepdims=True)
        vals.append(top)
        idxs.append(idx)
        work = jnp.where(lane == idx, -jnp.inf, work)
    exps = [jnp.exp(v - vals[0]) for v in vals]
    total = exps[0] + exps[1] + exps[2] + exps[3]
    out_lane = lax.broadcasted_iota(I32, (tm, LANES), 1)
    lane_f = out_lane.astype(F32)
    onehots = [(lane_f == idxs[k]).astype(F32) for k in range(TOP_K)]
    chosen = onehots[0] + onehots[1] + onehots[2] + onehots[3]
    earlier = (lax.broadcasted_iota(I32, (tm, tm), 1) < lax.broadcasted_iota(I32, (tm, tm), 0)).astype(BF16)
    base = run_ref[...] + _dot(earlier, chosen.astype(BF16))
    run_ref[...] = run_ref[...] + jnp.sum(chosen, axis=0, keepdims=True)
    cnt_ref[...] = run_ref[...].astype(I32)
    idx_out = jnp.zeros((tm, LANES), F32)
    gate_out = jnp.zeros((tm, LANES), F32)
    rank_out = jnp.zeros((tm, LANES), F32)
    for k in range(TOP_K):
        idx_out = jnp.where(out_lane == k, idxs[k], idx_out)
        gate_out = jnp.where(out_lane == k, exps[k] / total, gate_out)
        rank_out = jnp.where(out_lane == k, jnp.sum(onehots[k] * base, axis=1, keepdims=True), rank_out)
    idx_ref[...] = idx_out.astype(I32)
    gate_ref[...] = gate_out
    rank_ref[...] = rank_out.astype(I32)


def _router(x, w, b, tm):
    m, d = x.shape
    return pl.pallas_call(
        functools.partial(_router_kernel, tm=tm),
        grid=(m // tm,),
        in_specs=[pl.BlockSpec((tm, d), lambda i: (i, 0)),
                  pl.BlockSpec((d, N_EXPERTS), lambda i: (0, 0)),
                  pl.BlockSpec((1, N_EXPERTS), lambda i: (0, 0))],
        out_specs=[pl.BlockSpec((tm, LANES), lambda i: (i, 0))] * 3 + [pl.BlockSpec((1, LANES), lambda i: (0, 0))],
        out_shape=[jax.ShapeDtypeStruct((m, LANES), I32), jax.ShapeDtypeStruct((m, LANES), F32),
                   jax.ShapeDtypeStruct((m, LANES), I32), jax.ShapeDtypeStruct((1, LANES), I32)],
        scratch_shapes=[pltpu.VMEM((1, LANES), F32)],
        compiler_params=_cparams(1),
        name="router",
    )(x, w, b)


def _row_copy(src_hbm, dst_ref, sem, src_row, dst_row):
    return pltpu.make_async_copy(src_hbm.at[pl.ds(src_row, 1), :], dst_ref.at[pl.ds(dst_row, 1), :], sem)


GATHER_UNROLL = 8


def _gather_kernel(tok_ref, nu_ref, x_hbm, o_ref, rows_ref, sem, *, bm):
    @pl.when(pl.program_id(0) < nu_ref[0])
    def _():
        def start(i, carry):
            for u in range(GATHER_UNROLL):
                r = i * GATHER_UNROLL + u
                _row_copy(x_hbm, rows_ref, sem, tok_ref[0, 0, r], r).start()
            return carry

        lax.fori_loop(0, bm // GATHER_UNROLL, start, 0)

        def wait(i, carry):
            for u in range(GATHER_UNROLL):
                _row_copy(x_hbm, rows_ref, sem, 0, i * GATHER_UNROLL + u).wait()
            return carry

        lax.fori_loop(0, bm // GATHER_UNROLL, wait, 0)
        o_ref[...] = rows_ref[...].astype(BF16)


def _gather_rows(x, buf_tok, n_used, bm):
    d = x.shape[1]
    nb = buf_tok.shape[0] // bm
    return pl.pallas_call(
        functools.partial(_gather_kernel, bm=bm),
        grid=(nb,),
        in_specs=[pl.BlockSpec((1, 1, bm), lambda i: (i, 0, 0), memory_space=pltpu.SMEM),
                  pl.BlockSpec(memory_space=pltpu.SMEM),
                  pl.BlockSpec(memory_space=pl.ANY)],
        out_specs=pl.BlockSpec((bm, d), lambda i: (i, 0)),
        out_shape=jax.ShapeDtypeStruct((nb * bm, d), BF16),
        scratch_shapes=[pltpu.VMEM((bm, d), F32), pltpu.SemaphoreType.DMA(())],
        compiler_params=_cparams(1),
        name="moe_gather",
    )(buf_tok.reshape(nb, 1, bm), n_used, x)


def _step_table(block_e, n_used):
    nb = block_e.shape[0]
    i = jnp.arange(nb, dtype=I32)
    valid = i < n_used[0]
    first = jnp.logical_and(valid, jnp.logical_or(i == 0, block_e != jnp.roll(block_e, 1)))
    first_pos = jnp.where(first, i, nb)
    next_pos = jnp.concatenate([lax.cummin(first_pos, reverse=True)[1:], jnp.full((1,), nb, I32)])
    next_e = jnp.where(next_pos < nb, block_e[jnp.minimum(next_pos, nb - 1)], -1)
    return jnp.stack([block_e, first.astype(I32), next_e.astype(I32), jnp.zeros_like(block_e)],
                     axis=1).reshape(nb * 4)


def _weight_copies(w_hbm, wbuf, sem, layer, expert, cols, tn):
    return [pltpu.make_async_copy(w_hbm.at[layer, expert, :, pl.ds(pl.multiple_of(col, tn), tn)],
                                  wbuf.at[h], sem.at[h]) for h, col in enumerate(cols)]


def _fetch_expert_tile(tbl_ref, w_hbm, wbuf, wbf, sem, layer, cols, tn):
    i = pl.program_id(1)
    expert = tbl_ref[4 * i]
    next_expert = tbl_ref[4 * i + 2]

    @pl.when(tbl_ref[4 * i + 1] == 1)
    def _():
        @pl.when(i == 0)
        def _():
            for cp in _weight_copies(w_hbm, wbuf, sem, layer, expert, cols, tn):
                cp.start()

        for cp in _weight_copies(w_hbm, wbuf, sem, layer, expert, cols, tn):
            cp.wait()
        for h in range(len(cols)):
            wbf[h] = wbuf[h].astype(BF16)

        @pl.when(next_expert >= 0)
        def _():
            for cp in _weight_copies(w_hbm, wbuf, sem, layer, next_expert, cols, tn):
                cp.start()


def _expert_up_kernel(tbl_ref, nu_ref, x_ref, w_hbm, bg_ref, bl_ref, act_ref, wbuf, wbf, sem, *, layer, tn):
    j = pl.program_id(0)
    nj = pl.num_programs(0)

    @pl.when(pl.program_id(1) < nu_ref[0])
    def _():
        _fetch_expert_tile(tbl_ref, w_hbm, wbuf, wbf, sem, layer, (j * tn, (nj + j) * tn), tn)
        x = x_ref[...]
        h_glu = _dot(x, wbf[0]) + bg_ref[...]
        h_lin = _dot(x, wbf[1]) + bl_ref[...]
        h_glu = jnp.minimum(h_glu, SWIGLU_LIMIT)
        h_lin = jnp.clip(h_lin, -SWIGLU_LIMIT, SWIGLU_LIMIT)
        act = h_glu * jax.nn.sigmoid(SWIGLU_ALPHA * h_glu) * (h_lin + 1.0)
        act_ref[...] = act.astype(BF16)


def _expert_up(xb, w1, b1, layer, tbl, n_used, bm, tn):
    rows, d = xb.shape
    de = w1.shape[3] // 2
    nj = de // tn
    nb = rows // bm

    def blk(i, nu):
        return jnp.minimum(i, nu[0] - 1)

    grid_spec = pltpu.PrefetchScalarGridSpec(
        num_scalar_prefetch=2,
        grid=(nj, nb),
        in_specs=[pl.BlockSpec((bm, d), lambda j, i, tb, nu: (blk(i, nu), 0)),
                  pl.BlockSpec(memory_space=pl.ANY),
                  pl.BlockSpec((None, None, 1, tn), lambda j, i, tb, nu: (layer, tb[4 * blk(i, nu)], 0, j)),
                  pl.BlockSpec((None, None, 1, tn),
                               lambda j, i, tb, nu: (layer, tb[4 * blk(i, nu)], 0, nj + j))],
        out_specs=pl.BlockSpec((bm, tn), lambda j, i, tb, nu: (blk(i, nu), j)),
        scratch_shapes=[pltpu.VMEM((2, d, tn), F32), pltpu.VMEM((2, d, tn), BF16),
                        pltpu.SemaphoreType.DMA((2,))],
    )
    return pl.pallas_call(
        functools.partial(_expert_up_kernel, layer=layer, tn=tn),
        grid_spec=grid_spec,
        out_shape=jax.ShapeDtypeStruct((rows, de), BF16),
        compiler_params=_cparams(2),
        name="moe_up",
    )(tbl, n_used, xb, w1, b1, b1)


def _expert_down_kernel(tbl_ref, nu_ref, a_ref, w_hbm, b_ref, y_ref, wbuf, wbf, sem, *, layer, tn):
    j = pl.program_id(0)

    @pl.when(pl.program_id(1) < nu_ref[0])
    def _():
        _fetch_expert_tile(tbl_ref, w_hbm, wbuf, wbf, sem, layer, (j * tn,), tn)
        y_ref[...] = _dot(a_ref[...], wbf[0]) + b_ref[...]


def _expert_down(act, w2, b2, layer, tbl, n_used, bm, tn):
    rows, de = act.shape
    d = w2.shape[3]
    nj = d // tn
    nb = rows // bm

    def blk(i, nu):
        return jnp.minimum(i, nu[0] - 1)

    grid_spec = pltpu.PrefetchScalarGridSpec(
        num_scalar_prefetch=2,
        grid=(nj, nb),
        in_specs=[pl.BlockSpec((bm, de), lambda j, i, tb, nu: (blk(i, nu), 0)),
                  pl.BlockSpec(memory_space=pl.ANY),
                  pl.BlockSpec((None, None, 1, tn), lambda j, i, tb, nu: (layer, tb[4 * blk(i, nu)], 0, j))],
        out_specs=pl.BlockSpec((bm, tn), lambda j, i, tb, nu: (blk(i, nu), j)),
        scratch_shapes=[pltpu.VMEM((1, de, tn), F32), pltpu.VMEM((1, de, tn), BF16),
                        pltpu.SemaphoreType.DMA((1,))],
    )
    return pl.pallas_call(
        functools.partial(_expert_down_kernel, layer=layer, tn=tn),
        grid_spec=grid_spec,
        out_shape=jax.ShapeDtypeStruct((rows, d), F32),
        compiler_params=_cparams(2),
        name="moe_down",
    )(tbl, n_used, act, w2, b2)


def _combine_kernel(dest_ref, y_hbm, gate_ref, x_ref, g_ref, b_ref, o_ref, rows_ref, sem, *, tm):
    def start(r, carry):
        for k in range(TOP_K):
            _row_copy(y_hbm, rows_ref.at[k], sem, dest_ref[0, 0, r * TOP_K + k], r).start()
        return carry

    lax.fori_loop(0, tm, start, 0)

    def wait(r, carry):
        for k in range(TOP_K):
            _row_copy(y_hbm, rows_ref.at[k], sem, 0, r).wait()
        return carry

    lax.fori_loop(0, tm, wait, 0)
    gates = gate_ref[...]
    ffn = gates[:, 0:1] * rows_ref[0]
    for k in range(1, TOP_K):
        ffn += gates[:, k:k + 1] * rows_ref[k]
    o_ref[...] = _layer_norm(DN_ALPHA * x_ref[...] + ffn, g_ref[...], b_ref[...])


def _combine_ln(y, dest, gates, x, g, b, tm):
    m, d = x.shape
    nb = m // tm
    return pl.pallas_call(
        functools.partial(_combine_kernel, tm=tm),
        grid=(nb,),
        in_specs=[pl.BlockSpec((1, 1, tm * TOP_K), lambda i: (i, 0, 0), memory_space=pltpu.SMEM),
                  pl.BlockSpec(memory_space=pl.ANY),
                  pl.BlockSpec((tm, LANES), lambda i: (i, 0)),
                  pl.BlockSpec((tm, d), lambda i: (i, 0)),
                  pl.BlockSpec((1, d), lambda i: (0, 0)),
                  pl.BlockSpec((1, d), lambda i: (0, 0))],
        out_specs=pl.BlockSpec((tm, d), lambda i: (i, 0)),
        out_shape=jax.ShapeDtypeStruct((m, d), F32),
        scratch_shapes=[pltpu.VMEM((TOP_K, tm, d), F32), pltpu.SemaphoreType.DMA(())],
        compiler_params=_cparams(1),
        name="moe_combine_ln",
    )(dest.reshape(nb, 1, tm * TOP_K), y, gates, x, g, b)


def _routing_tables(top_i, rank, counts, bm):
    n = top_i.shape[0]
    nk = n * TOP_K
    padded = (counts + bm - 1) // bm * bm
    pad_end = jnp.cumsum(padded)
    pad_start = pad_end - padded
    dest = (pad_start[top_i] + rank).astype(I32).reshape(nk)
    nb = (nk + N_EXPERTS * (bm - 1)) // bm
    buf_tok = jnp.zeros((nb * bm,), I32).at[dest].set(jnp.arange(nk, dtype=I32) // TOP_K)
    block_row = jnp.arange(nb, dtype=I32)[:, None] * bm
    block_e = jnp.minimum(jnp.sum((pad_end[None, :] <= block_row).astype(I32), axis=1), N_EXPERTS - 1)
    n_used = (pad_end[-1:] // bm).astype(I32)
    return dest, buf_tok, _step_table(block_e, n_used), n_used


def _moe_ln(x, lp, big, layer):
    idx, gates, rank, counts = _router(x, lp['w_router'], lp['b_router'].reshape(1, N_EXPERTS), tm=256)
    dest, buf_tok, steps, n_used = _routing_tables(idx[:, :TOP_K], rank[:, :TOP_K],
                                                     counts[0, :N_EXPERTS], MOE_BM)
    xb = _gather_rows(x, buf_tok, n_used, MOE_BM)
    nl, e, d, de2 = big['w_e1'].shape
    act = _expert_up(xb, big['w_e1'], big['b_e1'].reshape(nl, e, 1, de2), layer, steps, n_used,
                     MOE_BM, tn=1024)
    y = _expert_down(act, big['w_e2'], big['b_e2'].reshape(nl, e, 1, d), layer, steps, n_used,
                     MOE_BM, tn=1024)
    return _combine_ln(y, dest, gates, x, lp['ln_ffn_g'].reshape(1, d), lp['ln_ffn_b'].reshape(1, d),
                       tm=128)


def _layer(x, lp, big, layer, st, n_p, bp, tp, bs, ts):
    d = x.shape[1]
    h = _inproj(x, big['w_in'], layer, tm=768, tn=768)
    qkv_w = 3 * SB_WIDTH
    u = h[:, qkv_w:qkv_w + POOL_WIDTH]
    c = h[:, qkv_w + POOL_WIDTH:]

    o_a = _sb_attention(h, 0, bp, tp, 256, None, None)
    hist = (st['cache_k'].reshape(bs, -1, SB_WIDTH), st['cache_v'].reshape(bs, -1, SB_WIDTH))
    o_a = _sb_attention(h, n_p, bs, ts, ts, hist, o_a)
    past = hist[0].shape[1]

    w_pool = lp['pool_w']
    scale = lp['pool_scale'].reshape(1, POOL_WIDTH)
    o_b = _pool_mixer(u, 0, bp, tp, 512, jnp.zeros((bp, 16, POOL_WIDTH), F32), w_pool, scale, 0, None)
    hist16 = jnp.pad(st['state_pool'], ((0, 0), (1, 0), (0, 0)))
    o_b = _pool_mixer(u, n_p, bs, ts, ts, hist16, w_pool, scale, past, o_b)

    zero_lora = jnp.zeros((RW_LORA // 2, RW_WIDTH), F32)
    w2p = jnp.concatenate([lp['rw_w2'], zero_lora], axis=0)
    a2p = jnp.concatenate([zero_lora, lp['rw_a2']], axis=0)
    pre_args = (lp['rw_mu'].reshape(1, RW_COLS), lp['rw_w0'].reshape(1, RW_WIDTH), w2p,
                lp['rw_a0'].reshape(1, RW_WIDTH), a2p, lp['rw_g2'])
    pre = _rw_pre(c, 0, bp, tp, 256, jnp.zeros((bp, 1, RW_COLS), F32), *pre_args, None)
    rkv, lw, a, g = _rw_pre(c, n_p, bs, ts, ts, st['state_shift'], *pre_args, pre)
    prm = tuple(lp[k].reshape(1, RW_WIDTH) for k in ('rw_kk', 'rw_ka', 'rw_rk', 'rw_gn_g', 'rw_gn_b'))
    o_c, wkv_p = _rw_scan(rkv, lw, a, g, 0, bp, tp, 512, 64, prm,
                          jnp.zeros((bp, RW_HEADS, HEAD_DIM, HEAD_DIM), F32), None)
    o_c, wkv_s = _rw_scan(rkv, lw, a, g, n_p, bs, ts, ts, ts, prm, st['state_wkv'], o_c)

    x1 = _outproj_ln(o_a, o_b, o_c, x, lp['w_out'].astype(BF16), lp['ln_mix_g'].reshape(1, d),
                     lp['ln_mix_b'].reshape(1, d), tm=256)
    x2 = _moe_ln(x1, lp, big, layer)

    def states(rows, b, t, wkv):
        hb = rows.reshape(b, t, -1)
        k_new = hb[:, :, SB_WIDTH:2 * SB_WIDTH].reshape(b, t, SB_HEADS, HEAD_DIM)
        v_new = hb[:, :, 2 * SB_WIDTH:qkv_w].reshape(b, t, SB_HEADS, HEAD_DIM)
        pool_new = hb[:, t - POOL_HIST:, qkv_w:qkv_w + POOL_WIDTH]
        shift_new = hb[:, t - 1:, qkv_w + POOL_WIDTH:]
        return k_new, v_new, pool_new, shift_new, wkv

    return x2, states(h[:n_p], bp, tp, wkv_p), states(h[n_p:], bs, ts, wkv_s)


def kernel(x_prompt, x_sample, cache_k, cache_v, state_pool, state_shift, state_wkv, w_in, w_out, ln_mix_g, ln_mix_b, ln_ffn_g, ln_ffn_b, pool_w, pool_scale, rw_mu, rw_w0, rw_w2, rw_a0, rw_a2, rw_g2, rw_kk, rw_ka, rw_rk, rw_gn_g, rw_gn_b, w_router, b_router, w_e1, b_e1, w_e2, b_e2):
    bp, tp, d = x_prompt.shape
    bs, ts, _ = x_sample.shape
    n_p = bp * tp
    x = jnp.concatenate([x_prompt.reshape(n_p, d), x_sample.reshape(bs * ts, d)], axis=0)
    big = dict(w_in=w_in, w_e1=w_e1, b_e1=b_e1, w_e2=w_e2, b_e2=b_e2)
    weights = dict(w_out=w_out, ln_mix_g=ln_mix_g, ln_mix_b=ln_mix_b, ln_ffn_g=ln_ffn_g,
                   ln_ffn_b=ln_ffn_b, pool_w=pool_w, pool_scale=pool_scale, rw_mu=rw_mu, rw_w0=rw_w0,
                   rw_w2=rw_w2, rw_a0=rw_a0, rw_a2=rw_a2, rw_g2=rw_g2, rw_kk=rw_kk, rw_ka=rw_ka,
                   rw_rk=rw_rk, rw_gn_g=rw_gn_g, rw_gn_b=rw_gn_b, w_router=w_router,
                   b_router=b_router)
    st_p, st_s = [], []
    for l in range(w_in.shape[0]):
        lp = {k: v[l] for k, v in weights.items()}
        st = dict(cache_k=cache_k[l], cache_v=cache_v[l], state_pool=state_pool[l],
                  state_shift=state_shift[l], state_wkv=state_wkv[l])
        x, sp, ss = _layer(x, lp, big, l, st, n_p, bp, tp, bs, ts)
        st_p.append(sp)
        st_s.append(ss)
    y_prompt = x[:n_p].reshape(bp, tp, d)
    y_sample = x[n_p:].reshape(bs, ts, d)
    k_p, v_p, pool_p, shift_p, wkv_p = (jnp.stack([s[i] for s in st_p]) for i in range(5))
    k_s, v_s, pool_s, shift_s, wkv_s = (jnp.stack([s[i] for s in st_s]) for i in range(5))
    return (y_prompt, y_sample, k_p, v_p, pool_p, shift_p, wkv_p, k_s, v_s, pool_s, shift_s, wkv_s)
```

```python
import functools

import jax
import jax.numpy as jnp
from jax import lax
from jax.experimental import pallas as pl
from jax.experimental.pallas import tpu as pltpu

F32 = jnp.float32
BF16 = jnp.bfloat16
I32 = jnp.int32

HEAD_DIM = 64
SB_HEADS = 12
SB_WIDTH = SB_HEADS * HEAD_DIM
POOL_WINDOWS = (2, 4, 8, 16)
POOL_GC = 128
POOL_WIDTH = POOL_GC * len(POOL_WINDOWS)
POOL_HIST = max(POOL_WINDOWS) - 1
RW_HEADS = 12
RW_WIDTH = RW_HEADS * HEAD_DIM
RW_LORA = 128
RW_GATE = 128
RW_COLS = 3 * RW_WIDTH + RW_LORA + RW_GATE
N_EXPERTS = 32
TOP_K = 4
SWIGLU_LIMIT = 7.0
SWIGLU_ALPHA = 1.702
DEPTH = 2
DN_ALPHA = (2.0 * DEPTH) ** 0.25
LN_EPS = 1e-5
GN_EPS = 64e-5

LANES = 128
HEAD_PAIRS = SB_WIDTH // LANES
VMEM_LIMIT = 56 * 1024 * 1024
ATT_LOG_FLOOR = -110.0

POOL_COL0 = 3 * SB_WIDTH
RW_COL0 = POOL_COL0 + POOL_WIDTH
RW_COL_TILE = 256
RW_COL_TILES = RW_COLS // RW_COL_TILE
MOE_BM = 256
RW_PREP_GROUP = 8


def _cparams(n_axes):
    return pltpu.CompilerParams(dimension_semantics=("arbitrary",) * n_axes,
                                vmem_limit_bytes=VMEM_LIMIT)


def _dot(a, b, precision=None):
    return jnp.dot(a, b, preferred_element_type=F32, precision=precision)


def _dot_nt(a, b, precision=None):
    return lax.dot_general(a, b, (((1,), (1,)), ((), ())), preferred_element_type=F32,
                           precision=precision)


def _dot_tn(a, b, precision=None):
    return lax.dot_general(a, b, (((0,), (0,)), ((), ())), preferred_element_type=F32,
                           precision=precision)


def _layer_norm(x, g, b):
    mu = jnp.mean(x, axis=-1, keepdims=True)
    xc = x - mu
    var = jnp.mean(xc * xc, axis=-1, keepdims=True)
    return xc * lax.rsqrt(var + LN_EPS) * g + b


def _inproj_kernel(x_ref, w_ref, o_ref, wbf_ref):
    @pl.when(pl.program_id(1) == 0)
    def _():
        wbf_ref[...] = w_ref[...].astype(BF16)

    o_ref[...] = _dot(x_ref[...].astype(BF16), wbf_ref[...])


def _inproj(x, w, layer, tm, tn):
    m, k = x.shape
    n = w.shape[2]
    return pl.pallas_call(
        _inproj_kernel,
        grid=(n // tn, m // tm),
        in_specs=[pl.BlockSpec((tm, k), lambda j, i: (i, 0)),
                  pl.BlockSpec((None, k, tn), lambda j, i: (layer, 0, j))],
        out_specs=pl.BlockSpec((tm, tn), lambda j, i: (i, j)),
        out_shape=jax.ShapeDtypeStruct((m, n), F32),
        scratch_shapes=[pltpu.VMEM((k, tn), BF16)],
        compiler_params=_cparams(2),
        name="inproj",
    )(x, w)


def _split2(a):
    hi = a.astype(BF16)
    lo = (a - hi.astype(F32)).astype(BF16)
    return hi, lo


def _suffix_matrix(n):
    return (lax.broadcasted_iota(I32, (n, n), 0) > lax.broadcasted_iota(I32, (n, n), 1)).astype(BF16)


def _sb_block(qhs, k_bf, v_bf, suffix, carries, mask):
    heads = range(2)
    z = [_dot_nt(qhs[h], k_bf) for h in heads]
    sp = [jnp.maximum(z_, 0.0) + jnp.log1p(jnp.exp(-jnp.abs(z_))) for z_ in z]
    log_stay = [-s_ if mask is None else jnp.where(mask, -s_, 0.0) for s_ in sp]
    parts = [_split2(l_) for l_ in log_stay]
    hi = [_dot(parts[h][0], suffix) for h in heads]
    lo = [_dot(parts[h][1], suffix) for h in heads]
    after = [hi[h] + lo[h] + carries[h] for h in heads]
    att = [jnp.exp(z[h] - sp[h] + after[h]) for h in heads]
    if mask is not None:
        att = [jnp.where(mask, a_, 0.0) for a_ in att]
    pv = [_dot(att[h].astype(BF16), v_bf) for h in heads]
    return pv, [after[h][:, 0:1] + log_stay[h][:, 0:1] for h in heads]


def _sb_live(carries):
    return (jnp.max(jnp.maximum(carries[0], carries[1])) > ATT_LOG_FLOOR).astype(I32)


def _sb_sweep(qhs, k_ref, v_ref, tk, n_blocks, suffix, live, accs, carries):
    def cond(s):
        return jnp.logical_and(s[0] < n_blocks, s[1] > 0)

    def body(s):
        step, _, accs, carries = s
        rows = pl.ds(pl.multiple_of((n_blocks - 1 - step) * tk, tk), tk)
        k_bf = k_ref[rows, :].astype(BF16)
        v_bf = v_ref[rows, :].astype(BF16)
        pv, carries = _sb_block(qhs, k_bf, v_bf, suffix, carries, None)
        accs = tuple(accs[h] + pv[h] for h in range(2))
        return step + 1, _sb_live(carries), accs, tuple(carries)

    _, live, accs, carries = lax.while_loop(cond, body, (jnp.int32(0), live, accs, carries))
    return live, accs, carries


def _sb_kernel(*refs, tq, th, past):
    if past:
        q_ref, kn_ref, vn_ref, kh_ref, vh_ref, o_ref, kbf_ref, vbf_ref = refs
    else:
        q_ref, kn_ref, vn_ref, o_ref, kbf_ref, vbf_ref = refs
    qi = pl.program_id(2)

    @pl.when(qi == 0)
    def _():
        kbf_ref[...] = kn_ref[...].astype(BF16)
        vbf_ref[...] = vn_ref[...].astype(BF16)

    q = q_ref[...] * (HEAD_DIM ** -0.5)
    lane = lax.broadcasted_iota(I32, (tq, LANES), 1)
    qhs = [jnp.where(lane < HEAD_DIM, q, 0.0).astype(BF16),
           jnp.where(lane >= HEAD_DIM, q, 0.0).astype(BF16)]
    suffix_new = _suffix_matrix(tq)
    diag_mask = lax.broadcasted_iota(I32, (tq, tq), 1) < lax.broadcasted_iota(I32, (tq, tq), 0)
    rows = pl.ds(pl.multiple_of(qi * tq, tq), tq)
    k_bf = kbf_ref[rows, :]
    v_bf = vbf_ref[rows, :]
    accs, carries = _sb_block(qhs, k_bf, v_bf, suffix_new, [jnp.zeros((tq, 1), F32)] * 2, diag_mask)
    accs, carries = tuple(accs), tuple(carries)
    live, accs, carries = _sb_sweep(qhs, kbf_ref, vbf_ref, tq, qi, suffix_new, _sb_live(carries),
                                    accs, carries)
    if past:
        _, accs, carries = _sb_sweep(qhs, kh_ref, vh_ref, th, past // th, _suffix_matrix(th), live,
                                     accs, carries)
    o_ref[...] = jnp.where(lane < HEAD_DIM, accs[0], accs[1])


def _sb_attention(h_all, row0, batch, t, tq, hist, prev_out):
    m = h_all.shape[0]
    nq = t // tq
    rb0 = row0 // tq
    sb0 = row0 // t
    in_specs = [pl.BlockSpec((tq, LANES), lambda b, p, i: (rb0 + b * nq + i, p)),
                pl.BlockSpec((t, LANES), lambda b, p, i: (sb0 + b, HEAD_PAIRS + p)),
                pl.BlockSpec((t, LANES), lambda b, p, i: (sb0 + b, 2 * HEAD_PAIRS + p))]
    args = [h_all, h_all, h_all]
    past, th = 0, 0
    if hist is not None:
        past = hist[0].shape[1]
        th = min(past, 512)
        in_specs += [pl.BlockSpec((None, past, LANES), lambda b, p, i: (b, 0, p))] * 2
        args += list(hist)
    aliases = {}
    if prev_out is not None:
        in_specs.append(pl.BlockSpec(memory_space=pl.ANY))
        args.append(prev_out)
        aliases = {len(args) - 1: 0}
    kern = functools.partial(_sb_kernel, tq=tq, th=th, past=past)
    if prev_out is not None:
        kern = _drop_last_input(kern, n_in=len(args))
    return pl.pallas_call(
        kern,
        grid=(batch, HEAD_PAIRS, nq),
        in_specs=in_specs,
        out_specs=pl.BlockSpec((tq, LANES), lambda b, p, i: (rb0 + b * nq + i, p)),
        out_shape=jax.ShapeDtypeStruct((m, SB_WIDTH), F32),
        scratch_shapes=[pltpu.VMEM((t, LANES), BF16)] * 2,
        input_output_aliases=aliases,
        compiler_params=_cparams(3),
        name="sb_attention",
    )(*args)


def _drop_last_input(kern, n_in):
    def wrapped(*refs):
        return kern(*refs[:n_in - 1], *refs[n_in:])
    return wrapped


def _pool_kernel(u0_ref, u1_ref, u2_ref, u3_ref, hist_ref, w_ref, scale_ref, o_ref, ext_ref, *, tt, pos0):
    ti = pl.program_id(1)
    halo = 16

    @pl.when(ti == 0)
    def _():
        ext_ref[0:halo, :] = hist_ref[...]

    @pl.when(ti > 0)
    def _():
        ext_ref[0:halo, :] = ext_ref[tt:tt + halo, :]

    u = jnp.concatenate([u0_ref[...], u1_ref[...], u2_ref[...], u3_ref[...]], axis=1)
    ext_ref[halo:halo + tt, :] = u
    pos = pos0 + ti * tt + lax.broadcasted_iota(I32, (tt, 1), 0)
    for g, w in enumerate(POOL_WINDOWS):
        cols = slice(g * POOL_GC, (g + 1) * POOL_GC)
        s = u[:, cols]
        for k in range(1, w):
            s = s + ext_ref[halo - k:halo - k + tt, cols]
        cnt = jnp.minimum(pos + 1, w).astype(F32)
        pooled = s / cnt - u[:, cols]
        y = _dot(pooled.astype(BF16), w_ref[g].astype(BF16))
        o_ref[:, cols] = y * scale_ref[:, cols]


def _pool_mixer(h_all, row0, batch, t, tt, hist16, w_pool, scale, pos0, prev_out):
    m = h_all.shape[0]
    nt = t // tt
    rb0 = row0 // tt
    tile0 = POOL_COL0 // POOL_GC
    in_specs = [pl.BlockSpec((tt, POOL_GC), lambda b, i, g=g: (rb0 + b * nt + i, tile0 + g))
                for g in range(len(POOL_WINDOWS))]
    in_specs += [pl.BlockSpec((None, 16, POOL_WIDTH), lambda b, i: (b, 0, 0)),
                pl.BlockSpec((len(POOL_WINDOWS), POOL_GC, POOL_GC), lambda b, i: (0, 0, 0)),
                pl.BlockSpec((1, POOL_WIDTH), lambda b, i: (0, 0))]
    args = [h_all] * len(POOL_WINDOWS) + [hist16, w_pool, scale]
    aliases = {}
    kern = functools.partial(_pool_kernel, tt=tt, pos0=pos0)
    if prev_out is not None:
        in_specs.append(pl.BlockSpec(memory_space=pl.ANY))
        args.append(prev_out)
        aliases = {len(args) - 1: 0}
        kern = _drop_last_input(kern, n_in=len(args))
    return pl.pallas_call(
        kern,
        grid=(batch, nt),
        in_specs=in_specs,
        out_specs=pl.BlockSpec((tt, POOL_WIDTH), lambda b, i: (rb0 + b * nt + i, 0)),
        out_shape=jax.ShapeDtypeStruct((m, POOL_WIDTH), F32),
        scratch_shapes=[pltpu.VMEM((tt + 16, POOL_WIDTH), F32)],
        input_output_aliases=aliases,
        compiler_params=_cparams(2),
        name="pool_mixer",
    )(*args)


def _rw_pre_kernel(*refs, tt):
    c_refs = refs[:RW_COL_TILES]
    (hist_ref, mu_ref, w0_ref, w2_ref, a0_ref, a2_ref, g2_ref,
     rkv_ref, lw_ref, a_ref, g_ref, last_ref) = refs[RW_COL_TILES:]
    ti = pl.program_id(1)

    @pl.when(ti == 0)
    def _():
        last_ref[...] = hist_ref[...]

    c = jnp.concatenate([ref[...] for ref in c_refs], axis=1)
    row = lax.broadcasted_iota(I32, (tt, 1), 0)
    prev = jnp.where(row == 0, last_ref[...], pltpu.roll(c, 1, 0))
    last_ref[...] = c[tt - 1:tt, :]
    xs = c + (prev - c) * mu_ref[...]
    rkv_ref[...] = xs[:, :3 * RW_WIDTH]
    lora = xs[:, 3 * RW_WIDTH:3 * RW_WIDTH + RW_LORA]
    wd = _dot(jnp.tanh(lora).astype(BF16), w2_ref[...].astype(BF16))
    sp = jax.nn.softplus(-(w0_ref[...] + wd))
    lw_ref[...] = -jnp.exp(-sp - 0.5)
    a_ref[...] = jax.nn.sigmoid(a0_ref[...] + _dot(lora.astype(BF16), a2_ref[...].astype(BF16)))
    gd = xs[:, 3 * RW_WIDTH + RW_LORA:]
    g_ref[...] = _dot(jax.nn.sigmoid(gd).astype(BF16), g2_ref[...].astype(BF16))


def _rw_pre(h_all, row0, batch, t, tt, shift_hist, mu, w0, w2p, a0, a2p, g2, prev_outs):
    m = h_all.shape[0]
    nt = t // tt
    rb0 = row0 // tt
    row_map = lambda b, i: (rb0 + b * nt + i, 0)
    const = lambda b, i: (0, 0)
    tile0 = RW_COL0 // RW_COL_TILE
    in_specs = [pl.BlockSpec((tt, RW_COL_TILE), lambda b, i, j=j: (rb0 + b * nt + i, tile0 + j))
                for j in range(RW_COL_TILES)]
    in_specs += [pl.BlockSpec((None, 1, RW_COLS), lambda b, i: (b, 0, 0)),
                pl.BlockSpec((1, RW_COLS), const),
                pl.BlockSpec((1, RW_WIDTH), const),
                pl.BlockSpec((RW_LORA, RW_WIDTH), const),
                pl.BlockSpec((1, RW_WIDTH), const),
                pl.BlockSpec((RW_LORA, RW_WIDTH), const),
                pl.BlockSpec((RW_GATE, RW_WIDTH), const)]
    args = [h_all] * RW_COL_TILES + [shift_hist, mu, w0, w2p, a0, a2p, g2]
    widths = (3 * RW_WIDTH, RW_WIDTH, RW_WIDTH, RW_WIDTH)
    aliases = {}
    kern = functools.partial(_rw_pre_kernel, tt=tt)
    if prev_outs is not None:
        n_real = len(args)
        for j, po in enumerate(prev_outs):
            in_specs.append(pl.BlockSpec(memory_space=pl.ANY))
            args.append(po)
            aliases[n_real + j] = j
        kern = _drop_inputs(kern, n_real, len(prev_outs))
    return pl.pallas_call(
        kern,
        grid=(batch, nt),
        in_specs=in_specs,
        out_specs=[pl.BlockSpec((tt, wd), row_map) for wd in widths],
        out_shape=[jax.ShapeDtypeStruct((m, wd), F32) for wd in widths],
        scratch_shapes=[pltpu.VMEM((1, RW_COLS), F32)],
        input_output_aliases=aliases,
        compiler_params=_cparams(2),
        name="rwkv_pre",
    )(*args)


def _drop_inputs(kern, n_real, n_drop):
    def wrapped(*refs):
        return kern(*refs[:n_real], *refs[n_real + n_drop:])
    return wrapped


def _head_ones():
    r = lax.broadcasted_iota(I32, (LANES, LANES), 0) // HEAD_DIM
    c = lax.broadcasted_iota(I32, (LANES, LANES), 1) // HEAD_DIM
    return (r == c).astype(BF16)


def _head_sum(x, ones_bd):
    hi, lo = _split2(x)
    return _dot(hi, ones_bd) + _dot(lo, ones_bd)


def _stack_heads(x):
    lane = lax.broadcasted_iota(I32, x.shape, 1)
    zero = jnp.zeros_like(x)
    return jnp.concatenate([jnp.where(lane < HEAD_DIM, x, zero), jnp.where(lane >= HEAD_DIM, x, zero)],
                           axis=0)


def _rw_scan_kernel(r_ref, k_ref, v_ref, lw_ref, a_ref, g_ref, kks_ref, ka_ref, rk_ref, gng_ref,
                    gnb_ref, s0_ref, o_ref, sout_ref, st_ref, o_s, lhs_s, ufree_s, prb_s, akvo_s, bkt_s,
                    v_s, e_s, *, tb, c):
    ti = pl.program_id(2)
    n = 2 * c
    nchunk = tb // c

    @pl.when(ti == 0)
    def _():
        st_ref[...] = s0_ref[...]

    ones_bd = _head_ones()
    r = r_ref[...]
    k0 = k_ref[...]
    v = v_ref[...]
    lw = lw_ref[...]
    a = a_ref[...]
    kk = k0 * kks_ref[...]
    kk = kk * lax.rsqrt(jnp.maximum(_head_sum(kk * kk, ones_bd), 1e-24))
    kmod = k0 * (1.0 + (a - 1.0) * ka_ref[...])
    ri = lax.broadcasted_iota(I32, (tb, tb), 0)
    ci = lax.broadcasted_iota(I32, (tb, tb), 1)
    l_chunk = jnp.logical_and(ri >= ci, ri // c == ci // c).astype(BF16)
    lw_hi, lw_lo = _split2(lw)
    gcum = _dot(l_chunk, lw_hi) + _dot(l_chunk, lw_lo)
    e_in = jnp.exp(gcum)
    e_neg = jnp.exp(-gcum)
    al_all = (-kk * jnp.exp(gcum - lw)).astype(BF16)
    be_all = (kk * a * e_neg).astype(BF16)
    kc_all = (kmod * e_neg).astype(BF16)
    rc_all = (r * e_in).astype(BF16)
    bonus = _head_sum(r * kmod * rk_ref[...], ones_bd) * v

    rn = lax.broadcasted_iota(I32, (n, n), 0)
    cn = lax.broadcasted_iota(I32, (n, n), 1)
    same_head = rn // c == cn // c
    strict = jnp.logical_and(same_head, cn < rn).astype(F32)
    incl = jnp.logical_and(same_head, cn <= rn).astype(F32)
    eye = (rn == cn).astype(F32)

    def prepare(group):
        each = lambda f, *lists: [f(*xs) for xs in zip(*lists)]
        rows = [slice(j * c, (j + 1) * c) for j in group]
        al = [_stack_heads(al_all[r_]) for r_ in rows]
        rc = [_stack_heads(rc_all[r_]) for r_ in rows]
        bk = [jnp.concatenate([_stack_heads(be_all[r_]), _stack_heads(kc_all[r_])], axis=0) for r_ in rows]
        vs = [_stack_heads(v[r_]).astype(BF16) for r_ in rows]
        m4 = each(lambda al_, rc_, bk_: _dot_nt(jnp.concatenate([al_, rc_], axis=0), bk_), al, rc, bk)
        a_ab = [m[:n, :n] * strict for m in m4]
        lower = [jnp.concatenate([m[:n, n:] * strict, m[n:, n:] * incl], axis=0).astype(BF16) for m in m4]
        p_rb = [(m[n:, :n] * incl).astype(BF16) for m in m4]
        akv = each(_dot, lower, vs)
        tinv = [eye + a_ for a_ in a_ab]
        p_bf = [a_.astype(BF16) for a_ in a_ab]
        p_bf = [_dot(p_, p_).astype(BF16) for p_ in p_bf]
        span = 2
        while span < c:
            if 2 * span < c:
                both = each(lambda p_, t_: _dot(p_, jnp.concatenate([t_.astype(BF16), p_], axis=1)),
                            p_bf, tinv)
                tinv = each(lambda t_, b_: t_ + b_[:, :n], tinv, both)
                p_bf = [b_[:, n:].astype(BF16) for b_ in both]
            else:
                tinv = each(lambda t_, p_: t_ + _dot(p_, t_.astype(BF16)), tinv, p_bf)
            span *= 2
        wu = each(lambda t_, al_, akv_: _dot(t_.astype(BF16),
                                             jnp.concatenate([al_, akv_[:n].astype(BF16)], axis=1)),
                  tinv, al, akv)
        for i, j in enumerate(group):
            lhs_s[j] = jnp.concatenate([wu[i][:, :LANES].astype(BF16), rc[i]], axis=0)
            ufree_s[j] = wu[i][:, LANES:]
            prb_s[j] = p_rb[i]
            akvo_s[j] = akv[i][n:]
            bkt_s[j] = bk[i].astype(F32).T.astype(BF16)
            v_s[j] = vs[i]
            e_last = e_in[(j + 1) * c - 1:(j + 1) * c, :]
            e_s[j] = jnp.broadcast_to(e_last, (LANES, LANES)).T

    for j0 in range(0, nchunk, RW_PREP_GROUP):
        prepare(range(j0, min(j0 + RW_PREP_GROUP, nchunk)))

    def advance(j, st):
        rows = pl.ds(pl.multiple_of(j * c, c), c)
        ws = _dot(lhs_s[j], st.astype(BF16))
        u = ws[:n] + ufree_s[j]
        u_bf = u.astype(BF16)
        o_bd = ws[n:] + _dot(prb_s[j], u_bf) + akvo_s[j]
        o_s[rows, :] = o_bd[:c] + o_bd[c:]
        return e_s[j] * (st + _dot(bkt_s[j], jnp.concatenate([u_bf, v_s[j]], axis=0)))

    st = lax.fori_loop(0, nchunk, advance, st_ref[...])
    st_ref[...] = st
    sout_ref[...] = st

    o = o_s[...]
    mu_o = _head_sum(o, ones_bd) * (1.0 / HEAD_DIM)
    oc = o - mu_o
    var_o = _head_sum(oc * oc, ones_bd) * (1.0 / HEAD_DIM)
    on = oc * lax.rsqrt(var_o + GN_EPS) * gng_ref[...] + gnb_ref[...]
    o_ref[...] = (on + bonus) * g_ref[...]


def _rw_scan(rkv, lw, a, g, row0, batch, t, tb, c, params, wkv0, prev_out):
    m = rkv.shape[0]
    nt = t // tb
    rb0 = row0 // tb
    n = 2 * c
    nchunk = tb // c

    def col(off):
        return lambda b, p, i: (rb0 + b * nt + i, off + p)

    s_t = jnp.swapaxes(wkv0, -1, -2).reshape(batch, HEAD_PAIRS, 2, HEAD_DIM, HEAD_DIM)
    st0 = jnp.einsum('bphkv,hg->bphkgv', s_t, jnp.eye(2, dtype=F32)).reshape(batch, HEAD_PAIRS, LANES, LANES)
    pspec = pl.BlockSpec((1, LANES), lambda b, p, i: (0, p))
    sspec = pl.BlockSpec((None, None, LANES, LANES), lambda b, p, i: (b, p, 0, 0))
    in_specs = [pl.BlockSpec((tb, LANES), col(0)), pl.BlockSpec((tb, LANES), col(HEAD_PAIRS)),
                pl.BlockSpec((tb, LANES), col(2 * HEAD_PAIRS)),
                pl.BlockSpec((tb, LANES), col(0)), pl.BlockSpec((tb, LANES), col(0)),
                pl.BlockSpec((tb, LANES), col(0))] + [pspec] * 5 + [sspec]
    args = [rkv, rkv, rkv, lw, a, g] + list(params) + [st0]
    aliases = {}
    kern = functools.partial(_rw_scan_kernel, tb=tb, c=c)
    if prev_out is not None:
        n_real = len(args)
        in_specs.append(pl.BlockSpec(memory_space=pl.ANY))
        args.append(prev_out)
        aliases = {n_real: 0}
        kern = _drop_inputs(kern, n_real, 1)
    scratch = [pltpu.VMEM((LANES, LANES), F32), pltpu.VMEM((tb, LANES), F32)]
    scratch += [pltpu.VMEM((nchunk, 2 * n, LANES), BF16), pltpu.VMEM((nchunk, n, LANES), F32),
                pltpu.VMEM((nchunk, n, n), BF16), pltpu.VMEM((nchunk, n, LANES), F32),
                pltpu.VMEM((nchunk, LANES, 2 * n), BF16), pltpu.VMEM((nchunk, n, LANES), BF16),
                pltpu.VMEM((nchunk, LANES, LANES), F32)]
    o_c, st = pl.pallas_call(
        kern,
        grid=(batch, HEAD_PAIRS, nt),
        in_specs=in_specs,
        out_specs=[pl.BlockSpec((tb, LANES), col(0)), sspec],
        out_shape=[jax.ShapeDtypeStruct((m, RW_WIDTH), F32),
                   jax.ShapeDtypeStruct((batch, HEAD_PAIRS, LANES, LANES), F32)],
        scratch_shapes=scratch,
        input_output_aliases=aliases,
        compiler_params=_cparams(3),
        name="rwkv_scan",
    )(*args)
    st = st.reshape(batch, HEAD_PAIRS, 2, HEAD_DIM, 2, HEAD_DIM)
    wkv = jnp.stack([st[:, :, 0, :, 0, :], st[:, :, 1, :, 1, :]], axis=2)
    return o_c, jnp.swapaxes(wkv, -1, -2).reshape(batch, RW_HEADS, HEAD_DIM, HEAD_DIM)


def _outproj_kernel(oa_ref, ob_ref, oc_ref, x_ref, w_ref, g_ref, b_ref, y_ref):
    mix = _dot(oa_ref[...].astype(BF16), w_ref[0:SB_WIDTH, :])
    mix += _dot(ob_ref[...].astype(BF16), w_ref[SB_WIDTH:SB_WIDTH + POOL_WIDTH, :])
    mix += _dot(oc_ref[...].astype(BF16), w_ref[SB_WIDTH + POOL_WIDTH:, :])
    y_ref[...] = _layer_norm(DN_ALPHA * x_ref[...] + mix, g_ref[...], b_ref[...])


def _outproj_ln(oa, ob, oc, x, w_bf, g, b, tm):
    m, d = x.shape
    row = lambda i: (i, 0)
    const = lambda i: (0, 0)
    return pl.pallas_call(
        _outproj_kernel,
        grid=(m // tm,),
        in_specs=[pl.BlockSpec((tm, SB_WIDTH), row), pl.BlockSpec((tm, POOL_WIDTH), row),
                  pl.BlockSpec((tm, RW_WIDTH), row), pl.BlockSpec((tm, d), row),
                  pl.BlockSpec(w_bf.shape, const), pl.BlockSpec((1, d), const),
                  pl.BlockSpec((1, d), const)],
        out_specs=pl.BlockSpec((tm, d), row),
        out_shape=jax.ShapeDtypeStruct((m, d), F32),
        compiler_params=_cparams(1),
        name="outproj_ln",
    )(oa, ob, oc, x, w_bf, g, b)


def _router_kernel(x_ref, w_ref, b_ref, idx_ref, gate_ref, rank_ref, cnt_ref, run_ref, *, tm):
    @pl.when(pl.program_id(0) == 0)
    def _():
        run_ref[...] = jnp.zeros_like(run_ref)

    logits = _dot(x_ref[...].astype(BF16), w_ref[...].astype(BF16)) + b_ref[...]
    lane = lax.broadcasted_iota(I32, (tm, N_EXPERTS), 1).astype(F32)
    work = logits
    vals, idxs = [], []
    for _ in range(TOP_K):
        top = jnp.max(work, axis=1, keepdims=True)
        idx = jnp.min(jnp.where(work == top, lane, float(N_EXPERTS)), axis=1, keepdims=True)
        vals.append(top)
        idxs.append(idx)
        work = jnp.where(lane == idx, -jnp.inf, work)
    exps = [jnp.exp(v - vals[0]) for v in vals]
    total = exps[0] + exps[1] + exps[2] + exps[3]
    out_lane = lax.broadcasted_iota(I32, (tm, LANES), 1)
    lane_f = out_lane.astype(F32)
    onehots = [(lane_f == idxs[k]).astype(F32) for k in range(TOP_K)]
    chosen = onehots[0] + onehots[1] + onehots[2] + onehots[3]
    earlier = (lax.broadcasted_iota(I32, (tm, tm), 1) < lax.broadcasted_iota(I32, (tm, tm), 0)).astype(BF16)
    base = run_ref[...] + _dot(earlier, chosen.astype(BF16))
    run_ref[...] = run_ref[...] + jnp.sum(chosen, axis=0, keepdims=True)
    cnt_ref[...] = run_ref[...].astype(I32)
    idx_out = jnp.zeros((tm, LANES), F32)
    gate_out = jnp.zeros((tm, LANES), F32)
    rank_out = jnp.zeros((tm, LANES), F32)
    for k in range(TOP_K):
        idx_out = jnp.where(out_lane == k, idxs[k], idx_out)
        gate_out = jnp.where(out_lane == k, exps[k] / total, gate_out)
        rank_out = jnp.where(out_lane == k, jnp.sum(onehots[k] * base, axis=1, keepdims=True), rank_out)
    idx_ref[...] = idx_out.astype(I32)
    gate_ref[...] = gate_out
    rank_ref[...] = rank_out.astype(I32)


def _router(x, w, b, tm):
    m, d = x.shape
    return pl.pallas_call(
        functools.partial(_router_kernel, tm=tm),
        grid=(m // tm,),
        in_specs=[pl.BlockSpec((tm, d), lambda i: (i, 0)),
                  pl.BlockSpec((d, N_EXPERTS), lambda i: (0, 0)),
                  pl.BlockSpec((1, N_EXPERTS), lambda i: (0, 0))],
        out_specs=[pl.BlockSpec((tm, LANES), lambda i: (i, 0))] * 3 + [pl.BlockSpec((1, LANES), lambda i: (0, 0))],
        out_shape=[jax.ShapeDtypeStruct((m, LANES), I32), jax.ShapeDtypeStruct((m, LANES), F32),
                   jax.ShapeDtypeStruct((m, LANES), I32), jax.ShapeDtypeStruct((1, LANES), I32)],
        scratch_shapes=[pltpu.VMEM((1, LANES), F32)],
        compiler_params=_cparams(1),
        name="router",
    )(x, w, b)


def _row_copy(src_hbm, dst_ref, sem, src_row, dst_row):
    return pltpu.make_async_copy(src_hbm.at[pl.ds(src_row, 1), :], dst_ref.at[pl.ds(dst_row, 1), :], sem)


GATHER_UNROLL = 8


def _gather_kernel(tok_ref, tok_next_ref, nu_ref, x_hbm, o_ref, rows_ref, sem, *, bm):
    i = pl.program_id(0)
    n_used = nu_ref[0]

    def start_block(toks, slot):
        def body(k, carry):
            for u in range(GATHER_UNROLL):
                r = k * GATHER_UNROLL + u
                _row_copy(x_hbm, rows_ref.at[slot], sem.at[slot], toks[0, 0, r], r).start()
            return carry

        lax.fori_loop(0, bm // GATHER_UNROLL, body, 0)

    @pl.when(i == 0)
    def _():
        start_block(tok_ref, 0)

    @pl.when(i + 1 < n_used)
    def _():
        start_block(tok_next_ref, (i + 1) % 2)

    @pl.when(i < n_used)
    def _():
        slot = i % 2

        def wait(k, carry):
            for u in range(GATHER_UNROLL):
                _row_copy(x_hbm, rows_ref.at[slot], sem.at[slot], 0, k * GATHER_UNROLL + u).wait()
            return carry

        lax.fori_loop(0, bm // GATHER_UNROLL, wait, 0)
        o_ref[...] = rows_ref[slot].astype(BF16)


def _gather_rows(x, buf_tok, n_used, bm):
    d = x.shape[1]
    nb = buf_tok.shape[0] // bm
    toks = buf_tok.reshape(nb, 1, bm)
    return pl.pallas_call(
        functools.partial(_gather_kernel, bm=bm),
        grid=(nb,),
        in_specs=[pl.BlockSpec((1, 1, bm), lambda i: (i, 0, 0), memory_space=pltpu.SMEM),
                  pl.BlockSpec((1, 1, bm), lambda i: (jnp.minimum(i + 1, nb - 1), 0, 0),
                               memory_space=pltpu.SMEM),
                  pl.BlockSpec(memory_space=pltpu.SMEM),
                  pl.BlockSpec(memory_space=pl.ANY)],
        out_specs=pl.BlockSpec((bm, d), lambda i: (i, 0)),
        out_shape=jax.ShapeDtypeStruct((nb * bm, d), BF16),
        scratch_shapes=[pltpu.VMEM((2, bm, d), F32), pltpu.SemaphoreType.DMA((2,))],
        compiler_params=_cparams(1),
        name="moe_gather",
    )(toks, toks, n_used, x)


def _step_table(block_e, n_used):
    nb = block_e.shape[0]
    i = jnp.arange(nb, dtype=I32)
    valid = i < n_used[0]
    first = jnp.logical_and(valid, jnp.logical_or(i == 0, block_e != jnp.roll(block_e, 1)))
    first_pos = jnp.where(first, i, nb)
    next_pos = jnp.concatenate([lax.cummin(first_pos, reverse=True)[1:], jnp.full((1,), nb, I32)])
    next_e = jnp.where(next_pos < nb, block_e[jnp.minimum(next_pos, nb - 1)], -1)
    return jnp.stack([block_e, first.astype(I32), next_e.astype(I32), jnp.zeros_like(block_e)],
                     axis=1).reshape(nb * 4)


def _weight_copies(w_hbm, wbuf, sem, layer, expert, cols, tn):
    return [pltpu.make_async_copy(w_hbm.at[layer, expert, :, pl.ds(pl.multiple_of(col, tn), tn)],
                                  wbuf.at[h], sem.at[h]) for h, col in enumerate(cols)]


def _fetch_expert_tile(tbl_ref, w_hbm, wbuf, wbf, sem, layer, cols, tn):
    i = pl.program_id(1)
    expert = tbl_ref[4 * i]
    next_expert = tbl_ref[4 * i + 2]

    @pl.when(tbl_ref[4 * i + 1] == 1)
    def _():
        @pl.when(i == 0)
        def _():
            for cp in _weight_copies(w_hbm, wbuf, sem, layer, expert, cols, tn):
                cp.start()

        for cp in _weight_copies(w_hbm, wbuf, sem, layer, expert, cols, tn):
            cp.wait()
        for h in range(len(cols)):
            wbf[h] = wbuf[h].astype(BF16)

        @pl.when(next_expert >= 0)
        def _():
            for cp in _weight_copies(w_hbm, wbuf, sem, layer, next_expert, cols, tn):
                cp.start()


def _expert_up_kernel(tbl_ref, nu_ref, x_ref, w_hbm, bg_ref, bl_ref, act_ref, wbuf, wbf, sem, *, layer, tn):
    j = pl.program_id(0)
    nj = pl.num_programs(0)

    @pl.when(pl.program_id(1) < nu_ref[0])
    def _():
        _fetch_expert_tile(tbl_ref, w_hbm, wbuf, wbf, sem, layer, (j * tn, (nj + j) * tn), tn)
        x = x_ref[...]
        h_glu = _dot(x, wbf[0]) + bg_ref[...]
        h_lin = _dot(x, wbf[1]) + bl_ref[...]
        h_glu = jnp.minimum(h_glu, SWIGLU_LIMIT)
        h_lin = jnp.clip(h_lin, -SWIGLU_LIMIT, SWIGLU_LIMIT)
        act = h_glu * jax.nn.sigmoid(SWIGLU_ALPHA * h_glu) * (h_lin + 1.0)
        act_ref[...] = act.astype(BF16)


def _expert_up(xb, w1, b1, layer, tbl, n_used, bm, tn):
    rows, d = xb.shape
    de = w1.shape[3] // 2
    nj = de // tn
    nb = rows // bm

    def blk(i, nu):
        return jnp.minimum(i, nu[0] - 1)

    grid_spec = pltpu.PrefetchScalarGridSpec(
        num_scalar_prefetch=2,
        grid=(nj, nb),
        in_specs=[pl.BlockSpec((bm, d), lambda j, i, tb, nu: (blk(i, nu), 0)),
                  pl.BlockSpec(memory_space=pl.ANY),
                  pl.BlockSpec((None, None, 1, tn), lambda j, i, tb, nu: (layer, tb[4 * blk(i, nu)], 0, j)),
                  pl.BlockSpec((None, None, 1, tn),
                               lambda j, i, tb, nu: (layer, tb[4 * blk(i, nu)], 0, nj + j))],
        out_specs=pl.BlockSpec((bm, tn), lambda j, i, tb, nu: (blk(i, nu), j)),
        scratch_shapes=[pltpu.VMEM((2, d, tn), F32), pltpu.VMEM((2, d, tn), BF16),
                        pltpu.SemaphoreType.DMA((2,))],
    )
    return pl.pallas_call(
        functools.partial(_expert_up_kernel, layer=layer, tn=tn),
        grid_spec=grid_spec,
        out_shape=jax.ShapeDtypeStruct((rows, de), BF16),
        compiler_params=_cparams(2),
        name="moe_up",
    )(tbl, n_used, xb, w1, b1, b1)


def _expert_down_kernel(tbl_ref, nu_ref, a_ref, w_hbm, b_ref, y_ref, wbuf, wbf, sem, *, layer, tn):
    j = pl.program_id(0)

    @pl.when(pl.program_id(1) < nu_ref[0])
    def _():
        _fetch_expert_tile(tbl_ref, w_hbm, wbuf, wbf, sem, layer, (j * tn,), tn)
        y_ref[...] = _dot(a_ref[...], wbf[0]) + b_ref[...]


def _expert_down(act, w2, b2, layer, tbl, n_used, bm, tn):
    rows, de = act.shape
    d = w2.shape[3]
    nj = d // tn
    nb = rows // bm

    def blk(i, nu):
        return jnp.minimum(i, nu[0] - 1)

    grid_spec = pltpu.PrefetchScalarGridSpec(
        num_scalar_prefetch=2,
        grid=(nj, nb),
        in_specs=[pl.BlockSpec((bm, de), lambda j, i, tb, nu: (blk(i, nu), 0)),
                  pl.BlockSpec(memory_space=pl.ANY),
                  pl.BlockSpec((None, None, 1, tn), lambda j, i, tb, nu: (layer, tb[4 * blk(i, nu)], 0, j))],
        out_specs=pl.BlockSpec((bm, tn), lambda j, i, tb, nu: (blk(i, nu), j)),
        scratch_shapes=[pltpu.VMEM((1, de, tn), F32), pltpu.VMEM((1, de, tn), BF16),
                        pltpu.SemaphoreType.DMA((1,))],
    )
    return pl.pallas_call(
        functools.partial(_expert_down_kernel, layer=layer, tn=tn),
        grid_spec=grid_spec,
        out_shape=jax.ShapeDtypeStruct((rows, d), F32),
        compiler_params=_cparams(2),
        name="moe_down",
    )(tbl, n_used, act, w2, b2)


def _combine_kernel(dest_ref, dest_next_ref, y_hbm, gate_ref, x_ref, g_ref, b_ref, o_ref, rows_ref, sem,
                    *, tm):
    i = pl.program_id(0)

    def start_block(dests, slot):
        def body(r, carry):
            for k in range(TOP_K):
                _row_copy(y_hbm, rows_ref.at[slot, k], sem.at[slot], dests[0, 0, r * TOP_K + k], r).start()
            return carry

        lax.fori_loop(0, tm, body, 0)

    @pl.when(i == 0)
    def _():
        start_block(dest_ref, 0)

    @pl.when(i + 1 < pl.num_programs(0))
    def _():
        start_block(dest_next_ref, (i + 1) % 2)

    slot = i % 2

    def wait(r, carry):
        for k in range(TOP_K):
            _row_copy(y_hbm, rows_ref.at[slot, k], sem.at[slot], 0, r).wait()
        return carry

    lax.fori_loop(0, tm, wait, 0)
    gates = gate_ref[...]
    ffn = gates[:, 0:1] * rows_ref[slot, 0]
    for k in range(1, TOP_K):
        ffn += gates[:, k:k + 1] * rows_ref[slot, k]
    o_ref[...] = _layer_norm(DN_ALPHA * x_ref[...] + ffn, g_ref[...], b_ref[...])


def _combine_ln(y, dest, gates, x, g, b, tm):
    m, d = x.shape
    nb = m // tm
    dests = dest.reshape(nb, 1, tm * TOP_K)
    return pl.pallas_call(
        functools.partial(_combine_kernel, tm=tm),
        grid=(nb,),
        in_specs=[pl.BlockSpec((1, 1, tm * TOP_K), lambda i: (i, 0, 0), memory_space=pltpu.SMEM),
                  pl.BlockSpec((1, 1, tm * TOP_K), lambda i: (jnp.minimum(i + 1, nb - 1), 0, 0),
                               memory_space=pltpu.SMEM),
                  pl.BlockSpec(memory_space=pl.ANY),
                  pl.BlockSpec((tm, LANES), lambda i: (i, 0)),
                  pl.BlockSpec((tm, d), lambda i: (i, 0)),
                  pl.BlockSpec((1, d), lambda i: (0, 0)),
                  pl.BlockSpec((1, d), lambda i: (0, 0))],
        out_specs=pl.BlockSpec((tm, d), lambda i: (i, 0)),
        out_shape=jax.ShapeDtypeStruct((m, d), F32),
        scratch_shapes=[pltpu.VMEM((2, TOP_K, tm, d), F32), pltpu.SemaphoreType.DMA((2,))],
        compiler_params=_cparams(1),
        name="moe_combine_ln",
    )(dests, dests, y, gates, x, g, b)


def _routing_tables(top_i, rank, counts, bm):
    n = top_i.shape[0]
    nk = n * TOP_K
    padded = (counts + bm - 1) // bm * bm
    pad_end = jnp.cumsum(padded)
    pad_start = pad_end - padded
    dest = (pad_start[top_i] + rank).astype(I32).reshape(nk)
    nb = (nk + N_EXPERTS * (bm - 1)) // bm
    buf_tok = jnp.zeros((nb * bm,), I32).at[dest].set(jnp.arange(nk, dtype=I32) // TOP_K)
    block_row = jnp.arange(nb, dtype=I32)[:, None] * bm
    block_e = jnp.minimum(jnp.sum((pad_end[None, :] <= block_row).astype(I32), axis=1), N_EXPERTS - 1)
    n_used = (pad_end[-1:] // bm).astype(I32)
    return dest, buf_tok, _step_table(block_e, n_used), n_used


def _moe_ln(x, lp, big, layer):
    idx, gates, rank, counts = _router(x, lp['w_router'], lp['b_router'].reshape(1, N_EXPERTS), tm=256)
    dest, buf_tok, steps, n_used = _routing_tables(idx[:, :TOP_K], rank[:, :TOP_K],
                                                     counts[0, :N_EXPERTS], MOE_BM)
    xb = _gather_rows(x, buf_tok, n_used, MOE_BM)
    nl, e, d, de2 = big['w_e1'].shape
    act = _expert_up(xb, big['w_e1'], big['b_e1'].reshape(nl, e, 1, de2), layer, steps, n_used,
                     MOE_BM, tn=1024)
    y = _expert_down(act, big['w_e2'], big['b_e2'].reshape(nl, e, 1, d), layer, steps, n_used,
                     MOE_BM, tn=1024)
    return _combine_ln(y, dest, gates, x, lp['ln_ffn_g'].reshape(1, d), lp['ln_ffn_b'].reshape(1, d),
                       tm=128)


def _layer(x, lp, big, layer, st, n_p, bp, tp, bs, ts):
    d = x.shape[1]
    h = _inproj(x, big['w_in'], layer, tm=768, tn=768)
    qkv_w = 3 * SB_WIDTH

    o_a = _sb_attention(h, 0, bp, tp, 256, None, None)
    hist = (st['cache_k'].reshape(bs, -1, SB_WIDTH), st['cache_v'].reshape(bs, -1, SB_WIDTH))
    o_a = _sb_attention(h, n_p, bs, ts, ts, hist, o_a)
    past = hist[0].shape[1]

    w_pool = lp['pool_w']
    scale = lp['pool_scale'].reshape(1, POOL_WIDTH)
    o_b = _pool_mixer(h, 0, bp, tp, 512, jnp.zeros((bp, 16, POOL_WIDTH), F32), w_pool, scale, 0, None)
    hist16 = jnp.pad(st['state_pool'], ((0, 0), (1, 0), (0, 0)))
    o_b = _pool_mixer(h, n_p, bs, ts, ts, hist16, w_pool, scale, past, o_b)

    zero_lora = jnp.zeros((RW_LORA // 2, RW_WIDTH), F32)
    w2p = jnp.concatenate([lp['rw_w2'], zero_lora], axis=0)
    a2p = jnp.concatenate([zero_lora, lp['rw_a2']], axis=0)
    pre_args = (lp['rw_mu'].reshape(1, RW_COLS), lp['rw_w0'].reshape(1, RW_WIDTH), w2p,
                lp['rw_a0'].reshape(1, RW_WIDTH), a2p, lp['rw_g2'])
    pre = _rw_pre(h, 0, bp, tp, 256, jnp.zeros((bp, 1, RW_COLS), F32), *pre_args, None)
    rkv, lw, a, g = _rw_pre(h, n_p, bs, ts, ts, st['state_shift'], *pre_args, pre)
    prm = tuple(lp[k].reshape(1, RW_WIDTH) for k in ('rw_kk', 'rw_ka', 'rw_rk', 'rw_gn_g', 'rw_gn_b'))
    o_c, wkv_p = _rw_scan(rkv, lw, a, g, 0, bp, tp, 512, 64, prm,
                          jnp.zeros((bp, RW_HEADS, HEAD_DIM, HEAD_DIM), F32), None)
    o_c, wkv_s = _rw_scan(rkv, lw, a, g, n_p, bs, ts, ts, ts, prm, st['state_wkv'], o_c)

    x1 = _outproj_ln(o_a, o_b, o_c, x, lp['w_out'].astype(BF16), lp['ln_mix_g'].reshape(1, d),
                     lp['ln_mix_b'].reshape(1, d), tm=256)
    x2 = _moe_ln(x1, lp, big, layer)

    def states(rows, b, t, wkv):
        hb = rows.reshape(b, t, -1)
        k_new = hb[:, :, SB_WIDTH:2 * SB_WIDTH].reshape(b, t, SB_HEADS, HEAD_DIM)
        v_new = hb[:, :, 2 * SB_WIDTH:qkv_w].reshape(b, t, SB_HEADS, HEAD_DIM)
        pool_new = hb[:, t - POOL_HIST:, qkv_w:qkv_w + POOL_WIDTH]
        shift_new = hb[:, t - 1:, qkv_w + POOL_WIDTH:]
        return k_new, v_new, pool_new, shift_new, wkv

    return x2, states(h[:n_p], bp, tp, wkv_p), states(h[n_p:], bs, ts, wkv_s)


def kernel(x_prompt, x_sample, cache_k, cache_v, state_pool, state_shift, state_wkv, w_in, w_out, ln_mix_g, ln_mix_b, ln_ffn_g, ln_ffn_b, pool_w, pool_scale, rw_mu, rw_w0, rw_w2, rw_a0, rw_a2, rw_g2, rw_kk, rw_ka, rw_rk, rw_gn_g, rw_gn_b, w_router, b_router, w_e1, b_e1, w_e2, b_e2):
    bp, tp, d = x_prompt.shape
    bs, ts, _ = x_sample.shape
    n_p = bp * tp
    x = jnp.concatenate([x_prompt.reshape(n_p, d), x_sample.reshape(bs * ts, d)], axis=0)
    big = dict(w_in=w_in, w_e1=w_e1, b_e1=b_e1, w_e2=w_e2, b_e2=b_e2)
    weights = dict(w_out=w_out, ln_mix_g=ln_mix_g, ln_mix_b=ln_mix_b, ln_ffn_g=ln_ffn_g,
                   ln_ffn_b=ln_ffn_b, pool_w=pool_w, pool_scale=pool_scale, rw_mu=rw_mu, rw_w0=rw_w0,
                   rw_w2=rw_w2, rw_a0=rw_a0, rw_a2=rw_a2, rw_g2=rw_g2, rw_kk=rw_kk, rw_ka=rw_ka,
                   rw_rk=rw_rk, rw_gn_g=rw_gn_g, rw_gn_b=rw_gn_b, w_router=w_router,
                   b_router=b_router)
    st_p, st_s = [], []
    for l in range(w_in.shape[0]):
        lp = {k: v[l] for k, v in weights.items()}
        st = dict(cache_k=cache_k[l], cache_v=cache_v[l], state_pool=state_pool[l],
                  state_shift=state_shift[l], state_wkv=state_wkv[l])
        x, sp, ss = _layer(x, lp, big, l, st, n_p, bp, tp, bs, ts)
        st_p.append(sp)
        st_s.append(ss)
    y_prompt = x[:n_p].reshape(bp, tp, d)
    y_sample = x[n_p:].reshape(bs, ts, d)
    k_p, v_p, pool_p, shift_p, wkv_p = (jnp.stack([s[i] for s in st_p]) for i in range(5))
    k_s, v_s, pool_s, shift_s, wkv_s = (jnp.stack([s[i] for s in st_s]) for i in range(5))
    return (y_prompt, y_sample, k_p, v_p, pool_p, shift_p, wkv_p, k_s, v_s, pool_s, shift_s, wkv_s)
```

```python
import functools

import jax
import jax.numpy as jnp
from jax import lax
from jax.experimental import pallas as pl
from jax.experimental.pallas import tpu as pltpu

F32 = jnp.float32
BF16 = jnp.bfloat16
I32 = jnp.int32

HEAD_DIM = 64
SB_HEADS = 12
SB_WIDTH = SB_HEADS * HEAD_DIM
POOL_WINDOWS = (2, 4, 8, 16)
POOL_GC = 128
POOL_WIDTH = POOL_GC * len(POOL_WINDOWS)
POOL_HIST = max(POOL_WINDOWS) - 1
RW_HEADS = 12
RW_WIDTH = RW_HEADS * HEAD_DIM
RW_LORA = 128
RW_GATE = 128
RW_COLS = 3 * RW_WIDTH + RW_LORA + RW_GATE
N_EXPERTS = 32
TOP_K = 4
SWIGLU_LIMIT = 7.0
SWIGLU_ALPHA = 1.702
DEPTH = 2
DN_ALPHA = (2.0 * DEPTH) ** 0.25
LN_EPS = 1e-5
GN_EPS = 64e-5

LANES = 128
HEAD_PAIRS = SB_WIDTH // LANES
VMEM_LIMIT = 56 * 1024 * 1024
ATT_LOG_FLOOR = -110.0

POOL_COL0 = 3 * SB_WIDTH
RW_COL0 = POOL_COL0 + POOL_WIDTH
RW_COL_TILE = 256
RW_COL_TILES = RW_COLS // RW_COL_TILE
MOE_BM = 256
RW_TILES = 2
RW_PREP_GROUP = 8


def _cparams(n_axes):
    return pltpu.CompilerParams(dimension_semantics=("arbitrary",) * n_axes,
                                vmem_limit_bytes=VMEM_LIMIT)


def _dot(a, b, precision=None):
    return jnp.dot(a, b, preferred_element_type=F32, precision=precision)


def _dot_nt(a, b, precision=None):
    return lax.dot_general(a, b, (((1,), (1,)), ((), ())), preferred_element_type=F32,
                           precision=precision)


def _dot_tn(a, b, precision=None):
    return lax.dot_general(a, b, (((0,), (0,)), ((), ())), preferred_element_type=F32,
                           precision=precision)


def _layer_norm(x, g, b):
    mu = jnp.mean(x, axis=-1, keepdims=True)
    xc = x - mu
    var = jnp.mean(xc * xc, axis=-1, keepdims=True)
    return xc * lax.rsqrt(var + LN_EPS) * g + b


def _inproj_kernel(x_ref, w_ref, o_ref, wbf_ref):
    @pl.when(pl.program_id(1) == 0)
    def _():
        wbf_ref[...] = w_ref[...].astype(BF16)

    o_ref[...] = _dot(x_ref[...].astype(BF16), wbf_ref[...])


def _inproj(x, w, layer, tm, tn):
    m, k = x.shape
    n = w.shape[2]
    return pl.pallas_call(
        _inproj_kernel,
        grid=(n // tn, m // tm),
        in_specs=[pl.BlockSpec((tm, k), lambda j, i: (i, 0)),
                  pl.BlockSpec((None, k, tn), lambda j, i: (layer, 0, j))],
        out_specs=pl.BlockSpec((tm, tn), lambda j, i: (i, j)),
        out_shape=jax.ShapeDtypeStruct((m, n), F32),
        scratch_shapes=[pltpu.VMEM((k, tn), BF16)],
        compiler_params=_cparams(2),
        name="inproj",
    )(x, w)


def _split2(a):
    hi = a.astype(BF16)
    lo = (a - hi.astype(F32)).astype(BF16)
    return hi, lo


def _suffix_matrix(n):
    return (lax.broadcasted_iota(I32, (n, n), 0) > lax.broadcasted_iota(I32, (n, n), 1)).astype(BF16)


def _sb_block(qhs, k_bf, v_bf, suffix, carries, mask):
    heads = range(2)
    z = [_dot_nt(qhs[h], k_bf) for h in heads]
    sp = [jnp.maximum(z_, 0.0) + jnp.log1p(jnp.exp(-jnp.abs(z_))) for z_ in z]
    log_stay = [-s_ if mask is None else jnp.where(mask, -s_, 0.0) for s_ in sp]
    parts = [_split2(l_) for l_ in log_stay]
    hi = [_dot(parts[h][0], suffix) for h in heads]
    lo = [_dot(parts[h][1], suffix) for h in heads]
    after = [hi[h] + lo[h] + carries[h] for h in heads]
    att = [jnp.exp(z[h] - sp[h] + after[h]) for h in heads]
    if mask is not None:
        att = [jnp.where(mask, a_, 0.0) for a_ in att]
    pv = [_dot(att[h].astype(BF16), v_bf) for h in heads]
    return pv, [after[h][:, 0:1] + log_stay[h][:, 0:1] for h in heads]


def _sb_live(carries):
    return (jnp.max(jnp.maximum(carries[0], carries[1])) > ATT_LOG_FLOOR).astype(I32)


def _sb_sweep(qhs, k_ref, v_ref, tk, n_blocks, suffix, live, accs, carries):
    def cond(s):
        return jnp.logical_and(s[0] < n_blocks, s[1] > 0)

    def body(s):
        step, _, accs, carries = s
        rows = pl.ds(pl.multiple_of((n_blocks - 1 - step) * tk, tk), tk)
        k_bf = k_ref[rows, :].astype(BF16)
        v_bf = v_ref[rows, :].astype(BF16)
        pv, carries = _sb_block(qhs, k_bf, v_bf, suffix, carries, None)
        accs = tuple(accs[h] + pv[h] for h in range(2))
        return step + 1, _sb_live(carries), accs, tuple(carries)

    _, live, accs, carries = lax.while_loop(cond, body, (jnp.int32(0), live, accs, carries))
    return live, accs, carries


def _sb_kernel(*refs, tq, th, past):
    if past:
        q_ref, kn_ref, vn_ref, kh_ref, vh_ref, o_ref, kbf_ref, vbf_ref = refs
    else:
        q_ref, kn_ref, vn_ref, o_ref, kbf_ref, vbf_ref = refs
    qi = pl.program_id(2)

    @pl.when(qi == 0)
    def _():
        kbf_ref[...] = kn_ref[...].astype(BF16)
        vbf_ref[...] = vn_ref[...].astype(BF16)

    q = q_ref[...] * (HEAD_DIM ** -0.5)
    lane = lax.broadcasted_iota(I32, (tq, LANES), 1)
    qhs = [jnp.where(lane < HEAD_DIM, q, 0.0).astype(BF16),
           jnp.where(lane >= HEAD_DIM, q, 0.0).astype(BF16)]
    suffix_new = _suffix_matrix(tq)
    diag_mask = lax.broadcasted_iota(I32, (tq, tq), 1) < lax.broadcasted_iota(I32, (tq, tq), 0)
    rows = pl.ds(pl.multiple_of(qi * tq, tq), tq)
    k_bf = kbf_ref[rows, :]
    v_bf = vbf_ref[rows, :]
    accs, carries = _sb_block(qhs, k_bf, v_bf, suffix_new, [jnp.zeros((tq, 1), F32)] * 2, diag_mask)
    accs, carries = tuple(accs), tuple(carries)
    live, accs, carries = _sb_sweep(qhs, kbf_ref, vbf_ref, tq, qi, suffix_new, _sb_live(carries),
                                    accs, carries)
    if past:
        _, accs, carries = _sb_sweep(qhs, kh_ref, vh_ref, th, past // th, _suffix_matrix(th), live,
                                     accs, carries)
    o_ref[...] = jnp.where(lane < HEAD_DIM, accs[0], accs[1])


def _sb_attention(h_all, row0, batch, t, tq, hist, prev_out):
    m = h_all.shape[0]
    nq = t // tq
    rb0 = row0 // tq
    sb0 = row0 // t
    in_specs = [pl.BlockSpec((tq, LANES), lambda b, p, i: (rb0 + b * nq + i, p)),
                pl.BlockSpec((t, LANES), lambda b, p, i: (sb0 + b, HEAD_PAIRS + p)),
                pl.BlockSpec((t, LANES), lambda b, p, i: (sb0 + b, 2 * HEAD_PAIRS + p))]
    args = [h_all, h_all, h_all]
    past, th = 0, 0
    if hist is not None:
        past = hist[0].shape[1]
        th = min(past, 512)
        in_specs += [pl.BlockSpec((None, past, LANES), lambda b, p, i: (b, 0, p))] * 2
        args += list(hist)
    aliases = {}
    if prev_out is not None:
        in_specs.append(pl.BlockSpec(memory_space=pl.ANY))
        args.append(prev_out)
        aliases = {len(args) - 1: 0}
    kern = functools.partial(_sb_kernel, tq=tq, th=th, past=past)
    if prev_out is not None:
        kern = _drop_last_input(kern, n_in=len(args))
    return pl.pallas_call(
        kern,
        grid=(batch, HEAD_PAIRS, nq),
        in_specs=in_specs,
        out_specs=pl.BlockSpec((tq, LANES), lambda b, p, i: (rb0 + b * nq + i, p)),
        out_shape=jax.ShapeDtypeStruct((m, SB_WIDTH), F32),
        scratch_shapes=[pltpu.VMEM((t, LANES), BF16)] * 2,
        input_output_aliases=aliases,
        compiler_params=_cparams(3),
        name="sb_attention",
    )(*args)


def _drop_last_input(kern, n_in):
    def wrapped(*refs):
        return kern(*refs[:n_in - 1], *refs[n_in:])
    return wrapped


def _pool_kernel(u0_ref, u1_ref, u2_ref, u3_ref, hist_ref, w_ref, scale_ref, o_ref, ext_ref, *, tt, pos0):
    ti = pl.program_id(1)
    halo = 16

    @pl.when(ti == 0)
    def _():
        ext_ref[0:halo, :] = hist_ref[...]

    @pl.when(ti > 0)
    def _():
        ext_ref[0:halo, :] = ext_ref[tt:tt + halo, :]

    u = jnp.concatenate([u0_ref[...], u1_ref[...], u2_ref[...], u3_ref[...]], axis=1)
    ext_ref[halo:halo + tt, :] = u
    pos = pos0 + ti * tt + lax.broadcasted_iota(I32, (tt, 1), 0)
    for g, w in enumerate(POOL_WINDOWS):
        cols = slice(g * POOL_GC, (g + 1) * POOL_GC)
        s = u[:, cols]
        for k in range(1, w):
            s = s + ext_ref[halo - k:halo - k + tt, cols]
        cnt = jnp.minimum(pos + 1, w).astype(F32)
        pooled = s / cnt - u[:, cols]
        y = _dot(pooled.astype(BF16), w_ref[g].astype(BF16))
        o_ref[:, cols] = y * scale_ref[:, cols]


def _pool_mixer(h_all, row0, batch, t, tt, hist16, w_pool, scale, pos0, prev_out):
    m = h_all.shape[0]
    nt = t // tt
    rb0 = row0 // tt
    tile0 = POOL_COL0 // POOL_GC
    in_specs = [pl.BlockSpec((tt, POOL_GC), lambda b, i, g=g: (rb0 + b * nt + i, tile0 + g))
                for g in range(len(POOL_WINDOWS))]
    in_specs += [pl.BlockSpec((None, 16, POOL_WIDTH), lambda b, i: (b, 0, 0)),
                pl.BlockSpec((len(POOL_WINDOWS), POOL_GC, POOL_GC), lambda b, i: (0, 0, 0)),
                pl.BlockSpec((1, POOL_WIDTH), lambda b, i: (0, 0))]
    args = [h_all] * len(POOL_WINDOWS) + [hist16, w_pool, scale]
    aliases = {}
    kern = functools.partial(_pool_kernel, tt=tt, pos0=pos0)
    if prev_out is not None:
        in_specs.append(pl.BlockSpec(memory_space=pl.ANY))
        args.append(prev_out)
        aliases = {len(args) - 1: 0}
        kern = _drop_last_input(kern, n_in=len(args))
    return pl.pallas_call(
        kern,
        grid=(batch, nt),
        in_specs=in_specs,
        out_specs=pl.BlockSpec((tt, POOL_WIDTH), lambda b, i: (rb0 + b * nt + i, 0)),
        out_shape=jax.ShapeDtypeStruct((m, POOL_WIDTH), F32),
        scratch_shapes=[pltpu.VMEM((tt + 16, POOL_WIDTH), F32)],
        input_output_aliases=aliases,
        compiler_params=_cparams(2),
        name="pool_mixer",
    )(*args)


def _rw_pre_kernel(*refs, tt):
    c_refs = refs[:RW_COL_TILES]
    (hist_ref, mu_ref, w0_ref, w2_ref, a0_ref, a2_ref, g2_ref,
     rkv_ref, lw_ref, a_ref, g_ref, last_ref) = refs[RW_COL_TILES:]
    ti = pl.program_id(1)

    @pl.when(ti == 0)
    def _():
        last_ref[...] = hist_ref[...]

    c = jnp.concatenate([ref[...] for ref in c_refs], axis=1)
    row = lax.broadcasted_iota(I32, (tt, 1), 0)
    prev = jnp.where(row == 0, last_ref[...], pltpu.roll(c, 1, 0))
    last_ref[...] = c[tt - 1:tt, :]
    xs = c + (prev - c) * mu_ref[...]
    rkv_ref[...] = xs[:, :3 * RW_WIDTH]
    lora = xs[:, 3 * RW_WIDTH:3 * RW_WIDTH + RW_LORA]
    wd = _dot(jnp.tanh(lora).astype(BF16), w2_ref[...].astype(BF16))
    sp = jax.nn.softplus(-(w0_ref[...] + wd))
    lw_ref[...] = -jnp.exp(-sp - 0.5)
    a_ref[...] = jax.nn.sigmoid(a0_ref[...] + _dot(lora.astype(BF16), a2_ref[...].astype(BF16)))
    gd = xs[:, 3 * RW_WIDTH + RW_LORA:]
    g_ref[...] = _dot(jax.nn.sigmoid(gd).astype(BF16), g2_ref[...].astype(BF16))


def _rw_pre(h_all, row0, batch, t, tt, shift_hist, mu, w0, w2p, a0, a2p, g2, prev_outs):
    m = h_all.shape[0]
    nt = t // tt
    rb0 = row0 // tt
    row_map = lambda b, i: (rb0 + b * nt + i, 0)
    const = lambda b, i: (0, 0)
    tile0 = RW_COL0 // RW_COL_TILE
    in_specs = [pl.BlockSpec((tt, RW_COL_TILE), lambda b, i, j=j: (rb0 + b * nt + i, tile0 + j))
                for j in range(RW_COL_TILES)]
    in_specs += [pl.BlockSpec((None, 1, RW_COLS), lambda b, i: (b, 0, 0)),
                pl.BlockSpec((1, RW_COLS), const),
                pl.BlockSpec((1, RW_WIDTH), const),
                pl.BlockSpec((RW_LORA, RW_WIDTH), const),
                pl.BlockSpec((1, RW_WIDTH), const),
                pl.BlockSpec((RW_LORA, RW_WIDTH), const),
                pl.BlockSpec((RW_GATE, RW_WIDTH), const)]
    args = [h_all] * RW_COL_TILES + [shift_hist, mu, w0, w2p, a0, a2p, g2]
    widths = (3 * RW_WIDTH, RW_WIDTH, RW_WIDTH, RW_WIDTH)
    aliases = {}
    kern = functools.partial(_rw_pre_kernel, tt=tt)
    if prev_outs is not None:
        n_real = len(args)
        for j, po in enumerate(prev_outs):
            in_specs.append(pl.BlockSpec(memory_space=pl.ANY))
            args.append(po)
            aliases[n_real + j] = j
        kern = _drop_inputs(kern, n_real, len(prev_outs))
    return pl.pallas_call(
        kern,
        grid=(batch, nt),
        in_specs=in_specs,
        out_specs=[pl.BlockSpec((tt, wd), row_map) for wd in widths],
        out_shape=[jax.ShapeDtypeStruct((m, wd), F32) for wd in widths],
        scratch_shapes=[pltpu.VMEM((1, RW_COLS), F32)],
        input_output_aliases=aliases,
        compiler_params=_cparams(2),
        name="rwkv_pre",
    )(*args)


def _drop_inputs(kern, n_real, n_drop):
    def wrapped(*refs):
        return kern(*refs[:n_real], *refs[n_real + n_drop:])
    return wrapped


def _head_ones():
    r = lax.broadcasted_iota(I32, (LANES, LANES), 0) // HEAD_DIM
    c = lax.broadcasted_iota(I32, (LANES, LANES), 1) // HEAD_DIM
    return (r == c).astype(BF16)


def _head_sum(x, ones_bd):
    hi, lo = _split2(x)
    return _dot(hi, ones_bd) + _dot(lo, ones_bd)


def _stack_heads(x):
    lane = lax.broadcasted_iota(I32, x.shape, 1)
    zero = jnp.zeros_like(x)
    return jnp.concatenate([jnp.where(lane < HEAD_DIM, x, zero), jnp.where(lane >= HEAD_DIM, x, zero)],
                           axis=0)


def _rw_scan_kernel(r_ref, k_ref, v_ref, lw_ref, a_ref, g_ref, kks_ref, ka_ref, rk_ref, gng_ref,
                    gnb_ref, s0_ref, o_ref, sout_ref, st_ref, o_s, lhs_s, ufree_s, prb_s, akvo_s, bkt_s,
                    v_s, e_s, *, tb, c):
    ti = pl.program_id(2)
    n = 2 * c
    nchunk = tb // c
    tiles = range(RW_TILES)
    lanes = [slice(p * LANES, (p + 1) * LANES) for p in tiles]

    @pl.when(ti == 0)
    def _():
        st_ref[...] = s0_ref[...]

    ones_bd = _head_ones()
    ri = lax.broadcasted_iota(I32, (tb, tb), 0)
    ci = lax.broadcasted_iota(I32, (tb, tb), 1)
    l_chunk = jnp.logical_and(ri >= ci, ri // c == ci // c).astype(BF16)
    v_all, e_in, al_all, be_all, kc_all, rc_all, bonus = [], [], [], [], [], [], []
    for p in tiles:
        r = r_ref[:, lanes[p]]
        k0 = k_ref[:, lanes[p]]
        v = v_ref[:, lanes[p]]
        lw = lw_ref[:, lanes[p]]
        a = a_ref[:, lanes[p]]
        kk = k0 * kks_ref[:, lanes[p]]
        kk = kk * lax.rsqrt(jnp.maximum(_head_sum(kk * kk, ones_bd), 1e-24))
        kmod = k0 * (1.0 + (a - 1.0) * ka_ref[:, lanes[p]])
        lw_hi, lw_lo = _split2(lw)
        gcum = _dot(l_chunk, lw_hi) + _dot(l_chunk, lw_lo)
        e = jnp.exp(gcum)
        e_neg = jnp.exp(-gcum)
        v_all.append(v)
        e_in.append(e)
        al_all.append((-kk * jnp.exp(gcum - lw)).astype(BF16))
        be_all.append((kk * a * e_neg).astype(BF16))
        kc_all.append((kmod * e_neg).astype(BF16))
        rc_all.append((r * e).astype(BF16))
        bonus.append(_head_sum(r * kmod * rk_ref[:, lanes[p]], ones_bd) * v)

    rn = lax.broadcasted_iota(I32, (n, n), 0)
    cn = lax.broadcasted_iota(I32, (n, n), 1)
    same_head = rn // c == cn // c
    strict = jnp.logical_and(same_head, cn < rn).astype(F32)
    incl = jnp.logical_and(same_head, cn <= rn).astype(F32)
    eye = (rn == cn).astype(F32)

    def prepare(group):
        each = lambda f, *lists: [f(*xs) for xs in zip(*lists)]
        rows = [(p, slice(j * c, (j + 1) * c)) for p, j in group]
        al = [_stack_heads(al_all[p][r_]) for p, r_ in rows]
        rc = [_stack_heads(rc_all[p][r_]) for p, r_ in rows]
        bk = [jnp.concatenate([_stack_heads(be_all[p][r_]), _stack_heads(kc_all[p][r_])], axis=0)
              for p, r_ in rows]
        vs = [_stack_heads(v_all[p][r_]).astype(BF16) for p, r_ in rows]
        m4 = each(lambda al_, rc_, bk_: _dot_nt(jnp.concatenate([al_, rc_], axis=0), bk_), al, rc, bk)
        a_ab = [m[:n, :n] * strict for m in m4]
        lower = [jnp.concatenate([m[:n, n:] * strict, m[n:, n:] * incl], axis=0).astype(BF16) for m in m4]
        p_rb = [(m[n:, :n] * incl).astype(BF16) for m in m4]
        akv = each(_dot, lower, vs)
        tinv = [eye + a_ for a_ in a_ab]
        p_bf = [a_.astype(BF16) for a_ in a_ab]
        p_bf = [_dot(p_, p_).astype(BF16) for p_ in p_bf]
        span = 2
        while span < c:
            if 2 * span < c:
                both = each(lambda p_, t_: _dot(p_, jnp.concatenate([t_.astype(BF16), p_], axis=1)),
                            p_bf, tinv)
                tinv = each(lambda t_, b_: t_ + b_[:, :n], tinv, both)
                p_bf = [b_[:, n:].astype(BF16) for b_ in both]
            else:
                tinv = each(lambda t_, p_: t_ + _dot(p_, t_.astype(BF16)), tinv, p_bf)
            span *= 2
        wu = each(lambda t_, al_, akv_: _dot(t_.astype(BF16),
                                             jnp.concatenate([al_, akv_[:n].astype(BF16)], axis=1)),
                  tinv, al, akv)
        for i, (p, j) in enumerate(group):
            lhs_s[p, j] = jnp.concatenate([wu[i][:, :LANES].astype(BF16), rc[i]], axis=0)
            ufree_s[p, j] = wu[i][:, LANES:]
            prb_s[p, j] = p_rb[i]
            akvo_s[p, j] = akv[i][n:]
            bkt_s[p, j] = bk[i].astype(F32).T.astype(BF16)
            v_s[p, j] = vs[i]
            e_last = e_in[p][(j + 1) * c - 1:(j + 1) * c, :]
            e_s[p, j] = jnp.broadcast_to(e_last, (LANES, LANES)).T

    per_group = max(1, RW_PREP_GROUP // RW_TILES)
    for j0 in range(0, nchunk, per_group):
        prepare([(p, j) for j in range(j0, min(j0 + per_group, nchunk)) for p in tiles])

    def advance(j, sts):
        rows = pl.ds(pl.multiple_of(j * c, c), c)
        ws = [_dot(lhs_s[p, j], sts[p].astype(BF16)) for p in tiles]
        u_bf = [(ws[p][:n] + ufree_s[p, j]).astype(BF16) for p in tiles]
        o_bd = [ws[p][n:] + _dot(prb_s[p, j], u_bf[p]) + akvo_s[p, j] for p in tiles]
        upd = [_dot(bkt_s[p, j], jnp.concatenate([u_bf[p], v_s[p, j]], axis=0)) for p in tiles]
        for p in tiles:
            o_s[rows, lanes[p]] = o_bd[p][:c] + o_bd[p][c:]
        return tuple(e_s[p, j] * (sts[p] + upd[p]) for p in tiles)

    sts = lax.fori_loop(0, nchunk, advance, tuple(st_ref[p] for p in tiles))
    for p in tiles:
        st_ref[p] = sts[p]
        sout_ref[p] = sts[p]
        o = o_s[:, lanes[p]]
        mu_o = _head_sum(o, ones_bd) * (1.0 / HEAD_DIM)
        oc = o - mu_o
        var_o = _head_sum(oc * oc, ones_bd) * (1.0 / HEAD_DIM)
        on = oc * lax.rsqrt(var_o + GN_EPS) * gng_ref[:, lanes[p]] + gnb_ref[:, lanes[p]]
        o_ref[:, lanes[p]] = (on + bonus[p]) * g_ref[:, lanes[p]]


def _rw_scan(rkv, lw, a, g, row0, batch, t, tb, c, params, wkv0, prev_out):
    m = rkv.shape[0]
    nt = t // tb
    rb0 = row0 // tb
    n = 2 * c
    nchunk = tb // c
    wide = RW_TILES * LANES
    groups = HEAD_PAIRS // RW_TILES

    def col(off):
        return lambda b, p, i: (rb0 + b * nt + i, off + p)

    s_t = jnp.swapaxes(wkv0, -1, -2).reshape(batch, HEAD_PAIRS, 2, HEAD_DIM, HEAD_DIM)
    st0 = jnp.einsum('bphkv,hg->bphkgv', s_t, jnp.eye(2, dtype=F32)).reshape(batch, HEAD_PAIRS, LANES, LANES)
    pspec = pl.BlockSpec((1, wide), lambda b, p, i: (0, p))
    sspec = pl.BlockSpec((None, RW_TILES, LANES, LANES), lambda b, p, i: (b, p, 0, 0))
    in_specs = [pl.BlockSpec((tb, wide), col(0)), pl.BlockSpec((tb, wide), col(groups)),
                pl.BlockSpec((tb, wide), col(2 * groups)),
                pl.BlockSpec((tb, wide), col(0)), pl.BlockSpec((tb, wide), col(0)),
                pl.BlockSpec((tb, wide), col(0))] + [pspec] * 5 + [sspec]
    args = [rkv, rkv, rkv, lw, a, g] + list(params) + [st0]
    aliases = {}
    kern = functools.partial(_rw_scan_kernel, tb=tb, c=c)
    if prev_out is not None:
        n_real = len(args)
        in_specs.append(pl.BlockSpec(memory_space=pl.ANY))
        args.append(prev_out)
        aliases = {n_real: 0}
        kern = _drop_inputs(kern, n_real, 1)
    per = (RW_TILES, nchunk)
    scratch = [pltpu.VMEM((RW_TILES, LANES, LANES), F32), pltpu.VMEM((tb, wide), F32)]
    scratch += [pltpu.VMEM(per + (2 * n, LANES), BF16), pltpu.VMEM(per + (n, LANES), F32),
                pltpu.VMEM(per + (n, n), BF16), pltpu.VMEM(per + (n, LANES), F32),
                pltpu.VMEM(per + (LANES, 2 * n), BF16), pltpu.VMEM(per + (n, LANES), BF16),
                pltpu.VMEM(per + (LANES, LANES), F32)]
    o_c, st = pl.pallas_call(
        kern,
        grid=(batch, groups, nt),
        in_specs=in_specs,
        out_specs=[pl.BlockSpec((tb, wide), col(0)), sspec],
        out_shape=[jax.ShapeDtypeStruct((m, RW_WIDTH), F32),
                   jax.ShapeDtypeStruct((batch, HEAD_PAIRS, LANES, LANES), F32)],
        scratch_shapes=scratch,
        input_output_aliases=aliases,
        compiler_params=_cparams(3),
        name="rwkv_scan",
    )(*args)
    st = st.reshape(batch, HEAD_PAIRS, 2, HEAD_DIM, 2, HEAD_DIM)
    wkv = jnp.stack([st[:, :, 0, :, 0, :], st[:, :, 1, :, 1, :]], axis=2)
    return o_c, jnp.swapaxes(wkv, -1, -2).reshape(batch, RW_HEADS, HEAD_DIM, HEAD_DIM)


def _outproj_kernel(oa_ref, ob_ref, oc_ref, x_ref, w_ref, g_ref, b_ref, y_ref):
    mix = _dot(oa_ref[...].astype(BF16), w_ref[0:SB_WIDTH, :])
    mix += _dot(ob_ref[...].astype(BF16), w_ref[SB_WIDTH:SB_WIDTH + POOL_WIDTH, :])
    mix += _dot(oc_ref[...].astype(BF16), w_ref[SB_WIDTH + POOL_WIDTH:, :])
    y_ref[...] = _layer_norm(DN_ALPHA * x_ref[...] + mix, g_ref[...], b_ref[...])


def _outproj_ln(oa, ob, oc, x, w_bf, g, b, tm):
    m, d = x.shape
    row = lambda i: (i, 0)
    const = lambda i: (0, 0)
    return pl.pallas_call(
        _outproj_kernel,
        grid=(m // tm,),
        in_specs=[pl.BlockSpec((tm, SB_WIDTH), row), pl.BlockSpec((tm, POOL_WIDTH), row),
                  pl.BlockSpec((tm, RW_WIDTH), row), pl.BlockSpec((tm, d), row),
                  pl.BlockSpec(w_bf.shape, const), pl.BlockSpec((1, d), const),
                  pl.BlockSpec((1, d), const)],
        out_specs=pl.BlockSpec((tm, d), row),
        out_shape=jax.ShapeDtypeStruct((m, d), F32),
        compiler_params=_cparams(1),
        name="outproj_ln",
    )(oa, ob, oc, x, w_bf, g, b)


def _router_kernel(x_ref, w_ref, b_ref, idx_ref, gate_ref, rank_ref, cnt_ref, run_ref, *, tm):
    @pl.when(pl.program_id(0) == 0)
    def _():
        run_ref[...] = jnp.zeros_like(run_ref)

    logits = _dot(x_ref[...].astype(BF16), w_ref[...].astype(BF16)) + b_ref[...]
    lane = lax.broadcasted_iota(I32, (tm, N_EXPERTS), 1).astype(F32)
    work = logits
    vals, idxs = [], []
    for _ in range(TOP_K):
        top = jnp.max(work, axis=1, keepdims=True)
        idx = jnp.min(jnp.where(work == top, lane, float(N_EXPERTS)), axis=1, keepdims=True)
        vals.append(top)
        idxs.append(idx)
        work = jnp.where(lane == idx, -jnp.inf, work)
    exps = [jnp.exp(v - vals[0]) for v in vals]
    total = exps[0] + exps[1] + exps[2] + exps[3]
    out_lane = lax.broadcasted_iota(I32, (tm, LANES), 1)
    lane_f = out_lane.astype(F32)
    onehots = [(lane_f == idxs[k]).astype(F32) for k in range(TOP_K)]
    chosen = onehots[0] + onehots[1] + onehots[2] + onehots[3]
    earlier = (lax.broadcasted_iota(I32, (tm, tm), 1) < lax.broadcasted_iota(I32, (tm, tm), 0)).astype(BF16)
    base = run_ref[...] + _dot(earlier, chosen.astype(BF16))
    run_ref[...] = run_ref[...] + jnp.sum(chosen, axis=0, keepdims=True)
    cnt_ref[...] = run_ref[...].astype(I32)
    idx_out = jnp.zeros((tm, LANES), F32)
    gate_out = jnp.zeros((tm, LANES), F32)
    rank_out = jnp.zeros((tm, LANES), F32)
    for k in range(TOP_K):
        idx_out = jnp.where(out_lane == k, idxs[k], idx_out)
        gate_out = jnp.where(out_lane == k, exps[k] / total, gate_out)
        rank_out = jnp.where(out_lane == k, jnp.sum(onehots[k] * base, axis=1, keepdims=True), rank_out)
    idx_ref[...] = idx_out.astype(I32)
    gate_ref[...] = gate_out
    rank_ref[...] = rank_out.astype(I32)


def _router(x, w, b, tm):
    m, d = x.shape
    return pl.pallas_call(
        functools.partial(_router_kernel, tm=tm),
        grid=(m // tm,),
        in_specs=[pl.BlockSpec((tm, d), lambda i: (i, 0)),
                  pl.BlockSpec((d, N_EXPERTS), lambda i: (0, 0)),
                  pl.BlockSpec((1, N_EXPERTS), lambda i: (0, 0))],
        out_specs=[pl.BlockSpec((tm, LANES), lambda i: (i, 0))] * 3 + [pl.BlockSpec((1, LANES), lambda i: (0, 0))],
        out_shape=[jax.ShapeDtypeStruct((m, LANES), I32), jax.ShapeDtypeStruct((m, LANES), F32),
                   jax.ShapeDtypeStruct((m, LANES), I32), jax.ShapeDtypeStruct((1, LANES), I32)],
        scratch_shapes=[pltpu.VMEM((1, LANES), F32)],
        compiler_params=_cparams(1),
        name="router",
    )(x, w, b)


def _row_copy(src_hbm, dst_ref, sem, src_row, dst_row):
    return pltpu.make_async_copy(src_hbm.at[pl.ds(src_row, 1), :], dst_ref.at[pl.ds(dst_row, 1), :], sem)


GATHER_UNROLL = 8


def _gather_kernel(tok_ref, tok_next_ref, nu_ref, x_hbm, o_ref, rows_ref, sem, *, bm):
    i = pl.program_id(0)
    n_used = nu_ref[0]

    def start_block(toks, slot):
        def body(k, carry):
            for u in range(GATHER_UNROLL):
                r = k * GATHER_UNROLL + u
                _row_copy(x_hbm, rows_ref.at[slot], sem.at[slot], toks[0, 0, r], r).start()
            return carry

        lax.fori_loop(0, bm // GATHER_UNROLL, body, 0)

    @pl.when(i == 0)
    def _():
        start_block(tok_ref, 0)

    @pl.when(i + 1 < n_used)
    def _():
        start_block(tok_next_ref, (i + 1) % 2)

    @pl.when(i < n_used)
    def _():
        slot = i % 2

        def wait(k, carry):
            for u in range(GATHER_UNROLL):
                _row_copy(x_hbm, rows_ref.at[slot], sem.at[slot], 0, k * GATHER_UNROLL + u).wait()
            return carry

        lax.fori_loop(0, bm // GATHER_UNROLL, wait, 0)
        o_ref[...] = rows_ref[slot].astype(BF16)


def _gather_rows(x, buf_tok, n_used, bm):
    d = x.shape[1]
    nb = buf_tok.shape[0] // bm
    toks = buf_tok.reshape(nb, 1, bm)
    return pl.pallas_call(
        functools.partial(_gather_kernel, bm=bm),
        grid=(nb,),
        in_specs=[pl.BlockSpec((1, 1, bm), lambda i: (i, 0, 0), memory_space=pltpu.SMEM),
                  pl.BlockSpec((1, 1, bm), lambda i: (jnp.minimum(i + 1, nb - 1), 0, 0),
                               memory_space=pltpu.SMEM),
                  pl.BlockSpec(memory_space=pltpu.SMEM),
                  pl.BlockSpec(memory_space=pl.ANY)],
        out_specs=pl.BlockSpec((bm, d), lambda i: (i, 0)),
        out_shape=jax.ShapeDtypeStruct((nb * bm, d), BF16),
        scratch_shapes=[pltpu.VMEM((2, bm, d), F32), pltpu.SemaphoreType.DMA((2,))],
        compiler_params=_cparams(1),
        name="moe_gather",
    )(toks, toks, n_used, x)


def _step_table(block_e, n_used):
    nb = block_e.shape[0]
    i = jnp.arange(nb, dtype=I32)
    valid = i < n_used[0]
    first = jnp.logical_and(valid, jnp.logical_or(i == 0, block_e != jnp.roll(block_e, 1)))
    first_pos = jnp.where(first, i, nb)
    next_pos = jnp.concatenate([lax.cummin(first_pos, reverse=True)[1:], jnp.full((1,), nb, I32)])
    next_e = jnp.where(next_pos < nb, block_e[jnp.minimum(next_pos, nb - 1)], -1)
    return jnp.stack([block_e, first.astype(I32), next_e.astype(I32), jnp.zeros_like(block_e)],
                     axis=1).reshape(nb * 4)


def _weight_copies(w_hbm, wbuf, sem, layer, expert, cols, tn):
    return [pltpu.make_async_copy(w_hbm.at[layer, expert, :, pl.ds(pl.multiple_of(col, tn), tn)],
                                  wbuf.at[h], sem.at[h]) for h, col in enumerate(cols)]


def _fetch_expert_tile(tbl_ref, w_hbm, wbuf, wbf, sem, layer, cols, tn):
    i = pl.program_id(1)
    expert = tbl_ref[4 * i]
    next_expert = tbl_ref[4 * i + 2]

    @pl.when(tbl_ref[4 * i + 1] == 1)
    def _():
        @pl.when(i == 0)
        def _():
            for cp in _weight_copies(w_hbm, wbuf, sem, layer, expert, cols, tn):
                cp.start()

        for cp in _weight_copies(w_hbm, wbuf, sem, layer, expert, cols, tn):
            cp.wait()
        for h in range(len(cols)):
            wbf[h] = wbuf[h].astype(BF16)

        @pl.when(next_expert >= 0)
        def _():
            for cp in _weight_copies(w_hbm, wbuf, sem, layer, next_expert, cols, tn):
                cp.start()


def _expert_up_kernel(tbl_ref, nu_ref, x_ref, w_hbm, bg_ref, bl_ref, act_ref, wbuf, wbf, sem, *, layer, tn):
    j = pl.program_id(0)
    nj = pl.num_programs(0)

    @pl.when(pl.program_id(1) < nu_ref[0])
    def _():
        _fetch_expert_tile(tbl_ref, w_hbm, wbuf, wbf, sem, layer, (j * tn, (nj + j) * tn), tn)
        x = x_ref[...]
        h_glu = _dot(x, wbf[0]) + bg_ref[...]
        h_lin = _dot(x, wbf[1]) + bl_ref[...]
        h_glu = jnp.minimum(h_glu, SWIGLU_LIMIT)
        h_lin = jnp.clip(h_lin, -SWIGLU_LIMIT, SWIGLU_LIMIT)
        act = h_glu * jax.nn.sigmoid(SWIGLU_ALPHA * h_glu) * (h_lin + 1.0)
        act_ref[...] = act.astype(BF16)


def _expert_up(xb, w1, b1, layer, tbl, n_used, bm, tn):
    rows, d = xb.shape
    de = w1.shape[3] // 2
    nj = de // tn
    nb = rows // bm

    def blk(i, nu):
        return jnp.minimum(i, nu[0] - 1)

    grid_spec = pltpu.PrefetchScalarGridSpec(
        num_scalar_prefetch=2,
        grid=(nj, nb),
        in_specs=[pl.BlockSpec((bm, d), lambda j, i, tb, nu: (blk(i, nu), 0)),
                  pl.BlockSpec(memory_space=pl.ANY),
                  pl.BlockSpec((None, None, 1, tn), lambda j, i, tb, nu: (layer, tb[4 * blk(i, nu)], 0, j)),
                  pl.BlockSpec((None, None, 1, tn),
                               lambda j, i, tb, nu: (layer, tb[4 * blk(i, nu)], 0, nj + j))],
        out_specs=pl.BlockSpec((bm, tn), lambda j, i, tb, nu: (blk(i, nu), j)),
        scratch_shapes=[pltpu.VMEM((2, d, tn), F32), pltpu.VMEM((2, d, tn), BF16),
                        pltpu.SemaphoreType.DMA((2,))],
    )
    return pl.pallas_call(
        functools.partial(_expert_up_kernel, layer=layer, tn=tn),
        grid_spec=grid_spec,
        out_shape=jax.ShapeDtypeStruct((rows, de), BF16),
        compiler_params=_cparams(2),
        name="moe_up",
    )(tbl, n_used, xb, w1, b1, b1)


def _expert_down_kernel(tbl_ref, nu_ref, a_ref, w_hbm, b_ref, y_ref, wbuf, wbf, sem, *, layer, tn):
    j = pl.program_id(0)

    @pl.when(pl.program_id(1) < nu_ref[0])
    def _():
        _fetch_expert_tile(tbl_ref, w_hbm, wbuf, wbf, sem, layer, (j * tn,), tn)
        y_ref[...] = _dot(a_ref[...], wbf[0]) + b_ref[...]


def _expert_down(act, w2, b2, layer, tbl, n_used, bm, tn):
    rows, de = act.shape
    d = w2.shape[3]
    nj = d // tn
    nb = rows // bm

    def blk(i, nu):
        return jnp.minimum(i, nu[0] - 1)

    grid_spec = pltpu.PrefetchScalarGridSpec(
        num_scalar_prefetch=2,
        grid=(nj, nb),
        in_specs=[pl.BlockSpec((bm, de), lambda j, i, tb, nu: (blk(i, nu), 0)),
                  pl.BlockSpec(memory_space=pl.ANY),
                  pl.BlockSpec((None, None, 1, tn), lambda j, i, tb, nu: (layer, tb[4 * blk(i, nu)], 0, j))],
        out_specs=pl.BlockSpec((bm, tn), lambda j, i, tb, nu: (blk(i, nu), j)),
        scratch_shapes=[pltpu.VMEM((1, de, tn), F32), pltpu.VMEM((1, de, tn), BF16),
                        pltpu.SemaphoreType.DMA((1,))],
    )
    return pl.pallas_call(
        functools.partial(_expert_down_kernel, layer=layer, tn=tn),
        grid_spec=grid_spec,
        out_shape=jax.ShapeDtypeStruct((rows, d), F32),
        compiler_params=_cparams(2),
        name="moe_down",
    )(tbl, n_used, act, w2, b2)


def _combine_kernel(dest_ref, dest_next_ref, y_hbm, gate_ref, x_ref, g_ref, b_ref, o_ref, rows_ref, sem,
                    *, tm):
    i = pl.program_id(0)

    def start_block(dests, slot):
        def body(r, carry):
            for k in range(TOP_K):
                _row_copy(y_hbm, rows_ref.at[slot, k], sem.at[slot], dests[0, 0, r * TOP_K + k], r).start()
            return carry

        lax.fori_loop(0, tm, body, 0)

    @pl.when(i == 0)
    def _():
        start_block(dest_ref, 0)

    @pl.when(i + 1 < pl.num_programs(0))
    def _():
        start_block(dest_next_ref, (i + 1) % 2)

    slot = i % 2

    def wait(r, carry):
        for k in range(TOP_K):
            _row_copy(y_hbm, rows_ref.at[slot, k], sem.at[slot], 0, r).wait()
        return carry

    lax.fori_loop(0, tm, wait, 0)
    gates = gate_ref[...]
    ffn = gates[:, 0:1] * rows_ref[slot, 0]
    for k in range(1, TOP_K):
        ffn += gates[:, k:k + 1] * rows_ref[slot, k]
    o_ref[...] = _layer_norm(DN_ALPHA * x_ref[...] + ffn, g_ref[...], b_ref[...])


def _combine_ln(y, dest, gates, x, g, b, tm):
    m, d = x.shape
    nb = m // tm
    dests = dest.reshape(nb, 1, tm * TOP_K)
    return pl.pallas_call(
        functools.partial(_combine_kernel, tm=tm),
        grid=(nb,),
        in_specs=[pl.BlockSpec((1, 1, tm * TOP_K), lambda i: (i, 0, 0), memory_space=pltpu.SMEM),
                  pl.BlockSpec((1, 1, tm * TOP_K), lambda i: (jnp.minimum(i + 1, nb - 1), 0, 0),
                               memory_space=pltpu.SMEM),
                  pl.BlockSpec(memory_space=pl.ANY),
                  pl.BlockSpec((tm, LANES), lambda i: (i, 0)),
                  pl.BlockSpec((tm, d), lambda i: (i, 0)),
                  pl.BlockSpec((1, d), lambda i: (0, 0)),
                  pl.BlockSpec((1, d), lambda i: (0, 0))],
        out_specs=pl.BlockSpec((tm, d), lambda i: (i, 0)),
        out_shape=jax.ShapeDtypeStruct((m, d), F32),
        scratch_shapes=[pltpu.VMEM((2, TOP_K, tm, d), F32), pltpu.SemaphoreType.DMA((2,))],
        compiler_params=_cparams(1),
        name="moe_combine_ln",
    )(dests, dests, y, gates, x, g, b)


def _routing_tables(top_i, rank, counts, bm):
    n = top_i.shape[0]
    nk = n * TOP_K
    padded = (counts + bm - 1) // bm * bm
    pad_end = jnp.cumsum(padded)
    pad_start = pad_end - padded
    dest = (pad_start[top_i] + rank).astype(I32).reshape(nk)
    nb = (nk + N_EXPERTS * (bm - 1)) // bm
    buf_tok = jnp.zeros((nb * bm,), I32).at[dest].set(jnp.arange(nk, dtype=I32) // TOP_K)
    block_row = jnp.arange(nb, dtype=I32)[:, None] * bm
    block_e = jnp.minimum(jnp.sum((pad_end[None, :] <= block_row).astype(I32), axis=1), N_EXPERTS - 1)
    n_used = (pad_end[-1:] // bm).astype(I32)
    return dest, buf_tok, _step_table(block_e, n_used), n_used


def _moe_ln(x, lp, big, layer):
    idx, gates, rank, counts = _router(x, lp['w_router'], lp['b_router'].reshape(1, N_EXPERTS), tm=256)
    dest, buf_tok, steps, n_used = _routing_tables(idx[:, :TOP_K], rank[:, :TOP_K],
                                                     counts[0, :N_EXPERTS], MOE_BM)
    xb = _gather_rows(x, buf_tok, n_used, MOE_BM)
    nl, e, d, de2 = big['w_e1'].shape
    act = _expert_up(xb, big['w_e1'], big['b_e1'].reshape(nl, e, 1, de2), layer, steps, n_used,
                     MOE_BM, tn=1024)
    y = _expert_down(act, big['w_e2'], big['b_e2'].reshape(nl, e, 1, d), layer, steps, n_used,
                     MOE_BM, tn=1024)
    return _combine_ln(y, dest, gates, x, lp['ln_ffn_g'].reshape(1, d), lp['ln_ffn_b'].reshape(1, d),
                       tm=128)


def _layer(x, lp, big, layer, st, n_p, bp, tp, bs, ts):
    d = x.shape[1]
    h = _inproj(x, big['w_in'], layer, tm=768, tn=768)
    qkv_w = 3 * SB_WIDTH

    o_a = _sb_attention(h, 0, bp, tp, 256, None, None)
    hist = (st['cache_k'].reshape(bs, -1, SB_WIDTH), st['cache_v'].reshape(bs, -1, SB_WIDTH))
    o_a = _sb_attention(h, n_p, bs, ts, ts, hist, o_a)
    past = hist[0].shape[1]

    w_pool = lp['pool_w']
    scale = lp['pool_scale'].reshape(1, POOL_WIDTH)
    o_b = _pool_mixer(h, 0, bp, tp, 512, jnp.zeros((bp, 16, POOL_WIDTH), F32), w_pool, scale, 0, None)
    hist16 = jnp.pad(st['state_pool'], ((0, 0), (1, 0), (0, 0)))
    o_b = _pool_mixer(h, n_p, bs, ts, ts, hist16, w_pool, scale, past, o_b)

    zero_lora = jnp.zeros((RW_LORA // 2, RW_WIDTH), F32)
    w2p = jnp.concatenate([lp['rw_w2'], zero_lora], axis=0)
    a2p = jnp.concatenate([zero_lora, lp['rw_a2']], axis=0)
    pre_args = (lp['rw_mu'].reshape(1, RW_COLS), lp['rw_w0'].reshape(1, RW_WIDTH), w2p,
                lp['rw_a0'].reshape(1, RW_WIDTH), a2p, lp['rw_g2'])
    pre = _rw_pre(h, 0, bp, tp, 256, jnp.zeros((bp, 1, RW_COLS), F32), *pre_args, None)
    rkv, lw, a, g = _rw_pre(h, n_p, bs, ts, ts, st['state_shift'], *pre_args, pre)
    prm = tuple(lp[k].reshape(1, RW_WIDTH) for k in ('rw_kk', 'rw_ka', 'rw_rk', 'rw_gn_g', 'rw_gn_b'))
    o_c, wkv_p = _rw_scan(rkv, lw, a, g, 0, bp, tp, 512, 64, prm,
                          jnp.zeros((bp, RW_HEADS, HEAD_DIM, HEAD_DIM), F32), None)
    o_c, wkv_s = _rw_scan(rkv, lw, a, g, n_p, bs, ts, ts, ts, prm, st['state_wkv'], o_c)

    x1 = _outproj_ln(o_a, o_b, o_c, x, lp['w_out'].astype(BF16), lp['ln_mix_g'].reshape(1, d),
                     lp['ln_mix_b'].reshape(1, d), tm=256)
    x2 = _moe_ln(x1, lp, big, layer)

    def states(r0, b, t, wkv):
        r1 = r0 + b * t
        k_new = h[r0:r1, SB_WIDTH:2 * SB_WIDTH].reshape(b, t, SB_HEADS, HEAD_DIM)
        v_new = h[r0:r1, 2 * SB_WIDTH:qkv_w].reshape(b, t, SB_HEADS, HEAD_DIM)
        pool_new = jnp.stack([h[r0 + (i + 1) * t - POOL_HIST:r0 + (i + 1) * t, qkv_w:qkv_w + POOL_WIDTH]
                              for i in range(b)])
        shift_new = h[r0 + t - 1:r1:t, qkv_w + POOL_WIDTH:].reshape(b, 1, RW_COLS)
        return k_new, v_new, pool_new, shift_new, wkv

    return x2, states(0, bp, tp, wkv_p), states(n_p, bs, ts, wkv_s)


def kernel(x_prompt, x_sample, cache_k, cache_v, state_pool, state_shift, state_wkv, w_in, w_out, ln_mix_g, ln_mix_b, ln_ffn_g, ln_ffn_b, pool_w, pool_scale, rw_mu, rw_w0, rw_w2, rw_a0, rw_a2, rw_g2, rw_kk, rw_ka, rw_rk, rw_gn_g, rw_gn_b, w_router, b_router, w_e1, b_e1, w_e2, b_e2):
    bp, tp, d = x_prompt.shape
    bs, ts, _ = x_sample.shape
    n_p = bp * tp
    x = jnp.concatenate([x_prompt.reshape(n_p, d), x_sample.reshape(bs * ts, d)], axis=0)
    big = dict(w_in=w_in, w_e1=w_e1, b_e1=b_e1, w_e2=w_e2, b_e2=b_e2)
    weights = dict(w_out=w_out, ln_mix_g=ln_mix_g, ln_mix_b=ln_mix_b, ln_ffn_g=ln_ffn_g,
                   ln_ffn_b=ln_ffn_b, pool_w=pool_w, pool_scale=pool_scale, rw_mu=rw_mu, rw_w0=rw_w0,
                   rw_w2=rw_w2, rw_a0=rw_a0, rw_a2=rw_a2, rw_g2=rw_g2, rw_kk=rw_kk, rw_ka=rw_ka,
                   rw_rk=rw_rk, rw_gn_g=rw_gn_g, rw_gn_b=rw_gn_b, w_router=w_router,
                   b_router=b_router)
    st_p, st_s = [], []
    for l in range(w_in.shape[0]):
        lp = {k: v[l] for k, v in weights.items()}
        st = dict(cache_k=cache_k[l], cache_v=cache_v[l], state_pool=state_pool[l],
                  state_shift=state_shift[l], state_wkv=state_wkv[l])
        x, sp, ss = _layer(x, lp, big, l, st, n_p, bp, tp, bs, ts)
        st_p.append(sp)
        st_s.append(ss)
    y_prompt = x[:n_p].reshape(bp, tp, d)
    y_sample = x[n_p:].reshape(bs, ts, d)
    k_p, v_p, pool_p, shift_p, wkv_p = (jnp.stack([s[i] for s in st_p]) for i in range(5))
    k_s, v_s, pool_s, shift_s, wkv_s = (jnp.stack([s[i] for s in st_s]) for i in range(5))
    return (y_prompt, y_sample, k_p, v_p, pool_p, shift_p, wkv_p, k_s, v_s, pool_s, shift_s, wkv_s)
```

```python
import functools

import jax
import jax.numpy as jnp
from jax import lax
from jax.experimental import pallas as pl
from jax.experimental.pallas import tpu as pltpu

F32 = jnp.float32
BF16 = jnp.bfloat16
I32 = jnp.int32

HEAD_DIM = 64
SB_HEADS = 12
SB_WIDTH = SB_HEADS * HEAD_DIM
POOL_WINDOWS = (2, 4, 8, 16)
POOL_GC = 128
POOL_WIDTH = POOL_GC * len(POOL_WINDOWS)
POOL_HIST = max(POOL_WINDOWS) - 1
RW_HEADS = 12
RW_WIDTH = RW_HEADS * HEAD_DIM
RW_LORA = 128
RW_GATE = 128
RW_COLS = 3 * RW_WIDTH + RW_LORA + RW_GATE
N_EXPERTS = 32
TOP_K = 4
SWIGLU_LIMIT = 7.0
SWIGLU_ALPHA = 1.702
DEPTH = 2
DN_ALPHA = (2.0 * DEPTH) ** 0.25
LN_EPS = 1e-5
GN_EPS = 64e-5

LANES = 128
HEAD_PAIRS = SB_WIDTH // LANES
VMEM_LIMIT = 56 * 1024 * 1024
ATT_LOG_FLOOR = -110.0

POOL_COL0 = 3 * SB_WIDTH
RW_COL0 = POOL_COL0 + POOL_WIDTH
RW_COL_TILE = 256
RW_COL_TILES = RW_COLS // RW_COL_TILE
MOE_BM = 256
RW_TILES = 2
RW_PREP_GROUP = 8


def _cparams(n_axes):
    return pltpu.CompilerParams(dimension_semantics=("arbitrary",) * n_axes,
                                vmem_limit_bytes=VMEM_LIMIT)


def _dot(a, b, precision=None):
    return jnp.dot(a, b, preferred_element_type=F32, precision=precision)


def _dot_nt(a, b, precision=None):
    return lax.dot_general(a, b, (((1,), (1,)), ((), ())), preferred_element_type=F32,
                           precision=precision)


def _dot_tn(a, b, precision=None):
    return lax.dot_general(a, b, (((0,), (0,)), ((), ())), preferred_element_type=F32,
                           precision=precision)


def _layer_norm(x, g, b):
    mu = jnp.mean(x, axis=-1, keepdims=True)
    xc = x - mu
    var = jnp.mean(xc * xc, axis=-1, keepdims=True)
    return xc * lax.rsqrt(var + LN_EPS) * g + b


def _inproj_kernel(x_ref, w_ref, o_ref, wbf_ref):
    @pl.when(pl.program_id(1) == 0)
    def _():
        wbf_ref[...] = w_ref[...].astype(BF16)

    o_ref[...] = _dot(x_ref[...].astype(BF16), wbf_ref[...])


def _inproj(x, w, layer, tm, tn):
    m, k = x.shape
    n = w.shape[2]
    return pl.pallas_call(
        _inproj_kernel,
        grid=(n // tn, m // tm),
        in_specs=[pl.BlockSpec((tm, k), lambda j, i: (i, 0)),
                  pl.BlockSpec((None, k, tn), lambda j, i: (layer, 0, j))],
        out_specs=pl.BlockSpec((tm, tn), lambda j, i: (i, j)),
        out_shape=jax.ShapeDtypeStruct((m, n), F32),
        scratch_shapes=[pltpu.VMEM((k, tn), BF16)],
        compiler_params=_cparams(2),
        name="inproj",
    )(x, w)


def _split2(a):
    hi = a.astype(BF16)
    lo = (a - hi.astype(F32)).astype(BF16)
    return hi, lo


def _suffix_matrix(n):
    return (lax.broadcasted_iota(I32, (n, n), 0) > lax.broadcasted_iota(I32, (n, n), 1)).astype(BF16)


def _sb_block(qhs, k_bf, v_bf, suffix, carries, mask):
    heads = range(2)
    z = [_dot_nt(qhs[h], k_bf) for h in heads]
    sp = [jnp.maximum(z_, 0.0) + jnp.log1p(jnp.exp(-jnp.abs(z_))) for z_ in z]
    log_stay = [-s_ if mask is None else jnp.where(mask, -s_, 0.0) for s_ in sp]
    parts = [_split2(l_) for l_ in log_stay]
    hi = [_dot(parts[h][0], suffix) for h in heads]
    lo = [_dot(parts[h][1], suffix) for h in heads]
    after = [hi[h] + lo[h] + carries[h] for h in heads]
    att = [jnp.exp(z[h] - sp[h] + after[h]) for h in heads]
    if mask is not None:
        att = [jnp.where(mask, a_, 0.0) for a_ in att]
    pv = [_dot(att[h].astype(BF16), v_bf) for h in heads]
    return pv, [after[h][:, 0:1] + log_stay[h][:, 0:1] for h in heads]


def _sb_live(carries):
    return (jnp.max(jnp.maximum(carries[0], carries[1])) > ATT_LOG_FLOOR).astype(I32)


def _sb_sweep(qhs, k_ref, v_ref, tk, n_blocks, suffix, live, accs, carries):
    def cond(s):
        return jnp.logical_and(s[0] < n_blocks, s[1] > 0)

    def body(s):
        step, _, accs, carries = s
        rows = pl.ds(pl.multiple_of((n_blocks - 1 - step) * tk, tk), tk)
        k_bf = k_ref[rows, :].astype(BF16)
        v_bf = v_ref[rows, :].astype(BF16)
        pv, carries = _sb_block(qhs, k_bf, v_bf, suffix, carries, None)
        accs = tuple(accs[h] + pv[h] for h in range(2))
        return step + 1, _sb_live(carries), accs, tuple(carries)

    _, live, accs, carries = lax.while_loop(cond, body, (jnp.int32(0), live, accs, carries))
    return live, accs, carries


def _sb_kernel(*refs, tq, th, past):
    if past:
        q_ref, kn_ref, vn_ref, kh_ref, vh_ref, o_ref, kbf_ref, vbf_ref = refs
    else:
        q_ref, kn_ref, vn_ref, o_ref, kbf_ref, vbf_ref = refs
    qi = pl.program_id(2)

    @pl.when(qi == 0)
    def _():
        kbf_ref[...] = kn_ref[...].astype(BF16)
        vbf_ref[...] = vn_ref[...].astype(BF16)

    q = q_ref[...] * (HEAD_DIM ** -0.5)
    lane = lax.broadcasted_iota(I32, (tq, LANES), 1)
    qhs = [jnp.where(lane < HEAD_DIM, q, 0.0).astype(BF16),
           jnp.where(lane >= HEAD_DIM, q, 0.0).astype(BF16)]
    suffix_new = _suffix_matrix(tq)
    diag_mask = lax.broadcasted_iota(I32, (tq, tq), 1) < lax.broadcasted_iota(I32, (tq, tq), 0)
    rows = pl.ds(pl.multiple_of(qi * tq, tq), tq)
    k_bf = kbf_ref[rows, :]
    v_bf = vbf_ref[rows, :]
    accs, carries = _sb_block(qhs, k_bf, v_bf, suffix_new, [jnp.zeros((tq, 1), F32)] * 2, diag_mask)
    accs, carries = tuple(accs), tuple(carries)
    live, accs, carries = _sb_sweep(qhs, kbf_ref, vbf_ref, tq, qi, suffix_new, _sb_live(carries),
                                    accs, carries)
    if past:
        _, accs, carries = _sb_sweep(qhs, kh_ref, vh_ref, th, past // th, _suffix_matrix(th), live,
                                     accs, carries)
    o_ref[...] = jnp.where(lane < HEAD_DIM, accs[0], accs[1])


def _sb_attention(h_all, row0, batch, t, tq, hist, prev_out):
    m = h_all.shape[0]
    nq = t // tq
    rb0 = row0 // tq
    sb0 = row0 // t
    in_specs = [pl.BlockSpec((tq, LANES), lambda b, p, i: (rb0 + b * nq + i, p)),
                pl.BlockSpec((t, LANES), lambda b, p, i: (sb0 + b, HEAD_PAIRS + p)),
                pl.BlockSpec((t, LANES), lambda b, p, i: (sb0 + b, 2 * HEAD_PAIRS + p))]
    args = [h_all, h_all, h_all]
    past, th = 0, 0
    if hist is not None:
        past = hist[0].shape[1]
        th = min(past, 512)
        in_specs += [pl.BlockSpec((None, past, LANES), lambda b, p, i: (b, 0, p))] * 2
        args += list(hist)
    aliases = {}
    if prev_out is not None:
        in_specs.append(pl.BlockSpec(memory_space=pl.ANY))
        args.append(prev_out)
        aliases = {len(args) - 1: 0}
    kern = functools.partial(_sb_kernel, tq=tq, th=th, past=past)
    if prev_out is not None:
        kern = _drop_last_input(kern, n_in=len(args))
    return pl.pallas_call(
        kern,
        grid=(batch, HEAD_PAIRS, nq),
        in_specs=in_specs,
        out_specs=pl.BlockSpec((tq, LANES), lambda b, p, i: (rb0 + b * nq + i, p)),
        out_shape=jax.ShapeDtypeStruct((m, SB_WIDTH), F32),
        scratch_shapes=[pltpu.VMEM((t, LANES), BF16)] * 2,
        input_output_aliases=aliases,
        compiler_params=_cparams(3),
        name="sb_attention",
    )(*args)


def _drop_last_input(kern, n_in):
    def wrapped(*refs):
        return kern(*refs[:n_in - 1], *refs[n_in:])
    return wrapped


def _pool_kernel(u0_ref, u1_ref, u2_ref, u3_ref, hist_ref, w_ref, scale_ref, o_ref, ext_ref, *, tt, pos0):
    ti = pl.program_id(1)
    halo = 16

    @pl.when(ti == 0)
    def _():
        ext_ref[0:halo, :] = hist_ref[...]

    @pl.when(ti > 0)
    def _():
        ext_ref[0:halo, :] = ext_ref[tt:tt + halo, :]

    u = jnp.concatenate([u0_ref[...], u1_ref[...], u2_ref[...], u3_ref[...]], axis=1)
    ext_ref[halo:halo + tt, :] = u
    pos = pos0 + ti * tt + lax.broadcasted_iota(I32, (tt, 1), 0)
    for g, w in enumerate(POOL_WINDOWS):
        cols = slice(g * POOL_GC, (g + 1) * POOL_GC)
        s = u[:, cols]
        for k in range(1, w):
            s = s + ext_ref[halo - k:halo - k + tt, cols]
        cnt = jnp.minimum(pos + 1, w).astype(F32)
        pooled = s / cnt - u[:, cols]
        y = _dot(pooled.astype(BF16), w_ref[g].astype(BF16))
        o_ref[:, cols] = y * scale_ref[:, cols]


def _pool_mixer(h_all, row0, batch, t, tt, hist16, w_pool, scale, pos0, prev_out):
    m = h_all.shape[0]
    nt = t // tt
    rb0 = row0 // tt
    tile0 = POOL_COL0 // POOL_GC
    in_specs = [pl.BlockSpec((tt, POOL_GC), lambda b, i, g=g: (rb0 + b * nt + i, tile0 + g))
                for g in range(len(POOL_WINDOWS))]
    in_specs += [pl.BlockSpec((None, 16, POOL_WIDTH), lambda b, i: (b, 0, 0)),
                pl.BlockSpec((len(POOL_WINDOWS), POOL_GC, POOL_GC), lambda b, i: (0, 0, 0)),
                pl.BlockSpec((1, POOL_WIDTH), lambda b, i: (0, 0))]
    args = [h_all] * len(POOL_WINDOWS) + [hist16, w_pool, scale]
    aliases = {}
    kern = functools.partial(_pool_kernel, tt=tt, pos0=pos0)
    if prev_out is not None:
        in_specs.append(pl.BlockSpec(memory_space=pl.ANY))
        args.append(prev_out)
        aliases = {len(args) - 1: 0}
        kern = _drop_last_input(kern, n_in=len(args))
    return pl.pallas_call(
        kern,
        grid=(batch, nt),
        in_specs=in_specs,
        out_specs=pl.BlockSpec((tt, POOL_WIDTH), lambda b, i: (rb0 + b * nt + i, 0)),
        out_shape=jax.ShapeDtypeStruct((m, POOL_WIDTH), F32),
        scratch_shapes=[pltpu.VMEM((tt + 16, POOL_WIDTH), F32)],
        input_output_aliases=aliases,
        compiler_params=_cparams(2),
        name="pool_mixer",
    )(*args)


def _rw_pre_kernel(*refs, tt):
    c_refs = refs[:RW_COL_TILES]
    (hist_ref, mu_ref, w0_ref, w2_ref, a0_ref, a2_ref, g2_ref,
     rkv_ref, lw_ref, a_ref, g_ref, last_ref) = refs[RW_COL_TILES:]
    ti = pl.program_id(1)

    @pl.when(ti == 0)
    def _():
        last_ref[...] = hist_ref[...]

    c = jnp.concatenate([ref[...] for ref in c_refs], axis=1)
    row = lax.broadcasted_iota(I32, (tt, 1), 0)
    prev = jnp.where(row == 0, last_ref[...], pltpu.roll(c, 1, 0))
    last_ref[...] = c[tt - 1:tt, :]
    xs = c + (prev - c) * mu_ref[...]
    rkv_ref[...] = xs[:, :3 * RW_WIDTH]
    lora = xs[:, 3 * RW_WIDTH:3 * RW_WIDTH + RW_LORA]
    wd = _dot(jnp.tanh(lora).astype(BF16), w2_ref[...].astype(BF16))
    sp = jax.nn.softplus(-(w0_ref[...] + wd))
    lw_ref[...] = -jnp.exp(-sp - 0.5)
    a_ref[...] = jax.nn.sigmoid(a0_ref[...] + _dot(lora.astype(BF16), a2_ref[...].astype(BF16)))
    gd = xs[:, 3 * RW_WIDTH + RW_LORA:]
    g_ref[...] = _dot(jax.nn.sigmoid(gd).astype(BF16), g2_ref[...].astype(BF16))


def _rw_pre(h_all, row0, batch, t, tt, shift_hist, mu, w0, w2p, a0, a2p, g2, prev_outs):
    m = h_all.shape[0]
    nt = t // tt
    rb0 = row0 // tt
    row_map = lambda b, i: (rb0 + b * nt + i, 0)
    const = lambda b, i: (0, 0)
    tile0 = RW_COL0 // RW_COL_TILE
    in_specs = [pl.BlockSpec((tt, RW_COL_TILE), lambda b, i, j=j: (rb0 + b * nt + i, tile0 + j))
                for j in range(RW_COL_TILES)]
    in_specs += [pl.BlockSpec((None, 1, RW_COLS), lambda b, i: (b, 0, 0)),
                pl.BlockSpec((1, RW_COLS), const),
                pl.BlockSpec((1, RW_WIDTH), const),
                pl.BlockSpec((RW_LORA, RW_WIDTH), const),
                pl.BlockSpec((1, RW_WIDTH), const),
                pl.BlockSpec((RW_LORA, RW_WIDTH), const),
                pl.BlockSpec((RW_GATE, RW_WIDTH), const)]
    args = [h_all] * RW_COL_TILES + [shift_hist, mu, w0, w2p, a0, a2p, g2]
    widths = (3 * RW_WIDTH, RW_WIDTH, RW_WIDTH, RW_WIDTH)
    aliases = {}
    kern = functools.partial(_rw_pre_kernel, tt=tt)
    if prev_outs is not None:
        n_real = len(args)
        for j, po in enumerate(prev_outs):
            in_specs.append(pl.BlockSpec(memory_space=pl.ANY))
            args.append(po)
            aliases[n_real + j] = j
        kern = _drop_inputs(kern, n_real, len(prev_outs))
    return pl.pallas_call(
        kern,
        grid=(batch, nt),
        in_specs=in_specs,
        out_specs=[pl.BlockSpec((tt, wd), row_map) for wd in widths],
        out_shape=[jax.ShapeDtypeStruct((m, wd), F32) for wd in widths],
        scratch_shapes=[pltpu.VMEM((1, RW_COLS), F32)],
        input_output_aliases=aliases,
        compiler_params=_cparams(2),
        name="rwkv_pre",
    )(*args)


def _drop_inputs(kern, n_real, n_drop):
    def wrapped(*refs):
        return kern(*refs[:n_real], *refs[n_real + n_drop:])
    return wrapped


def _head_ones():
    r = lax.broadcasted_iota(I32, (LANES, LANES), 0) // HEAD_DIM
    c = lax.broadcasted_iota(I32, (LANES, LANES), 1) // HEAD_DIM
    return (r == c).astype(BF16)


def _head_sum(x, ones_bd):
    hi, lo = _split2(x)
    return _dot(hi, ones_bd) + _dot(lo, ones_bd)


def _stack_heads(x):
    lane = lax.broadcasted_iota(I32, x.shape, 1)
    zero = jnp.zeros_like(x)
    return jnp.concatenate([jnp.where(lane < HEAD_DIM, x, zero), jnp.where(lane >= HEAD_DIM, x, zero)],
                           axis=0)


def _rw_scan_kernel(r_ref, k_ref, v_ref, lw_ref, a_ref, g_ref, kks_ref, ka_ref, rk_ref, gng_ref,
                    gnb_ref, s0_ref, o_ref, sout_ref, st_ref, o_s, lhs_s, ufree_s, prb_s, akvo_s, bkt_s,
                    v_s, e_s, *, tb, c):
    ti = pl.program_id(2)
    n = 2 * c
    nchunk = tb // c
    tiles = range(RW_TILES)
    lanes = [slice(p * LANES, (p + 1) * LANES) for p in tiles]

    @pl.when(ti == 0)
    def _():
        st_ref[...] = s0_ref[...]

    ones_bd = _head_ones()
    ri = lax.broadcasted_iota(I32, (tb, tb), 0)
    ci = lax.broadcasted_iota(I32, (tb, tb), 1)
    l_chunk = jnp.logical_and(ri >= ci, ri // c == ci // c).astype(BF16)
    v_all, e_in, al_all, be_all, kc_all, rc_all, bonus = [], [], [], [], [], [], []
    for p in tiles:
        r = r_ref[:, lanes[p]]
        k0 = k_ref[:, lanes[p]]
        v = v_ref[:, lanes[p]]
        lw = lw_ref[:, lanes[p]]
        a = a_ref[:, lanes[p]]
        kk = k0 * kks_ref[:, lanes[p]]
        kk = kk * lax.rsqrt(jnp.maximum(_head_sum(kk * kk, ones_bd), 1e-24))
        kmod = k0 * (1.0 + (a - 1.0) * ka_ref[:, lanes[p]])
        lw_hi, lw_lo = _split2(lw)
        gcum = _dot(l_chunk, lw_hi) + _dot(l_chunk, lw_lo)
        e = jnp.exp(gcum)
        e_neg = jnp.exp(-gcum)
        v_all.append(v)
        e_in.append(e)
        al_all.append((-kk * jnp.exp(gcum - lw)).astype(BF16))
        be_all.append((kk * a * e_neg).astype(BF16))
        kc_all.append((kmod * e_neg).astype(BF16))
        rc_all.append((r * e).astype(BF16))
        bonus.append(_head_sum(r * kmod * rk_ref[:, lanes[p]], ones_bd) * v)

    rn = lax.broadcasted_iota(I32, (n, n), 0)
    cn = lax.broadcasted_iota(I32, (n, n), 1)
    same_head = rn // c == cn // c
    strict = jnp.logical_and(same_head, cn < rn).astype(F32)
    incl = jnp.logical_and(same_head, cn <= rn).astype(F32)
    eye = (rn == cn).astype(F32)

    def prepare(group):
        each = lambda f, *lists: [f(*xs) for xs in zip(*lists)]
        rows = [(p, slice(j * c, (j + 1) * c)) for p, j in group]
        al = [_stack_heads(al_all[p][r_]) for p, r_ in rows]
        rc = [_stack_heads(rc_all[p][r_]) for p, r_ in rows]
        bk = [jnp.concatenate([_stack_heads(be_all[p][r_]), _stack_heads(kc_all[p][r_])], axis=0)
              for p, r_ in rows]
        vs = [_stack_heads(v_all[p][r_]).astype(BF16) for p, r_ in rows]
        m4 = each(lambda al_, rc_, bk_: _dot_nt(jnp.concatenate([al_, rc_], axis=0), bk_), al, rc, bk)
        a_ab = [m[:n, :n] * strict for m in m4]
        lower = [jnp.concatenate([m[:n, n:] * strict, m[n:, n:] * incl], axis=0).astype(BF16) for m in m4]
        p_rb = [(m[n:, :n] * incl).astype(BF16) for m in m4]
        akv = each(_dot, lower, vs)
        tinv = [eye + a_ for a_ in a_ab]
        p_bf = [a_.astype(BF16) for a_ in a_ab]
        p_bf = [_dot(p_, p_).astype(BF16) for p_ in p_bf]
        span = 2
        while span < c:
            if 2 * span < c:
                both = each(lambda p_, t_: _dot(p_, jnp.concatenate([t_.astype(BF16), p_], axis=1)),
                            p_bf, tinv)
                tinv = each(lambda t_, b_: t_ + b_[:, :n], tinv, both)
                p_bf = [b_[:, n:].astype(BF16) for b_ in both]
            else:
                tinv = each(lambda t_, p_: t_ + _dot(p_, t_.astype(BF16)), tinv, p_bf)
            span *= 2
        wu = each(lambda t_, al_, akv_: _dot(t_.astype(BF16),
                                             jnp.concatenate([al_, akv_[:n].astype(BF16)], axis=1)),
                  tinv, al, akv)
        for i, (p, j) in enumerate(group):
            lhs_s[p, j] = jnp.concatenate([wu[i][:, :LANES].astype(BF16), rc[i]], axis=0)
            ufree_s[p, j] = wu[i][:, LANES:]
            prb_s[p, j] = p_rb[i]
            akvo_s[p, j] = akv[i][n:]
            bkt_s[p, j] = bk[i].astype(F32).T.astype(BF16)
            v_s[p, j] = vs[i]
            e_last = e_in[p][(j + 1) * c - 1:(j + 1) * c, :]
            e_s[p, j] = jnp.broadcast_to(e_last, (LANES, LANES)).T

    per_group = max(1, RW_PREP_GROUP // RW_TILES)
    for j0 in range(0, nchunk, per_group):
        prepare([(p, j) for j in range(j0, min(j0 + per_group, nchunk)) for p in tiles])

    def advance(j, sts):
        rows = pl.ds(pl.multiple_of(j * c, c), c)
        ws = [_dot(lhs_s[p, j], sts[p].astype(BF16)) for p in tiles]
        u_bf = [(ws[p][:n] + ufree_s[p, j]).astype(BF16) for p in tiles]
        o_bd = [ws[p][n:] + _dot(prb_s[p, j], u_bf[p]) + akvo_s[p, j] for p in tiles]
        upd = [_dot(bkt_s[p, j], jnp.concatenate([u_bf[p], v_s[p, j]], axis=0)) for p in tiles]
        for p in tiles:
            o_s[rows, lanes[p]] = o_bd[p][:c] + o_bd[p][c:]
        return tuple(e_s[p, j] * (sts[p] + upd[p]) for p in tiles)

    sts = lax.fori_loop(0, nchunk, advance, tuple(st_ref[p] for p in tiles))
    for p in tiles:
        st_ref[p] = sts[p]
        sout_ref[p] = sts[p]
        o = o_s[:, lanes[p]]
        mu_o = _head_sum(o, ones_bd) * (1.0 / HEAD_DIM)
        oc = o - mu_o
        var_o = _head_sum(oc * oc, ones_bd) * (1.0 / HEAD_DIM)
        on = oc * lax.rsqrt(var_o + GN_EPS) * gng_ref[:, lanes[p]] + gnb_ref[:, lanes[p]]
        o_ref[:, lanes[p]] = (on + bonus[p]) * g_ref[:, lanes[p]]


def _rw_scan(rkv, lw, a, g, row0, batch, t, tb, c, params, wkv0, prev_out):
    m = rkv.shape[0]
    nt = t // tb
    rb0 = row0 // tb
    n = 2 * c
    nchunk = tb // c
    wide = RW_TILES * LANES
    groups = HEAD_PAIRS // RW_TILES

    def col(off):
        return lambda b, p, i: (rb0 + b * nt + i, off + p)

    s_t = jnp.swapaxes(wkv0, -1, -2).reshape(batch, HEAD_PAIRS, 2, HEAD_DIM, HEAD_DIM)
    st0 = jnp.einsum('bphkv,hg->bphkgv', s_t, jnp.eye(2, dtype=F32)).reshape(batch, HEAD_PAIRS, LANES, LANES)
    pspec = pl.BlockSpec((1, wide), lambda b, p, i: (0, p))
    sspec = pl.BlockSpec((None, RW_TILES, LANES, LANES), lambda b, p, i: (b, p, 0, 0))
    in_specs = [pl.BlockSpec((tb, wide), col(0)), pl.BlockSpec((tb, wide), col(groups)),
                pl.BlockSpec((tb, wide), col(2 * groups)),
                pl.BlockSpec((tb, wide), col(0)), pl.BlockSpec((tb, wide), col(0)),
                pl.BlockSpec((tb, wide), col(0))] + [pspec] * 5 + [sspec]
    args = [rkv, rkv, rkv, lw, a, g] + list(params) + [st0]
    aliases = {}
    kern = functools.partial(_rw_scan_kernel, tb=tb, c=c)
    if prev_out is not None:
        n_real = len(args)
        in_specs.append(pl.BlockSpec(memory_space=pl.ANY))
        args.append(prev_out)
        aliases = {n_real: 0}
        kern = _drop_inputs(kern, n_real, 1)
    per = (RW_TILES, nchunk)
    scratch = [pltpu.VMEM((RW_TILES, LANES, LANES), F32), pltpu.VMEM((tb, wide), F32)]
    scratch += [pltpu.VMEM(per + (2 * n, LANES), BF16), pltpu.VMEM(per + (n, LANES), F32),
                pltpu.VMEM(per + (n, n), BF16), pltpu.VMEM(per + (n, LANES), F32),
                pltpu.VMEM(per + (LANES, 2 * n), BF16), pltpu.VMEM(per + (n, LANES), BF16),
                pltpu.VMEM(per + (LANES, LANES), F32)]
    o_c, st = pl.pallas_call(
        kern,
        grid=(batch, groups, nt),
        in_specs=in_specs,
        out_specs=[pl.BlockSpec((tb, wide), col(0)), sspec],
        out_shape=[jax.ShapeDtypeStruct((m, RW_WIDTH), F32),
                   jax.ShapeDtypeStruct((batch, HEAD_PAIRS, LANES, LANES), F32)],
        scratch_shapes=scratch,
        input_output_aliases=aliases,
        compiler_params=_cparams(3),
        name="rwkv_scan",
    )(*args)
    st = st.reshape(batch, HEAD_PAIRS, 2, HEAD_DIM, 2, HEAD_DIM)
    wkv = jnp.stack([st[:, :, 0, :, 0, :], st[:, :, 1, :, 1, :]], axis=2)
    return o_c, jnp.swapaxes(wkv, -1, -2).reshape(batch, RW_HEADS, HEAD_DIM, HEAD_DIM)


def _outproj_kernel(oa_ref, ob_ref, oc_ref, x_ref, w_ref, g_ref, b_ref, y_ref):
    mix = _dot(oa_ref[...].astype(BF16), w_ref[0:SB_WIDTH, :])
    mix += _dot(ob_ref[...].astype(BF16), w_ref[SB_WIDTH:SB_WIDTH + POOL_WIDTH, :])
    mix += _dot(oc_ref[...].astype(BF16), w_ref[SB_WIDTH + POOL_WIDTH:, :])
    y_ref[...] = _layer_norm(DN_ALPHA * x_ref[...] + mix, g_ref[...], b_ref[...])


def _outproj_ln(oa, ob, oc, x, w_bf, g, b, tm):
    m, d = x.shape
    row = lambda i: (i, 0)
    const = lambda i: (0, 0)
    return pl.pallas_call(
        _outproj_kernel,
        grid=(m // tm,),
        in_specs=[pl.BlockSpec((tm, SB_WIDTH), row), pl.BlockSpec((tm, POOL_WIDTH), row),
                  pl.BlockSpec((tm, RW_WIDTH), row), pl.BlockSpec((tm, d), row),
                  pl.BlockSpec(w_bf.shape, const), pl.BlockSpec((1, d), const),
                  pl.BlockSpec((1, d), const)],
        out_specs=pl.BlockSpec((tm, d), row),
        out_shape=jax.ShapeDtypeStruct((m, d), F32),
        compiler_params=_cparams(1),
        name="outproj_ln",
    )(oa, ob, oc, x, w_bf, g, b)


def _router_kernel(x_ref, w_ref, b_ref, idx_ref, gate_ref, rank_ref, cnt_ref, run_ref, *, tm):
    @pl.when(pl.program_id(0) == 0)
    def _():
        run_ref[...] = jnp.zeros_like(run_ref)

    logits = _dot(x_ref[...].astype(BF16), w_ref[...].astype(BF16)) + b_ref[...]
    lane = lax.broadcasted_iota(I32, (tm, N_EXPERTS), 1).astype(F32)
    work = logits
    vals, idxs = [], []
    for _ in range(TOP_K):
        top = jnp.max(work, axis=1, keepdims=True)
        idx = jnp.min(jnp.where(work == top, lane, float(N_EXPERTS)), axis=1, keepdims=True)
        vals.append(top)
        idxs.append(idx)
        work = jnp.where(lane == idx, -jnp.inf, work)
    exps = [jnp.exp(v - vals[0]) for v in vals]
    total = exps[0] + exps[1] + exps[2] + exps[3]
    out_lane = lax.broadcasted_iota(I32, (tm, LANES), 1)
    lane_f = out_lane.astype(F32)
    onehots = [(lane_f == idxs[k]).astype(F32) for k in range(TOP_K)]
    chosen = onehots[0] + onehots[1] + onehots[2] + onehots[3]
    earlier = (lax.broadcasted_iota(I32, (tm, tm), 1) < lax.broadcasted_iota(I32, (tm, tm), 0)).astype(BF16)
    base = run_ref[...] + _dot(earlier, chosen.astype(BF16))
    run_ref[...] = run_ref[...] + jnp.sum(chosen, axis=0, keepdims=True)
    cnt_ref[...] = run_ref[...].astype(I32)
    idx_out = jnp.zeros((tm, LANES), F32)
    gate_out = jnp.zeros((tm, LANES), F32)
    rank_out = jnp.zeros((tm, LANES), F32)
    for k in range(TOP_K):
        idx_out = jnp.where(out_lane == k, idxs[k], idx_out)
        gate_out = jnp.where(out_lane == k, exps[k] / total, gate_out)
        rank_out = jnp.where(out_lane == k, jnp.sum(onehots[k] * base, axis=1, keepdims=True), rank_out)
    idx_ref[...] = idx_out.astype(I32)
    gate_ref[...] = gate_out
    rank_ref[...] = rank_out.astype(I32)


def _router(x, w, b, tm):
    m, d = x.shape
    return pl.pallas_call(
        functools.partial(_router_kernel, tm=tm),
        grid=(m // tm,),
        in_specs=[pl.BlockSpec((tm, d), lambda i: (i, 0)),
                  pl.BlockSpec((d, N_EXPERTS), lambda i: (0, 0)),
                  pl.BlockSpec((1, N_EXPERTS), lambda i: (0, 0))],
        out_specs=[pl.BlockSpec((tm, LANES), lambda i: (i, 0))] * 3 + [pl.BlockSpec((1, LANES), lambda i: (0, 0))],
        out_shape=[jax.ShapeDtypeStruct((m, LANES), I32), jax.ShapeDtypeStruct((m, LANES), F32),
                   jax.ShapeDtypeStruct((m, LANES), I32), jax.ShapeDtypeStruct((1, LANES), I32)],
        scratch_shapes=[pltpu.VMEM((1, LANES), F32)],
        compiler_params=_cparams(1),
        name="router",
    )(x, w, b)


def _row_copy(src_hbm, dst_ref, sem, src_row, dst_row):
    return pltpu.make_async_copy(src_hbm.at[pl.ds(src_row, 1), :], dst_ref.at[pl.ds(dst_row, 1), :], sem)


GATHER_UNROLL = 8


def _gather_kernel(tok_ref, tok_next_ref, nu_ref, x_hbm, o_ref, rows_ref, sem, *, bm):
    i = pl.program_id(0)
    n_used = nu_ref[0]

    def start_block(toks, slot):
        def body(k, carry):
            for u in range(GATHER_UNROLL):
                r = k * GATHER_UNROLL + u
                _row_copy(x_hbm, rows_ref.at[slot], sem.at[slot], toks[0, 0, r], r).start(priority=u % 2)
            return carry

        lax.fori_loop(0, bm // GATHER_UNROLL, body, 0)

    @pl.when(i == 0)
    def _():
        start_block(tok_ref, 0)

    @pl.when(i + 1 < n_used)
    def _():
        start_block(tok_next_ref, (i + 1) % 2)

    @pl.when(i < n_used)
    def _():
        slot = i % 2

        def wait(k, carry):
            for u in range(GATHER_UNROLL):
                _row_copy(x_hbm, rows_ref.at[slot], sem.at[slot], 0, k * GATHER_UNROLL + u).wait()
            return carry

        lax.fori_loop(0, bm // GATHER_UNROLL, wait, 0)
        o_ref[...] = rows_ref[slot].astype(BF16)


def _gather_rows(x, buf_tok, n_used, bm):
    d = x.shape[1]
    nb = buf_tok.shape[0] // bm
    toks = buf_tok.reshape(nb, 1, bm)
    return pl.pallas_call(
        functools.partial(_gather_kernel, bm=bm),
        grid=(nb,),
        in_specs=[pl.BlockSpec((1, 1, bm), lambda i: (i, 0, 0), memory_space=pltpu.SMEM),
                  pl.BlockSpec((1, 1, bm), lambda i: (jnp.minimum(i + 1, nb - 1), 0, 0),
                               memory_space=pltpu.SMEM),
                  pl.BlockSpec(memory_space=pltpu.SMEM),
                  pl.BlockSpec(memory_space=pl.ANY)],
        out_specs=pl.BlockSpec((bm, d), lambda i: (i, 0)),
        out_shape=jax.ShapeDtypeStruct((nb * bm, d), BF16),
        scratch_shapes=[pltpu.VMEM((2, bm, d), F32), pltpu.SemaphoreType.DMA((2,))],
        compiler_params=_cparams(1),
        name="moe_gather",
    )(toks, toks, n_used, x)


def _step_table(block_e, n_used):
    nb = block_e.shape[0]
    i = jnp.arange(nb, dtype=I32)
    valid = i < n_used[0]
    first = jnp.logical_and(valid, jnp.logical_or(i == 0, block_e != jnp.roll(block_e, 1)))
    first_pos = jnp.where(first, i, nb)
    next_pos = jnp.concatenate([lax.cummin(first_pos, reverse=True)[1:], jnp.full((1,), nb, I32)])
    next_e = jnp.where(next_pos < nb, block_e[jnp.minimum(next_pos, nb - 1)], -1)
    return jnp.stack([block_e, first.astype(I32), next_e.astype(I32), jnp.zeros_like(block_e)],
                     axis=1).reshape(nb * 4)


def _weight_copies(w_hbm, wbuf, sem, layer, expert, cols, tn):
    return [pltpu.make_async_copy(w_hbm.at[layer, expert, :, pl.ds(pl.multiple_of(col, tn), tn)],
                                  wbuf.at[h], sem.at[h]) for h, col in enumerate(cols)]


def _fetch_expert_tile(tbl_ref, w_hbm, wbuf, wbf, sem, layer, cols, tn):
    i = pl.program_id(1)
    expert = tbl_ref[4 * i]
    next_expert = tbl_ref[4 * i + 2]

    @pl.when(tbl_ref[4 * i + 1] == 1)
    def _():
        @pl.when(i == 0)
        def _():
            for cp in _weight_copies(w_hbm, wbuf, sem, layer, expert, cols, tn):
                cp.start()

        for cp in _weight_copies(w_hbm, wbuf, sem, layer, expert, cols, tn):
            cp.wait()
        for h in range(len(cols)):
            wbf[h] = wbuf[h].astype(BF16)

        @pl.when(next_expert >= 0)
        def _():
            for cp in _weight_copies(w_hbm, wbuf, sem, layer, next_expert, cols, tn):
                cp.start()


def _expert_up_kernel(tbl_ref, nu_ref, x_ref, w_hbm, bg_ref, bl_ref, act_ref, wbuf, wbf, sem, *, layer, tn):
    j = pl.program_id(0)
    nj = pl.num_programs(0)

    @pl.when(pl.program_id(1) < nu_ref[0])
    def _():
        _fetch_expert_tile(tbl_ref, w_hbm, wbuf, wbf, sem, layer, (j * tn, (nj + j) * tn), tn)
        x = x_ref[...]
        h_glu = _dot(x, wbf[0]) + bg_ref[...]
        h_lin = _dot(x, wbf[1]) + bl_ref[...]
        h_glu = jnp.minimum(h_glu, SWIGLU_LIMIT)
        h_lin = jnp.clip(h_lin, -SWIGLU_LIMIT, SWIGLU_LIMIT)
        act = h_glu * jax.nn.sigmoid(SWIGLU_ALPHA * h_glu) * (h_lin + 1.0)
        act_ref[...] = act.astype(BF16)


def _expert_up(xb, w1, b1, layer, tbl, n_used, bm, tn):
    rows, d = xb.shape
    de = w1.shape[3] // 2
    nj = de // tn
    nb = rows // bm

    def blk(i, nu):
        return jnp.minimum(i, nu[0] - 1)

    grid_spec = pltpu.PrefetchScalarGridSpec(
        num_scalar_prefetch=2,
        grid=(nj, nb),
        in_specs=[pl.BlockSpec((bm, d), lambda j, i, tb, nu: (blk(i, nu), 0)),
                  pl.BlockSpec(memory_space=pl.ANY),
                  pl.BlockSpec((None, None, 1, tn), lambda j, i, tb, nu: (layer, tb[4 * blk(i, nu)], 0, j)),
                  pl.BlockSpec((None, None, 1, tn),
                               lambda j, i, tb, nu: (layer, tb[4 * blk(i, nu)], 0, nj + j))],
        out_specs=pl.BlockSpec((bm, tn), lambda j, i, tb, nu: (blk(i, nu), j)),
        scratch_shapes=[pltpu.VMEM((2, d, tn), F32), pltpu.VMEM((2, d, tn), BF16),
                        pltpu.SemaphoreType.DMA((2,))],
    )
    return pl.pallas_call(
        functools.partial(_expert_up_kernel, layer=layer, tn=tn),
        grid_spec=grid_spec,
        out_shape=jax.ShapeDtypeStruct((rows, de), BF16),
        compiler_params=_cparams(2),
        name="moe_up",
    )(tbl, n_used, xb, w1, b1, b1)


def _expert_down_kernel(tbl_ref, nu_ref, a_ref, w_hbm, b_ref, y_ref, wbuf, wbf, sem, *, layer, tn):
    j = pl.program_id(0)

    @pl.when(pl.program_id(1) < nu_ref[0])
    def _():
        _fetch_expert_tile(tbl_ref, w_hbm, wbuf, wbf, sem, layer, (j * tn,), tn)
        y_ref[...] = _dot(a_ref[...], wbf[0]) + b_ref[...]


def _expert_down(act, w2, b2, layer, tbl, n_used, bm, tn):
    rows, de = act.shape
    d = w2.shape[3]
    nj = d // tn
    nb = rows // bm

    def blk(i, nu):
        return jnp.minimum(i, nu[0] - 1)

    grid_spec = pltpu.PrefetchScalarGridSpec(
        num_scalar_prefetch=2,
        grid=(nj, nb),
        in_specs=[pl.BlockSpec((bm, de), lambda j, i, tb, nu: (blk(i, nu), 0)),
                  pl.BlockSpec(memory_space=pl.ANY),
                  pl.BlockSpec((None, None, 1, tn), lambda j, i, tb, nu: (layer, tb[4 * blk(i, nu)], 0, j))],
        out_specs=pl.BlockSpec((bm, tn), lambda j, i, tb, nu: (blk(i, nu), j)),
        scratch_shapes=[pltpu.VMEM((1, de, tn), F32), pltpu.VMEM((1, de, tn), BF16),
                        pltpu.SemaphoreType.DMA((1,))],
    )
    return pl.pallas_call(
        functools.partial(_expert_down_kernel, layer=layer, tn=tn),
        grid_spec=grid_spec,
        out_shape=jax.ShapeDtypeStruct((rows, d), F32),
        compiler_params=_cparams(2),
        name="moe_down",
    )(tbl, n_used, act, w2, b2)


def _combine_kernel(dest_ref, dest_next_ref, y_hbm, gate_ref, x_ref, g_ref, b_ref, o_ref, rows_ref, sem,
                    *, tm):
    i = pl.program_id(0)

    def start_block(dests, slot):
        def body(r, carry):
            for k in range(TOP_K):
                _row_copy(y_hbm, rows_ref.at[slot, k], sem.at[slot], dests[0, 0, r * TOP_K + k],
                          r).start(priority=k % 2)
            return carry

        lax.fori_loop(0, tm, body, 0)

    @pl.when(i == 0)
    def _():
        start_block(dest_ref, 0)

    @pl.when(i + 1 < pl.num_programs(0))
    def _():
        start_block(dest_next_ref, (i + 1) % 2)

    slot = i % 2

    def wait(r, carry):
        for k in range(TOP_K):
            _row_copy(y_hbm, rows_ref.at[slot, k], sem.at[slot], 0, r).wait()
        return carry

    lax.fori_loop(0, tm, wait, 0)
    gates = gate_ref[...]
    ffn = gates[:, 0:1] * rows_ref[slot, 0]
    for k in range(1, TOP_K):
        ffn += gates[:, k:k + 1] * rows_ref[slot, k]
    o_ref[...] = _layer_norm(DN_ALPHA * x_ref[...] + ffn, g_ref[...], b_ref[...])


def _combine_ln(y, dest, gates, x, g, b, tm):
    m, d = x.shape
    nb = m // tm
    dests = dest.reshape(nb, 1, tm * TOP_K)
    return pl.pallas_call(
        functools.partial(_combine_kernel, tm=tm),
        grid=(nb,),
        in_specs=[pl.BlockSpec((1, 1, tm * TOP_K), lambda i: (i, 0, 0), memory_space=pltpu.SMEM),
                  pl.BlockSpec((1, 1, tm * TOP_K), lambda i: (jnp.minimum(i + 1, nb - 1), 0, 0),
                               memory_space=pltpu.SMEM),
                  pl.BlockSpec(memory_space=pl.ANY),
                  pl.BlockSpec((tm, LANES), lambda i: (i, 0)),
                  pl.BlockSpec((tm, d), lambda i: (i, 0)),
                  pl.BlockSpec((1, d), lambda i: (0, 0)),
                  pl.BlockSpec((1, d), lambda i: (0, 0))],
        out_specs=pl.BlockSpec((tm, d), lambda i: (i, 0)),
        out_shape=jax.ShapeDtypeStruct((m, d), F32),
        scratch_shapes=[pltpu.VMEM((2, TOP_K, tm, d), F32), pltpu.SemaphoreType.DMA((2,))],
        compiler_params=_cparams(1),
        name="moe_combine_ln",
    )(dests, dests, y, gates, x, g, b)


def _routing_tables(top_i, rank, counts, bm):
    n = top_i.shape[0]
    nk = n * TOP_K
    padded = (counts + bm - 1) // bm * bm
    pad_end = jnp.cumsum(padded)
    pad_start = pad_end - padded
    dest = (pad_start[top_i] + rank).astype(I32).reshape(nk)
    nb = (nk + N_EXPERTS * (bm - 1)) // bm
    buf_tok = jnp.zeros((nb * bm,), I32).at[dest].set(jnp.arange(nk, dtype=I32) // TOP_K)
    block_row = jnp.arange(nb, dtype=I32)[:, None] * bm
    block_e = jnp.minimum(jnp.sum((pad_end[None, :] <= block_row).astype(I32), axis=1), N_EXPERTS - 1)
    n_used = (pad_end[-1:] // bm).astype(I32)
    return dest, buf_tok, _step_table(block_e, n_used), n_used


def _moe_ln(x, lp, big, layer):
    idx, gates, rank, counts = _router(x, lp['w_router'], lp['b_router'].reshape(1, N_EXPERTS), tm=256)
    dest, buf_tok, steps, n_used = _routing_tables(idx[:, :TOP_K], rank[:, :TOP_K],
                                                     counts[0, :N_EXPERTS], MOE_BM)
    xb = _gather_rows(x, buf_tok, n_used, MOE_BM)
    nl, e, d, de2 = big['w_e1'].shape
    act = _expert_up(xb, big['w_e1'], big['b_e1'].reshape(nl, e, 1, de2), layer, steps, n_used,
                     MOE_BM, tn=1024)
    y = _expert_down(act, big['w_e2'], big['b_e2'].reshape(nl, e, 1, d), layer, steps, n_used,
                     MOE_BM, tn=d)
    return _combine_ln(y, dest, gates, x, lp['ln_ffn_g'].reshape(1, d), lp['ln_ffn_b'].reshape(1, d),
                       tm=128)


def _layer(x, lp, big, layer, st, n_p, bp, tp, bs, ts):
    d = x.shape[1]
    h = _inproj(x, big['w_in'], layer, tm=768, tn=768)
    qkv_w = 3 * SB_WIDTH

    o_a = _sb_attention(h, 0, bp, tp, 256, None, None)
    hist = (st['cache_k'].reshape(bs, -1, SB_WIDTH), st['cache_v'].reshape(bs, -1, SB_WIDTH))
    o_a = _sb_attention(h, n_p, bs, ts, ts, hist, o_a)
    past = hist[0].shape[1]

    w_pool = lp['pool_w']
    scale = lp['pool_scale'].reshape(1, POOL_WIDTH)
    o_b = _pool_mixer(h, 0, bp, tp, 512, jnp.zeros((bp, 16, POOL_WIDTH), F32), w_pool, scale, 0, None)
    hist16 = jnp.pad(st['state_pool'], ((0, 0), (1, 0), (0, 0)))
    o_b = _pool_mixer(h, n_p, bs, ts, ts, hist16, w_pool, scale, past, o_b)

    zero_lora = jnp.zeros((RW_LORA // 2, RW_WIDTH), F32)
    w2p = jnp.concatenate([lp['rw_w2'], zero_lora], axis=0)
    a2p = jnp.concatenate([zero_lora, lp['rw_a2']], axis=0)
    pre_args = (lp['rw_mu'].reshape(1, RW_COLS), lp['rw_w0'].reshape(1, RW_WIDTH), w2p,
                lp['rw_a0'].reshape(1, RW_WIDTH), a2p, lp['rw_g2'])
    pre = _rw_pre(h, 0, bp, tp, 256, jnp.zeros((bp, 1, RW_COLS), F32), *pre_args, None)
    rkv, lw, a, g = _rw_pre(h, n_p, bs, ts, ts, st['state_shift'], *pre_args, pre)
    prm = tuple(lp[k].reshape(1, RW_WIDTH) for k in ('rw_kk', 'rw_ka', 'rw_rk', 'rw_gn_g', 'rw_gn_b'))
    o_c, wkv_p = _rw_scan(rkv, lw, a, g, 0, bp, tp, 512, 64, prm,
                          jnp.zeros((bp, RW_HEADS, HEAD_DIM, HEAD_DIM), F32), None)
    o_c, wkv_s = _rw_scan(rkv, lw, a, g, n_p, bs, ts, ts, ts, prm, st['state_wkv'], o_c)

    x1 = _outproj_ln(o_a, o_b, o_c, x, lp['w_out'].astype(BF16), lp['ln_mix_g'].reshape(1, d),
                     lp['ln_mix_b'].reshape(1, d), tm=256)
    x2 = _moe_ln(x1, lp, big, layer)

    def states(r0, b, t, wkv):
        r1 = r0 + b * t
        k_new = h[r0:r1, SB_WIDTH:2 * SB_WIDTH].reshape(b, t, SB_HEADS, HEAD_DIM)
        v_new = h[r0:r1, 2 * SB_WIDTH:qkv_w].reshape(b, t, SB_HEADS, HEAD_DIM)
        pool_new = jnp.stack([h[r0 + (i + 1) * t - POOL_HIST:r0 + (i + 1) * t, qkv_w:qkv_w + POOL_WIDTH]
                              for i in range(b)])
        shift_new = h[r0 + t - 1:r1:t, qkv_w + POOL_WIDTH:].reshape(b, 1, RW_COLS)
        return k_new, v_new, pool_new, shift_new, wkv

    return x2, states(0, bp, tp, wkv_p), states(n_p, bs, ts, wkv_s)


def kernel(x_prompt, x_sample, cache_k, cache_v, state_pool, state_shift, state_wkv, w_in, w_out, ln_mix_g, ln_mix_b, ln_ffn_g, ln_ffn_b, pool_w, pool_scale, rw_mu, rw_w0, rw_w2, rw_a0, rw_a2, rw_g2, rw_kk, rw_ka, rw_rk, rw_gn_g, rw_gn_b, w_router, b_router, w_e1, b_e1, w_e2, b_e2):
    bp, tp, d = x_prompt.shape
    bs, ts, _ = x_sample.shape
    n_p = bp * tp
    x = jnp.concatenate([x_prompt.reshape(n_p, d), x_sample.reshape(bs * ts, d)], axis=0)
    big = dict(w_in=w_in, w_e1=w_e1, b_e1=b_e1, w_e2=w_e2, b_e2=b_e2)
    weights = dict(w_out=w_out, ln_mix_g=ln_mix_g, ln_mix_b=ln_mix_b, ln_ffn_g=ln_ffn_g,
                   ln_ffn_b=ln_ffn_b, pool_w=pool_w, pool_scale=pool_scale, rw_mu=rw_mu, rw_w0=rw_w0,
                   rw_w2=rw_w2, rw_a0=rw_a0, rw_a2=rw_a2, rw_g2=rw_g2, rw_kk=rw_kk, rw_ka=rw_ka,
                   rw_rk=rw_rk, rw_gn_g=rw_gn_g, rw_gn_b=rw_gn_b, w_router=w_router,
                   b_router=b_router)
    st_p, st_s = [], []
    for l in range(w_in.shape[0]):
        lp = {k: v[l] for k, v in weights.items()}
        st = dict(cache_k=cache_k[l], cache_v=cache_v[l], state_pool=state_pool[l],
                  state_shift=state_shift[l], state_wkv=state_wkv[l])
        x, sp, ss = _layer(x, lp, big, l, st, n_p, bp, tp, bs, ts)
        st_p.append(sp)
        st_s.append(ss)
    y_prompt = x[:n_p].reshape(bp, tp, d)
    y_sample = x[n_p:].reshape(bs, ts, d)
    k_p, v_p, pool_p, shift_p, wkv_p = (jnp.stack([s[i] for s in st_p]) for i in range(5))
    k_s, v_s, pool_s, shift_s, wkv_s = (jnp.stack([s[i] for s in st_s]) for i in range(5))
    return (y_prompt, y_sample, k_p, v_p, pool_p, shift_p, wkv_p, k_s, v_s, pool_s, shift_s, wkv_s)
```

```python
import functools

import jax
import jax.numpy as jnp
from jax import lax
from jax.experimental import pallas as pl
from jax.experimental.pallas import tpu as pltpu

F32 = jnp.float32
BF16 = jnp.bfloat16
I32 = jnp.int32

HEAD_DIM = 64
SB_HEADS = 12
SB_WIDTH = SB_HEADS * HEAD_DIM
POOL_WINDOWS = (2, 4, 8, 16)
POOL_GC = 128
POOL_WIDTH = POOL_GC * len(POOL_WINDOWS)
POOL_HIST = max(POOL_WINDOWS) - 1
RW_HEADS = 12
RW_WIDTH = RW_HEADS * HEAD_DIM
RW_LORA = 128
RW_GATE = 128
RW_COLS = 3 * RW_WIDTH + RW_LORA + RW_GATE
N_EXPERTS = 32
TOP_K = 4
SWIGLU_LIMIT = 7.0
SWIGLU_ALPHA = 1.702
DEPTH = 2
DN_ALPHA = (2.0 * DEPTH) ** 0.25
LN_EPS = 1e-5
GN_EPS = 64e-5

LANES = 128
HEAD_PAIRS = SB_WIDTH // LANES
VMEM_LIMIT = 56 * 1024 * 1024
ATT_LOG_FLOOR = -110.0

POOL_COL0 = 3 * SB_WIDTH
RW_COL0 = POOL_COL0 + POOL_WIDTH
RW_COL_TILE = 256
RW_COL_TILES = RW_COLS // RW_COL_TILE
MOE_BM = 256
RW_TILES = 2
RW_PREP_GROUP = 8


def _cparams(n_axes):
    return pltpu.CompilerParams(dimension_semantics=("arbitrary",) * n_axes,
                                vmem_limit_bytes=VMEM_LIMIT)


def _dot(a, b, precision=None):
    return jnp.dot(a, b, preferred_element_type=F32, precision=precision)


def _dot_nt(a, b, precision=None):
    return lax.dot_general(a, b, (((1,), (1,)), ((), ())), preferred_element_type=F32,
                           precision=precision)


def _dot_tn(a, b, precision=None):
    return lax.dot_general(a, b, (((0,), (0,)), ((), ())), preferred_element_type=F32,
                           precision=precision)


def _layer_norm(x, g, b):
    mu = jnp.mean(x, axis=-1, keepdims=True)
    xc = x - mu
    var = jnp.mean(xc * xc, axis=-1, keepdims=True)
    return xc * lax.rsqrt(var + LN_EPS) * g + b


def _inproj_kernel(x_ref, w_ref, o_ref, wbf_ref):
    @pl.when(pl.program_id(1) == 0)
    def _():
        wbf_ref[...] = w_ref[...].astype(BF16)

    o_ref[...] = _dot(x_ref[...].astype(BF16), wbf_ref[...])


def _inproj(x, w, layer, tm, tn):
    m, k = x.shape
    n = w.shape[2]
    return pl.pallas_call(
        _inproj_kernel,
        grid=(n // tn, m // tm),
        in_specs=[pl.BlockSpec((tm, k), lambda j, i: (i, 0)),
                  pl.BlockSpec((None, k, tn), lambda j, i: (layer, 0, j))],
        out_specs=pl.BlockSpec((tm, tn), lambda j, i: (i, j)),
        out_shape=jax.ShapeDtypeStruct((m, n), F32),
        scratch_shapes=[pltpu.VMEM((k, tn), BF16)],
        compiler_params=_cparams(2),
        name="inproj",
    )(x, w)


def _split2(a):
    hi = a.astype(BF16)
    lo = (a - hi.astype(F32)).astype(BF16)
    return hi, lo


def _suffix_matrix(n):
    return (lax.broadcasted_iota(I32, (n, n), 0) > lax.broadcasted_iota(I32, (n, n), 1)).astype(BF16)


def _sb_block(qhs, k_bf, v_bf, suffix, carries, mask):
    heads = range(2)
    z = [_dot_nt(qhs[h], k_bf) for h in heads]
    sp = [jnp.maximum(z_, 0.0) + jnp.log1p(jnp.exp(-jnp.abs(z_))) for z_ in z]
    log_stay = [-s_ if mask is None else jnp.where(mask, -s_, 0.0) for s_ in sp]
    parts = [_split2(l_) for l_ in log_stay]
    hi = [_dot(parts[h][0], suffix) for h in heads]
    lo = [_dot(parts[h][1], suffix) for h in heads]
    after = [hi[h] + lo[h] + carries[h] for h in heads]
    att = [jnp.exp(z[h] - sp[h] + after[h]) for h in heads]
    if mask is not None:
        att = [jnp.where(mask, a_, 0.0) for a_ in att]
    pv = [_dot(att[h].astype(BF16), v_bf) for h in heads]
    return pv, [after[h][:, 0:1] + log_stay[h][:, 0:1] for h in heads]


def _sb_live(carries):
    return (jnp.max(jnp.maximum(carries[0], carries[1])) > ATT_LOG_FLOOR).astype(I32)


def _sb_sweep(qhs, k_ref, v_ref, tk, n_blocks, suffix, live, accs, carries):
    def cond(s):
        return jnp.logical_and(s[0] < n_blocks, s[1] > 0)

    def body(s):
        step, _, accs, carries = s
        rows = pl.ds(pl.multiple_of((n_blocks - 1 - step) * tk, tk), tk)
        k_bf = k_ref[rows, :].astype(BF16)
        v_bf = v_ref[rows, :].astype(BF16)
        pv, carries = _sb_block(qhs, k_bf, v_bf, suffix, carries, None)
        accs = tuple(accs[h] + pv[h] for h in range(2))
        return step + 1, _sb_live(carries), accs, tuple(carries)

    _, live, accs, carries = lax.while_loop(cond, body, (jnp.int32(0), live, accs, carries))
    return live, accs, carries


def _sb_kernel(*refs, tq, th, past):
    if past:
        q_ref, kn_ref, vn_ref, kh_ref, vh_ref, o_ref, kbf_ref, vbf_ref = refs
    else:
        q_ref, kn_ref, vn_ref, o_ref, kbf_ref, vbf_ref = refs
    qi = pl.program_id(2)

    @pl.when(qi == 0)
    def _():
        kbf_ref[...] = kn_ref[...].astype(BF16)
        vbf_ref[...] = vn_ref[...].astype(BF16)

    q = q_ref[...] * (HEAD_DIM ** -0.5)
    lane = lax.broadcasted_iota(I32, (tq, LANES), 1)
    qhs = [jnp.where(lane < HEAD_DIM, q, 0.0).astype(BF16),
           jnp.where(lane >= HEAD_DIM, q, 0.0).astype(BF16)]
    suffix_new = _suffix_matrix(tq)
    diag_mask = lax.broadcasted_iota(I32, (tq, tq), 1) < lax.broadcasted_iota(I32, (tq, tq), 0)
    rows = pl.ds(pl.multiple_of(qi * tq, tq), tq)
    k_bf = kbf_ref[rows, :]
    v_bf = vbf_ref[rows, :]
    accs, carries = _sb_block(qhs, k_bf, v_bf, suffix_new, [jnp.zeros((tq, 1), F32)] * 2, diag_mask)
    accs, carries = tuple(accs), tuple(carries)
    live, accs, carries = _sb_sweep(qhs, kbf_ref, vbf_ref, tq, qi, suffix_new, _sb_live(carries),
                                    accs, carries)
    if past:
        _, accs, carries = _sb_sweep(qhs, kh_ref, vh_ref, th, past // th, _suffix_matrix(th), live,
                                     accs, carries)
    o_ref[...] = jnp.where(lane < HEAD_DIM, accs[0], accs[1])


def _sb_attention(h_all, row0, batch, t, tq, hist, prev_out):
    m = h_all.shape[0]
    nq = t // tq
    rb0 = row0 // tq
    sb0 = row0 // t
    in_specs = [pl.BlockSpec((tq, LANES), lambda b, p, i: (rb0 + b * nq + i, p)),
                pl.BlockSpec((t, LANES), lambda b, p, i: (sb0 + b, HEAD_PAIRS + p)),
                pl.BlockSpec((t, LANES), lambda b, p, i: (sb0 + b, 2 * HEAD_PAIRS + p))]
    args = [h_all, h_all, h_all]
    past, th = 0, 0
    if hist is not None:
        past = hist[0].shape[1]
        th = min(past, 512)
        in_specs += [pl.BlockSpec((None, past, LANES), lambda b, p, i: (b, 0, p))] * 2
        args += list(hist)
    aliases = {}
    if prev_out is not None:
        in_specs.append(pl.BlockSpec(memory_space=pl.ANY))
        args.append(prev_out)
        aliases = {len(args) - 1: 0}
    kern = functools.partial(_sb_kernel, tq=tq, th=th, past=past)
    if prev_out is not None:
        kern = _drop_last_input(kern, n_in=len(args))
    return pl.pallas_call(
        kern,
        grid=(batch, HEAD_PAIRS, nq),
        in_specs=in_specs,
        out_specs=pl.BlockSpec((tq, LANES), lambda b, p, i: (rb0 + b * nq + i, p)),
        out_shape=jax.ShapeDtypeStruct((m, SB_WIDTH), F32),
        scratch_shapes=[pltpu.VMEM((t, LANES), BF16)] * 2,
        input_output_aliases=aliases,
        compiler_params=_cparams(3),
        name="sb_attention",
    )(*args)


def _drop_last_input(kern, n_in):
    def wrapped(*refs):
        return kern(*refs[:n_in - 1], *refs[n_in:])
    return wrapped


def _pool_kernel(u0_ref, u1_ref, u2_ref, u3_ref, hist_ref, w_ref, scale_ref, o_ref, ext_ref, *, tt, pos0):
    ti = pl.program_id(1)
    halo = 16

    @pl.when(ti == 0)
    def _():
        ext_ref[0:halo, :] = hist_ref[...]

    @pl.when(ti > 0)
    def _():
        ext_ref[0:halo, :] = ext_ref[tt:tt + halo, :]

    u = jnp.concatenate([u0_ref[...], u1_ref[...], u2_ref[...], u3_ref[...]], axis=1)
    ext_ref[halo:halo + tt, :] = u
    pos = pos0 + ti * tt + lax.broadcasted_iota(I32, (tt, 1), 0)
    for g, w in enumerate(POOL_WINDOWS):
        cols = slice(g * POOL_GC, (g + 1) * POOL_GC)
        s = u[:, cols]
        for k in range(1, w):
            s = s + ext_ref[halo - k:halo - k + tt, cols]
        cnt = jnp.minimum(pos + 1, w).astype(F32)
        pooled = s / cnt - u[:, cols]
        y = _dot(pooled.astype(BF16), w_ref[g].astype(BF16))
        o_ref[:, cols] = y * scale_ref[:, cols]


def _pool_mixer(h_all, row0, batch, t, tt, hist16, w_pool, scale, pos0, prev_out):
    m = h_all.shape[0]
    nt = t // tt
    rb0 = row0 // tt
    tile0 = POOL_COL0 // POOL_GC
    in_specs = [pl.BlockSpec((tt, POOL_GC), lambda b, i, g=g: (rb0 + b * nt + i, tile0 + g))
                for g in range(len(POOL_WINDOWS))]
    in_specs += [pl.BlockSpec((None, 16, POOL_WIDTH), lambda b, i: (b, 0, 0)),
                pl.BlockSpec((len(POOL_WINDOWS), POOL_GC, POOL_GC), lambda b, i: (0, 0, 0)),
                pl.BlockSpec((1, POOL_WIDTH), lambda b, i: (0, 0))]
    args = [h_all] * len(POOL_WINDOWS) + [hist16, w_pool, scale]
    aliases = {}
    kern = functools.partial(_pool_kernel, tt=tt, pos0=pos0)
    if prev_out is not None:
        in_specs.append(pl.BlockSpec(memory_space=pl.ANY))
        args.append(prev_out)
        aliases = {len(args) - 1: 0}
        kern = _drop_last_input(kern, n_in=len(args))
    return pl.pallas_call(
        kern,
        grid=(batch, nt),
        in_specs=in_specs,
        out_specs=pl.BlockSpec((tt, POOL_WIDTH), lambda b, i: (rb0 + b * nt + i, 0)),
        out_shape=jax.ShapeDtypeStruct((m, POOL_WIDTH), F32),
        scratch_shapes=[pltpu.VMEM((tt + 16, POOL_WIDTH), F32)],
        input_output_aliases=aliases,
        compiler_params=_cparams(2),
        name="pool_mixer",
    )(*args)


def _rw_pre_kernel(*refs, tt):
    c_refs = refs[:RW_COL_TILES]
    (hist_ref, mu_ref, w0_ref, w2_ref, a0_ref, a2_ref, g2_ref,
     rkv_ref, lw_ref, a_ref, g_ref, last_ref) = refs[RW_COL_TILES:]
    ti = pl.program_id(1)

    @pl.when(ti == 0)
    def _():
        last_ref[...] = hist_ref[...]

    c = jnp.concatenate([ref[...] for ref in c_refs], axis=1)
    row = lax.broadcasted_iota(I32, (tt, 1), 0)
    prev = jnp.where(row == 0, last_ref[...], pltpu.roll(c, 1, 0))
    last_ref[...] = c[tt - 1:tt, :]
    xs = c + (prev - c) * mu_ref[...]
    rkv_ref[...] = xs[:, :3 * RW_WIDTH]
    lora = xs[:, 3 * RW_WIDTH:3 * RW_WIDTH + RW_LORA]
    wd = _dot(jnp.tanh(lora).astype(BF16), w2_ref[...].astype(BF16))
    sp = jax.nn.softplus(-(w0_ref[...] + wd))
    lw_ref[...] = -jnp.exp(-sp - 0.5)
    a_ref[...] = jax.nn.sigmoid(a0_ref[...] + _dot(lora.astype(BF16), a2_ref[...].astype(BF16)))
    gd = xs[:, 3 * RW_WIDTH + RW_LORA:]
    g_ref[...] = _dot(jax.nn.sigmoid(gd).astype(BF16), g2_ref[...].astype(BF16))


def _rw_pre(h_all, row0, batch, t, tt, shift_hist, mu, w0, w2p, a0, a2p, g2, prev_outs):
    m = h_all.shape[0]
    nt = t // tt
    rb0 = row0 // tt
    row_map = lambda b, i: (rb0 + b * nt + i, 0)
    const = lambda b, i: (0, 0)
    tile0 = RW_COL0 // RW_COL_TILE
    in_specs = [pl.BlockSpec((tt, RW_COL_TILE), lambda b, i, j=j: (rb0 + b * nt + i, tile0 + j))
                for j in range(RW_COL_TILES)]
    in_specs += [pl.BlockSpec((None, 1, RW_COLS), lambda b, i: (b, 0, 0)),
                pl.BlockSpec((1, RW_COLS), const),
                pl.BlockSpec((1, RW_WIDTH), const),
                pl.BlockSpec((RW_LORA, RW_WIDTH), const),
                pl.BlockSpec((1, RW_WIDTH), const),
                pl.BlockSpec((RW_LORA, RW_WIDTH), const),
                pl.BlockSpec((RW_GATE, RW_WIDTH), const)]
    args = [h_all] * RW_COL_TILES + [shift_hist, mu, w0, w2p, a0, a2p, g2]
    widths = (3 * RW_WIDTH, RW_WIDTH, RW_WIDTH, RW_WIDTH)
    aliases = {}
    kern = functools.partial(_rw_pre_kernel, tt=tt)
    if prev_outs is not None:
        n_real = len(args)
        for j, po in enumerate(prev_outs):
            in_specs.append(pl.BlockSpec(memory_space=pl.ANY))
            args.append(po)
            aliases[n_real + j] = j
        kern = _drop_inputs(kern, n_real, len(prev_outs))
    return pl.pallas_call(
        kern,
        grid=(batch, nt),
        in_specs=in_specs,
        out_specs=[pl.BlockSpec((tt, wd), row_map) for wd in widths],
        out_shape=[jax.ShapeDtypeStruct((m, wd), F32) for wd in widths],
        scratch_shapes=[pltpu.VMEM((1, RW_COLS), F32)],
        input_output_aliases=aliases,
        compiler_params=_cparams(2),
        name="rwkv_pre",
    )(*args)


def _drop_inputs(kern, n_real, n_drop):
    def wrapped(*refs):
        return kern(*refs[:n_real], *refs[n_real + n_drop:])
    return wrapped


def _head_ones():
    r = lax.broadcasted_iota(I32, (LANES, LANES), 0) // HEAD_DIM
    c = lax.broadcasted_iota(I32, (LANES, LANES), 1) // HEAD_DIM
    return (r == c).astype(BF16)


def _head_sum(x, ones_bd):
    hi, lo = _split2(x)
    return _dot(hi, ones_bd) + _dot(lo, ones_bd)


def _stack_heads(x):
    lane = lax.broadcasted_iota(I32, x.shape, 1)
    zero = jnp.zeros_like(x)
    return jnp.concatenate([jnp.where(lane < HEAD_DIM, x, zero), jnp.where(lane >= HEAD_DIM, x, zero)],
                           axis=0)


def _rw_scan_kernel(r_ref, k_ref, v_ref, lw_ref, a_ref, g_ref, kks_ref, ka_ref, rk_ref, gng_ref,
                    gnb_ref, s0_ref, o_ref, sout_ref, st_ref, o_s, lhs_s, ufree_s, prb_s, akvo_s, bkt_s,
                    v_s, e_s, *, tb, c):
    ti = pl.program_id(2)
    n = 2 * c
    nchunk = tb // c
    tiles = range(RW_TILES)
    lanes = [slice(p * LANES, (p + 1) * LANES) for p in tiles]

    @pl.when(ti == 0)
    def _():
        st_ref[...] = s0_ref[...]

    ones_bd = _head_ones()
    ri = lax.broadcasted_iota(I32, (tb, tb), 0)
    ci = lax.broadcasted_iota(I32, (tb, tb), 1)
    l_chunk = jnp.logical_and(ri >= ci, ri // c == ci // c).astype(BF16)
    v_all, e_in, al_all, be_all, kc_all, rc_all, bonus = [], [], [], [], [], [], []
    for p in tiles:
        r = r_ref[:, lanes[p]]
        k0 = k_ref[:, lanes[p]]
        v = v_ref[:, lanes[p]]
        lw = lw_ref[:, lanes[p]]
        a = a_ref[:, lanes[p]]
        kk = k0 * kks_ref[:, lanes[p]]
        kk = kk * lax.rsqrt(jnp.maximum(_head_sum(kk * kk, ones_bd), 1e-24))
        kmod = k0 * (1.0 + (a - 1.0) * ka_ref[:, lanes[p]])
        lw_hi, lw_lo = _split2(lw)
        gcum = _dot(l_chunk, lw_hi) + _dot(l_chunk, lw_lo)
        e = jnp.exp(gcum)
        e_neg = jnp.exp(-gcum)
        v_all.append(v)
        e_in.append(e)
        al_all.append((-kk * jnp.exp(gcum - lw)).astype(BF16))
        be_all.append((kk * a * e_neg).astype(BF16))
        kc_all.append((kmod * e_neg).astype(BF16))
        rc_all.append((r * e).astype(BF16))
        bonus.append(_head_sum(r * kmod * rk_ref[:, lanes[p]], ones_bd) * v)

    rn = lax.broadcasted_iota(I32, (n, n), 0)
    cn = lax.broadcasted_iota(I32, (n, n), 1)
    same_head = rn // c == cn // c
    strict = jnp.logical_and(same_head, cn < rn).astype(F32)
    incl = jnp.logical_and(same_head, cn <= rn).astype(F32)
    eye = (rn == cn).astype(F32)

    def prepare(group):
        each = lambda f, *lists: [f(*xs) for xs in zip(*lists)]
        rows = [(p, slice(j * c, (j + 1) * c)) for p, j in group]
        al = [_stack_heads(al_all[p][r_]) for p, r_ in rows]
        rc = [_stack_heads(rc_all[p][r_]) for p, r_ in rows]
        bk = [jnp.concatenate([_stack_heads(be_all[p][r_]), _stack_heads(kc_all[p][r_])], axis=0)
              for p, r_ in rows]
        vs = [_stack_heads(v_all[p][r_]).astype(BF16) for p, r_ in rows]
        m4 = each(lambda al_, rc_, bk_: _dot_nt(jnp.concatenate([al_, rc_], axis=0), bk_), al, rc, bk)
        a_ab = [m[:n, :n] * strict for m in m4]
        lower = [jnp.concatenate([m[:n, n:] * strict, m[n:, n:] * incl], axis=0).astype(BF16) for m in m4]
        p_rb = [(m[n:, :n] * incl).astype(BF16) for m in m4]
        akv = each(_dot, lower, vs)
        tinv = [eye + a_ for a_ in a_ab]
        p_bf = [a_.astype(BF16) for a_ in a_ab]
        p_bf = [_dot(p_, p_).astype(BF16) for p_ in p_bf]
        span = 2
        while span < c:
            if 2 * span < c:
                both = each(lambda p_, t_: _dot(p_, jnp.concatenate([t_.astype(BF16), p_], axis=1)),
                            p_bf, tinv)
                tinv = each(lambda t_, b_: t_ + b_[:, :n], tinv, both)
                p_bf = [b_[:, n:].astype(BF16) for b_ in both]
            else:
                tinv = each(lambda t_, p_: t_ + _dot(p_, t_.astype(BF16)), tinv, p_bf)
            span *= 2
        wu = each(lambda t_, al_, akv_: _dot(t_.astype(BF16),
                                             jnp.concatenate([al_, akv_[:n].astype(BF16)], axis=1)),
                  tinv, al, akv)
        for i, (p, j) in enumerate(group):
            lhs_s[p, j] = jnp.concatenate([wu[i][:, :LANES].astype(BF16), rc[i]], axis=0)
            ufree_s[p, j] = wu[i][:, LANES:]
            prb_s[p, j] = p_rb[i]
            akvo_s[p, j] = akv[i][n:]
            bkt_s[p, j] = bk[i].astype(F32).T.astype(BF16)
            v_s[p, j] = vs[i]
            e_last = e_in[p][(j + 1) * c - 1:(j + 1) * c, :]
            e_s[p, j] = jnp.broadcast_to(e_last, (LANES, LANES)).T

    per_group = max(1, RW_PREP_GROUP // RW_TILES)
    for j0 in range(0, nchunk, per_group):
        prepare([(p, j) for j in range(j0, min(j0 + per_group, nchunk)) for p in tiles])

    def advance(j, sts):
        rows = pl.ds(pl.multiple_of(j * c, c), c)
        ws = [_dot(lhs_s[p, j], sts[p].astype(BF16)) for p in tiles]
        u_bf = [(ws[p][:n] + ufree_s[p, j]).astype(BF16) for p in tiles]
        o_bd = [ws[p][n:] + _dot(prb_s[p, j], u_bf[p]) + akvo_s[p, j] for p in tiles]
        upd = [_dot(bkt_s[p, j], jnp.concatenate([u_bf[p], v_s[p, j]], axis=0)) for p in tiles]
        for p in tiles:
            o_s[rows, lanes[p]] = o_bd[p][:c] + o_bd[p][c:]
        return tuple(e_s[p, j] * (sts[p] + upd[p]) for p in tiles)

    sts = lax.fori_loop(0, nchunk, advance, tuple(st_ref[p] for p in tiles))
    for p in tiles:
        st_ref[p] = sts[p]
        sout_ref[p] = sts[p]
        o = o_s[:, lanes[p]]
        mu_o = _head_sum(o, ones_bd) * (1.0 / HEAD_DIM)
        oc = o - mu_o
        var_o = _head_sum(oc * oc, ones_bd) * (1.0 / HEAD_DIM)
        on = oc * lax.rsqrt(var_o + GN_EPS) * gng_ref[:, lanes[p]] + gnb_ref[:, lanes[p]]
        o_ref[:, lanes[p]] = (on + bonus[p]) * g_ref[:, lanes[p]]


def _rw_scan(rkv, lw, a, g, row0, batch, t, tb, c, params, wkv0, prev_out):
    m = rkv.shape[0]
    nt = t // tb
    rb0 = row0 // tb
    n = 2 * c
    nchunk = tb // c
    wide = RW_TILES * LANES
    groups = HEAD_PAIRS // RW_TILES

    def col(off):
        return lambda b, p, i: (rb0 + b * nt + i, off + p)

    s_t = jnp.swapaxes(wkv0, -1, -2).reshape(batch, HEAD_PAIRS, 2, HEAD_DIM, HEAD_DIM)
    st0 = jnp.einsum('bphkv,hg->bphkgv', s_t, jnp.eye(2, dtype=F32)).reshape(batch, HEAD_PAIRS, LANES, LANES)
    pspec = pl.BlockSpec((1, wide), lambda b, p, i: (0, p))
    sspec = pl.BlockSpec((None, RW_TILES, LANES, LANES), lambda b, p, i: (b, p, 0, 0))
    in_specs = [pl.BlockSpec((tb, wide), col(0)), pl.BlockSpec((tb, wide), col(groups)),
                pl.BlockSpec((tb, wide), col(2 * groups)),
                pl.BlockSpec((tb, wide), col(0)), pl.BlockSpec((tb, wide), col(0)),
                pl.BlockSpec((tb, wide), col(0))] + [pspec] * 5 + [sspec]
    args = [rkv, rkv, rkv, lw, a, g] + list(params) + [st0]
    aliases = {}
    kern = functools.partial(_rw_scan_kernel, tb=tb, c=c)
    if prev_out is not None:
        n_real = len(args)
        in_specs.append(pl.BlockSpec(memory_space=pl.ANY))
        args.append(prev_out)
        aliases = {n_real: 0}
        kern = _drop_inputs(kern, n_real, 1)
    per = (RW_TILES, nchunk)
    scratch = [pltpu.VMEM((RW_TILES, LANES, LANES), F32), pltpu.VMEM((tb, wide), F32)]
    scratch += [pltpu.VMEM(per + (2 * n, LANES), BF16), pltpu.VMEM(per + (n, LANES), F32),
                pltpu.VMEM(per + (n, n), BF16), pltpu.VMEM(per + (n, LANES), F32),
                pltpu.VMEM(per + (LANES, 2 * n), BF16), pltpu.VMEM(per + (n, LANES), BF16),
                pltpu.VMEM(per + (LANES, LANES), F32)]
    o_c, st = pl.pallas_call(
        kern,
        grid=(batch, groups, nt),
        in_specs=in_specs,
        out_specs=[pl.BlockSpec((tb, wide), col(0)), sspec],
        out_shape=[jax.ShapeDtypeStruct((m, RW_WIDTH), F32),
                   jax.ShapeDtypeStruct((batch, HEAD_PAIRS, LANES, LANES), F32)],
        scratch_shapes=scratch,
        input_output_aliases=aliases,
        compiler_params=_cparams(3),
        name="rwkv_scan",
    )(*args)
    st = st.reshape(batch, HEAD_PAIRS, 2, HEAD_DIM, 2, HEAD_DIM)
    wkv = jnp.stack([st[:, :, 0, :, 0, :], st[:, :, 1, :, 1, :]], axis=2)
    return o_c, jnp.swapaxes(wkv, -1, -2).reshape(batch, RW_HEADS, HEAD_DIM, HEAD_DIM)


def _outproj_kernel(oa_ref, ob_ref, oc_ref, x_ref, w_ref, g_ref, b_ref, y_ref):
    mix = _dot(oa_ref[...].astype(BF16), w_ref[0:SB_WIDTH, :])
    mix += _dot(ob_ref[...].astype(BF16), w_ref[SB_WIDTH:SB_WIDTH + POOL_WIDTH, :])
    mix += _dot(oc_ref[...].astype(BF16), w_ref[SB_WIDTH + POOL_WIDTH:, :])
    y_ref[...] = _layer_norm(DN_ALPHA * x_ref[...] + mix, g_ref[...], b_ref[...])


def _outproj_ln(oa, ob, oc, x, w_bf, g, b, tm):
    m, d = x.shape
    row = lambda i: (i, 0)
    const = lambda i: (0, 0)
    return pl.pallas_call(
        _outproj_kernel,
        grid=(m // tm,),
        in_specs=[pl.BlockSpec((tm, SB_WIDTH), row), pl.BlockSpec((tm, POOL_WIDTH), row),
                  pl.BlockSpec((tm, RW_WIDTH), row), pl.BlockSpec((tm, d), row),
                  pl.BlockSpec(w_bf.shape, const), pl.BlockSpec((1, d), const),
                  pl.BlockSpec((1, d), const)],
        out_specs=pl.BlockSpec((tm, d), row),
        out_shape=jax.ShapeDtypeStruct((m, d), F32),
        compiler_params=_cparams(1),
        name="outproj_ln",
    )(oa, ob, oc, x, w_bf, g, b)


def _router_kernel(x_ref, w_ref, b_ref, idx_ref, gate_ref, rank_ref, cnt_ref, run_ref, *, tm):
    @pl.when(pl.program_id(0) == 0)
    def _():
        run_ref[...] = jnp.zeros_like(run_ref)

    logits = _dot(x_ref[...].astype(BF16), w_ref[...].astype(BF16)) + b_ref[...]
    lane = lax.broadcasted_iota(I32, (tm, N_EXPERTS), 1).astype(F32)
    work = logits
    vals, idxs = [], []
    for _ in range(TOP_K):
        top = jnp.max(work, axis=1, keepdims=True)
        idx = jnp.min(jnp.where(work == top, lane, float(N_EXPERTS)), axis=1, keepdims=True)
        vals.append(top)
        idxs.append(idx)
        work = jnp.where(lane == idx, -jnp.inf, work)
    exps = [jnp.exp(v - vals[0]) for v in vals]
    total = exps[0] + exps[1] + exps[2] + exps[3]
    out_lane = lax.broadcasted_iota(I32, (tm, LANES), 1)
    lane_f = out_lane.astype(F32)
    onehots = [(lane_f == idxs[k]).astype(F32) for k in range(TOP_K)]
    chosen = onehots[0] + onehots[1] + onehots[2] + onehots[3]
    earlier = (lax.broadcasted_iota(I32, (tm, tm), 1) < lax.broadcasted_iota(I32, (tm, tm), 0)).astype(BF16)
    base = run_ref[...] + _dot(earlier, chosen.astype(BF16))
    run_ref[...] = run_ref[...] + jnp.sum(chosen, axis=0, keepdims=True)
    cnt_ref[...] = run_ref[...].astype(I32)
    idx_out = jnp.zeros((tm, LANES), F32)
    gate_out = jnp.zeros((tm, LANES), F32)
    rank_out = jnp.zeros((tm, LANES), F32)
    for k in range(TOP_K):
        idx_out = jnp.where(out_lane == k, idxs[k], idx_out)
        gate_out = jnp.where(out_lane == k, exps[k] / total, gate_out)
        rank_out = jnp.where(out_lane == k, jnp.sum(onehots[k] * base, axis=1, keepdims=True), rank_out)
    idx_ref[...] = idx_out.astype(I32)
    gate_ref[...] = gate_out
    rank_ref[...] = rank_out.astype(I32)


def _router(x, w, b, tm):
    m, d = x.shape
    return pl.pallas_call(
        functools.partial(_router_kernel, tm=tm),
        grid=(m // tm,),
        in_specs=[pl.BlockSpec((tm, d), lambda i: (i, 0)),
                  pl.BlockSpec((d, N_EXPERTS), lambda i: (0, 0)),
                  pl.BlockSpec((1, N_EXPERTS), lambda i: (0, 0))],
        out_specs=[pl.BlockSpec((tm, LANES), lambda i: (i, 0))] * 3 + [pl.BlockSpec((1, LANES), lambda i: (0, 0))],
        out_shape=[jax.ShapeDtypeStruct((m, LANES), I32), jax.ShapeDtypeStruct((m, LANES), F32),
                   jax.ShapeDtypeStruct((m, LANES), I32), jax.ShapeDtypeStruct((1, LANES), I32)],
        scratch_shapes=[pltpu.VMEM((1, LANES), F32)],
        compiler_params=_cparams(1),
        name="router",
    )(x, w, b)


def _row_copy(src_hbm, dst_ref, sem, src_row, dst_row):
    return pltpu.make_async_copy(src_hbm.at[pl.ds(src_row, 1), :], dst_ref.at[pl.ds(dst_row, 1), :], sem)


GATHER_UNROLL = 8


def _gather_kernel(tok_ref, tok_next_ref, nu_ref, x_hbm, o_ref, rows_ref, sem, *, bm):
    i = pl.program_id(0)
    n_used = nu_ref[0]

    def start_block(toks, slot):
        def body(k, carry):
            for u in range(GATHER_UNROLL):
                r = k * GATHER_UNROLL + u
                _row_copy(x_hbm, rows_ref.at[slot], sem.at[slot], toks[0, 0, r], r).start(priority=u % 2)
            return carry

        lax.fori_loop(0, bm // GATHER_UNROLL, body, 0)

    @pl.when(i == 0)
    def _():
        start_block(tok_ref, 0)

    @pl.when(i + 1 < n_used)
    def _():
        start_block(tok_next_ref, (i + 1) % 2)

    @pl.when(i < n_used)
    def _():
        slot = i % 2

        def wait(k, carry):
            for u in range(GATHER_UNROLL):
                _row_copy(x_hbm, rows_ref.at[slot], sem.at[slot], 0, k * GATHER_UNROLL + u).wait()
            return carry

        lax.fori_loop(0, bm // GATHER_UNROLL, wait, 0)
        o_ref[...] = rows_ref[slot].astype(BF16)


def _gather_rows(x, buf_tok, n_used, bm):
    d = x.shape[1]
    nb = buf_tok.shape[0] // bm
    toks = buf_tok.reshape(nb, 1, bm)
    return pl.pallas_call(
        functools.partial(_gather_kernel, bm=bm),
        grid=(nb,),
        in_specs=[pl.BlockSpec((1, 1, bm), lambda i: (i, 0, 0), memory_space=pltpu.SMEM),
                  pl.BlockSpec((1, 1, bm), lambda i: (jnp.minimum(i + 1, nb - 1), 0, 0),
                               memory_space=pltpu.SMEM),
                  pl.BlockSpec(memory_space=pltpu.SMEM),
                  pl.BlockSpec(memory_space=pl.ANY)],
        out_specs=pl.BlockSpec((bm, d), lambda i: (i, 0)),
        out_shape=jax.ShapeDtypeStruct((nb * bm, d), BF16),
        scratch_shapes=[pltpu.VMEM((2, bm, d), F32), pltpu.SemaphoreType.DMA((2,))],
        compiler_params=_cparams(1),
        name="moe_gather",
    )(toks, toks, n_used, x)


def _step_table(block_e, n_used):
    nb = block_e.shape[0]
    i = jnp.arange(nb, dtype=I32)
    valid = i < n_used[0]
    first = jnp.logical_and(valid, jnp.logical_or(i == 0, block_e != jnp.roll(block_e, 1)))
    first_pos = jnp.where(first, i, nb)
    next_pos = jnp.concatenate([lax.cummin(first_pos, reverse=True)[1:], jnp.full((1,), nb, I32)])
    next_e = jnp.where(next_pos < nb, block_e[jnp.minimum(next_pos, nb - 1)], -1)
    return jnp.stack([block_e, first.astype(I32), next_e.astype(I32), jnp.zeros_like(block_e)],
                     axis=1).reshape(nb * 4)


def _weight_copies(w_hbm, wbuf, sem, layer, expert, cols, tn):
    return [pltpu.make_async_copy(w_hbm.at[layer, expert, :, pl.ds(pl.multiple_of(col, tn), tn)],
                                  wbuf.at[h], sem.at[h]) for h, col in enumerate(cols)]


def _fetch_expert_tile(tbl_ref, w_hbm, wbuf, wbf, sem, layer, cols, tn):
    i = pl.program_id(1)
    expert = tbl_ref[4 * i]
    next_expert = tbl_ref[4 * i + 2]

    @pl.when(tbl_ref[4 * i + 1] == 1)
    def _():
        @pl.when(i == 0)
        def _():
            for cp in _weight_copies(w_hbm, wbuf, sem, layer, expert, cols, tn):
                cp.start()

        for cp in _weight_copies(w_hbm, wbuf, sem, layer, expert, cols, tn):
            cp.wait()
        for h in range(len(cols)):
            wbf[h] = wbuf[h].astype(BF16)

        @pl.when(next_expert >= 0)
        def _():
            for cp in _weight_copies(w_hbm, wbuf, sem, layer, next_expert, cols, tn):
                cp.start()


def _expert_up_kernel(tbl_ref, nu_ref, x_ref, w_hbm, bg_ref, bl_ref, act_ref, wbuf, wbf, sem, *, layer, tn):
    j = pl.program_id(0)
    nj = pl.num_programs(0)

    @pl.when(pl.program_id(1) < nu_ref[0])
    def _():
        _fetch_expert_tile(tbl_ref, w_hbm, wbuf, wbf, sem, layer, (j * tn, (nj + j) * tn), tn)
        x = x_ref[...]
        h_glu = _dot(x, wbf[0]) + bg_ref[...]
        h_lin = _dot(x, wbf[1]) + bl_ref[...]
        h_glu = jnp.minimum(h_glu, SWIGLU_LIMIT)
        h_lin = jnp.clip(h_lin, -SWIGLU_LIMIT, SWIGLU_LIMIT)
        act = h_glu * jax.nn.sigmoid(SWIGLU_ALPHA * h_glu) * (h_lin + 1.0)
        act_ref[...] = act.astype(BF16)


def _expert_up(xb, w1, b1, layer, tbl, n_used, bm, tn):
    rows, d = xb.shape
    de = w1.shape[3] // 2
    nj = de // tn
    nb = rows // bm

    def blk(i, nu):
        return jnp.minimum(i, nu[0] - 1)

    grid_spec = pltpu.PrefetchScalarGridSpec(
        num_scalar_prefetch=2,
        grid=(nj, nb),
        in_specs=[pl.BlockSpec((bm, d), lambda j, i, tb, nu: (blk(i, nu), 0)),
                  pl.BlockSpec(memory_space=pl.ANY),
                  pl.BlockSpec((None, None, 1, tn), lambda j, i, tb, nu: (layer, tb[4 * blk(i, nu)], 0, j)),
                  pl.BlockSpec((None, None, 1, tn),
                               lambda j, i, tb, nu: (layer, tb[4 * blk(i, nu)], 0, nj + j))],
        out_specs=pl.BlockSpec((bm, tn), lambda j, i, tb, nu: (blk(i, nu), j)),
        scratch_shapes=[pltpu.VMEM((2, d, tn), F32), pltpu.VMEM((2, d, tn), BF16),
                        pltpu.SemaphoreType.DMA((2,))],
    )
    return pl.pallas_call(
        functools.partial(_expert_up_kernel, layer=layer, tn=tn),
        grid_spec=grid_spec,
        out_shape=jax.ShapeDtypeStruct((rows, de), BF16),
        compiler_params=_cparams(2),
        name="moe_up",
    )(tbl, n_used, xb, w1, b1, b1)


def _expert_down_kernel(tbl_ref, nu_ref, a_ref, w_hbm, b_ref, y_ref, wbuf, wbf, sem, *, layer, tn):
    j = pl.program_id(0)

    @pl.when(pl.program_id(1) < nu_ref[0])
    def _():
        _fetch_expert_tile(tbl_ref, w_hbm, wbuf, wbf, sem, layer, (j * tn,), tn)
        y_ref[...] = _dot(a_ref[...], wbf[0]) + b_ref[...]


def _expert_down(act, w2, b2, layer, tbl, n_used, bm, tn):
    rows, de = act.shape
    d = w2.shape[3]
    nj = d // tn
    nb = rows // bm

    def blk(i, nu):
        return jnp.minimum(i, nu[0] - 1)

    grid_spec = pltpu.PrefetchScalarGridSpec(
        num_scalar_prefetch=2,
        grid=(nj, nb),
        in_specs=[pl.BlockSpec((bm, de), lambda j, i, tb, nu: (blk(i, nu), 0)),
                  pl.BlockSpec(memory_space=pl.ANY),
                  pl.BlockSpec((None, None, 1, tn), lambda j, i, tb, nu: (layer, tb[4 * blk(i, nu)], 0, j))],
        out_specs=pl.BlockSpec((bm, tn), lambda j, i, tb, nu: (blk(i, nu), j)),
        scratch_shapes=[pltpu.VMEM((1, de, tn), F32), pltpu.VMEM((1, de, tn), BF16),
                        pltpu.SemaphoreType.DMA((1,))],
    )
    return pl.pallas_call(
        functools.partial(_expert_down_kernel, layer=layer, tn=tn),
        grid_spec=grid_spec,
        out_shape=jax.ShapeDtypeStruct((rows, d), F32),
        compiler_params=_cparams(2),
        name="moe_down",
    )(tbl, n_used, act, w2, b2)


def _combine_kernel(dest_ref, dest_next_ref, y_hbm, gate_ref, x_ref, g_ref, b_ref, o_ref, rows_ref, sem,
                    *, tm):
    i = pl.program_id(0)

    def start_block(dests, slot):
        def body(r, carry):
            for k in range(TOP_K):
                _row_copy(y_hbm, rows_ref.at[slot, k], sem.at[slot], dests[0, 0, r * TOP_K + k],
                          r).start(priority=k % 2)
            return carry

        lax.fori_loop(0, tm, body, 0)

    @pl.when(i == 0)
    def _():
        start_block(dest_ref, 0)

    @pl.when(i + 1 < pl.num_programs(0))
    def _():
        start_block(dest_next_ref, (i + 1) % 2)

    slot = i % 2

    def wait(r, carry):
        for k in range(TOP_K):
            _row_copy(y_hbm, rows_ref.at[slot, k], sem.at[slot], 0, r).wait()
        return carry

    lax.fori_loop(0, tm, wait, 0)
    gates = gate_ref[...]
    ffn = gates[:, 0:1] * rows_ref[slot, 0]
    for k in range(1, TOP_K):
        ffn += gates[:, k:k + 1] * rows_ref[slot, k]
    o_ref[...] = _layer_norm(DN_ALPHA * x_ref[...] + ffn, g_ref[...], b_ref[...])


def _combine_ln(y, dest, gates, x, g, b, tm):
    m, d = x.shape
    nb = m // tm
    dests = dest.reshape(nb, 1, tm * TOP_K)
    return pl.pallas_call(
        functools.partial(_combine_kernel, tm=tm),
        grid=(nb,),
        in_specs=[pl.BlockSpec((1, 1, tm * TOP_K), lambda i: (i, 0, 0), memory_space=pltpu.SMEM),
                  pl.BlockSpec((1, 1, tm * TOP_K), lambda i: (jnp.minimum(i + 1, nb - 1), 0, 0),
                               memory_space=pltpu.SMEM),
                  pl.BlockSpec(memory_space=pl.ANY),
                  pl.BlockSpec((tm, LANES), lambda i: (i, 0)),
                  pl.BlockSpec((tm, d), lambda i: (i, 0)),
                  pl.BlockSpec((1, d), lambda i: (0, 0)),
                  pl.BlockSpec((1, d), lambda i: (0, 0))],
        out_specs=pl.BlockSpec((tm, d), lambda i: (i, 0)),
        out_shape=jax.ShapeDtypeStruct((m, d), F32),
        scratch_shapes=[pltpu.VMEM((2, TOP_K, tm, d), F32), pltpu.SemaphoreType.DMA((2,))],
        compiler_params=_cparams(1),
        name="moe_combine_ln",
    )(dests, dests, y, gates, x, g, b)


def _routing_tables(top_i, rank, counts, bm):
    n = top_i.shape[0]
    nk = n * TOP_K
    padded = (counts + bm - 1) // bm * bm
    pad_end = jnp.cumsum(padded)
    pad_start = pad_end - padded
    dest = (pad_start[top_i] + rank).astype(I32).reshape(nk)
    nb = (nk + N_EXPERTS * (bm - 1)) // bm
    buf_tok = jnp.zeros((nb * bm,), I32).at[dest].set(jnp.arange(nk, dtype=I32) // TOP_K)
    block_row = jnp.arange(nb, dtype=I32)[:, None] * bm
    block_e = jnp.minimum(jnp.sum((pad_end[None, :] <= block_row).astype(I32), axis=1), N_EXPERTS - 1)
    n_used = (pad_end[-1:] // bm).astype(I32)
    return dest, buf_tok, _step_table(block_e, n_used), n_used


def _moe_ln(x, lp, big, layer):
    idx, gates, rank, counts = _router(x, lp['w_router'], lp['b_router'].reshape(1, N_EXPERTS), tm=256)
    dest, buf_tok, steps, n_used = _routing_tables(idx[:, :TOP_K], rank[:, :TOP_K],
                                                     counts[0, :N_EXPERTS], MOE_BM)
    xb = _gather_rows(x, buf_tok, n_used, MOE_BM)
    nl, e, d, de2 = big['w_e1'].shape
    act = _expert_up(xb, big['w_e1'], big['b_e1'].reshape(nl, e, 1, de2), layer, steps, n_used,
                     MOE_BM, tn=1024)
    y = _expert_down(act, big['w_e2'], big['b_e2'].reshape(nl, e, 1, d), layer, steps, n_used,
                     MOE_BM, tn=d)
    return _combine_ln(y, dest, gates, x, lp['ln_ffn_g'].reshape(1, d), lp['ln_ffn_b'].reshape(1, d),
                       tm=128)


def _layer(x, lp, big, layer, st, n_p, bp, tp, bs, ts):
    d = x.shape[1]
    h = _inproj(x, big['w_in'], layer, tm=384, tn=1792)
    qkv_w = 3 * SB_WIDTH

    o_a = _sb_attention(h, 0, bp, tp, 256, None, None)
    hist = (st['cache_k'].reshape(bs, -1, SB_WIDTH), st['cache_v'].reshape(bs, -1, SB_WIDTH))
    o_a = _sb_attention(h, n_p, bs, ts, ts, hist, o_a)
    past = hist[0].shape[1]

    w_pool = lp['pool_w']
    scale = lp['pool_scale'].reshape(1, POOL_WIDTH)
    o_b = _pool_mixer(h, 0, bp, tp, 512, jnp.zeros((bp, 16, POOL_WIDTH), F32), w_pool, scale, 0, None)
    hist16 = jnp.pad(st['state_pool'], ((0, 0), (1, 0), (0, 0)))
    o_b = _pool_mixer(h, n_p, bs, ts, ts, hist16, w_pool, scale, past, o_b)

    zero_lora = jnp.zeros((RW_LORA // 2, RW_WIDTH), F32)
    w2p = jnp.concatenate([lp['rw_w2'], zero_lora], axis=0)
    a2p = jnp.concatenate([zero_lora, lp['rw_a2']], axis=0)
    pre_args = (lp['rw_mu'].reshape(1, RW_COLS), lp['rw_w0'].reshape(1, RW_WIDTH), w2p,
                lp['rw_a0'].reshape(1, RW_WIDTH), a2p, lp['rw_g2'])
    pre = _rw_pre(h, 0, bp, tp, 256, jnp.zeros((bp, 1, RW_COLS), F32), *pre_args, None)
    rkv, lw, a, g = _rw_pre(h, n_p, bs, ts, ts, st['state_shift'], *pre_args, pre)
    prm = tuple(lp[k].reshape(1, RW_WIDTH) for k in ('rw_kk', 'rw_ka', 'rw_rk', 'rw_gn_g', 'rw_gn_b'))
    o_c, wkv_p = _rw_scan(rkv, lw, a, g, 0, bp, tp, 512, 64, prm,
                          jnp.zeros((bp, RW_HEADS, HEAD_DIM, HEAD_DIM), F32), None)
    o_c, wkv_s = _rw_scan(rkv, lw, a, g, n_p, bs, ts, ts, ts, prm, st['state_wkv'], o_c)

    x1 = _outproj_ln(o_a, o_b, o_c, x, lp['w_out'].astype(BF16), lp['ln_mix_g'].reshape(1, d),
                     lp['ln_mix_b'].reshape(1, d), tm=256)
    x2 = _moe_ln(x1, lp, big, layer)

    def states(r0, b, t, wkv):
        r1 = r0 + b * t
        k_new = h[r0:r1, SB_WIDTH:2 * SB_WIDTH].reshape(b, t, SB_HEADS, HEAD_DIM)
        v_new = h[r0:r1, 2 * SB_WIDTH:qkv_w].reshape(b, t, SB_HEADS, HEAD_DIM)
        pool_new = jnp.stack([h[r0 + (i + 1) * t - POOL_HIST:r0 + (i + 1) * t, qkv_w:qkv_w + POOL_WIDTH]
                              for i in range(b)])
        shift_new = h[r0 + t - 1:r1:t, qkv_w + POOL_WIDTH:].reshape(b, 1, RW_COLS)
        return k_new, v_new, pool_new, shift_new, wkv

    return x2, states(0, bp, tp, wkv_p), states(n_p, bs, ts, wkv_s)


def kernel(x_prompt, x_sample, cache_k, cache_v, state_pool, state_shift, state_wkv, w_in, w_out, ln_mix_g, ln_mix_b, ln_ffn_g, ln_ffn_b, pool_w, pool_scale, rw_mu, rw_w0, rw_w2, rw_a0, rw_a2, rw_g2, rw_kk, rw_ka, rw_rk, rw_gn_g, rw_gn_b, w_router, b_router, w_e1, b_e1, w_e2, b_e2):
    bp, tp, d = x_prompt.shape
    bs, ts, _ = x_sample.shape
    n_p = bp * tp
    x = jnp.concatenate([x_prompt.reshape(n_p, d), x_sample.reshape(bs * ts, d)], axis=0)
    big = dict(w_in=w_in, w_e1=w_e1, b_e1=b_e1, w_e2=w_e2, b_e2=b_e2)
    weights = dict(w_out=w_out, ln_mix_g=ln_mix_g, ln_mix_b=ln_mix_b, ln_ffn_g=ln_ffn_g,
                   ln_ffn_b=ln_ffn_b, pool_w=pool_w, pool_scale=pool_scale, rw_mu=rw_mu, rw_w0=rw_w0,
                   rw_w2=rw_w2, rw_a0=rw_a0, rw_a2=rw_a2, rw_g2=rw_g2, rw_kk=rw_kk, rw_ka=rw_ka,
                   rw_rk=rw_rk, rw_gn_g=rw_gn_g, rw_gn_b=rw_gn_b, w_router=w_router,
                   b_router=b_router)
    st_p, st_s = [], []
    for l in range(w_in.shape[0]):
        lp = {k: v[l] for k, v in weights.items()}
        st = dict(cache_k=cache_k[l], cache_v=cache_v[l], state_pool=state_pool[l],
                  state_shift=state_shift[l], state_wkv=state_wkv[l])
        x, sp, ss = _layer(x, lp, big, l, st, n_p, bp, tp, bs, ts)
        st_p.append(sp)
        st_s.append(ss)
    y_prompt = x[:n_p].reshape(bp, tp, d)
    y_sample = x[n_p:].reshape(bs, ts, d)
    k_p, v_p, pool_p, shift_p, wkv_p = (jnp.stack([s[i] for s in st_p]) for i in range(5))
    k_s, v_s, pool_s, shift_s, wkv_s = (jnp.stack([s[i] for s in st_s]) for i in range(5))
    return (y_prompt, y_sample, k_p, v_p, pool_p, shift_p, wkv_p, k_s, v_s, pool_s, shift_s, wkv_s)
```
